```python
import math
import jax
import jax.numpy as jnp
from jax import lax
import numpy as np

D_MODEL = 2048
BATCH = 32
SEQ = 256
DEPTH = 2
DEC_BATCH = 8
DEC_SEQ = 2048
PAST_LEN = 256

GRID_W = 64
EPS = 1e-6
NEG_BIG = -1e30
LB_FLOOR = 1e-30
HY_WIDTH = 1024
HY_BANDS = 16
HY_EMB_DIM = 1 + 2 * HY_BANDS
HY_HIDDEN = 64
HY_SLOW_DECAY = math.log(1e-2) / 1.5
HY_FAST_DECAY = math.log(1e-2) / 0.3
N_HEADS = 8
N_KV_HEADS = 2
GROUP = N_HEADS // N_KV_HEADS
HEAD_DIM = 128
ATTN_WIDTH = N_HEADS * HEAD_DIM
KV_WIDTH = N_KV_HEADS * HEAD_DIM
WINDOW = 128
BLOCK = 128
ROPE_BASE = 10000.0
ATTN_SCALE = HEAD_DIM ** -0.5
HG_HEADS = 8
HG_DK = 128
HG_DV = 128
HG_WIDTH = HG_HEADS * HG_DK
HG_CHUNK = 32
IN_SPLITS = (3 * HY_WIDTH, ATTN_WIDTH, KV_WIDTH, KV_WIDTH, HG_WIDTH, HG_WIDTH, HG_WIDTH, HG_WIDTH, HG_WIDTH, D_MODEL, D_MODEL, D_MODEL)
IN_WIDTH = 3 * HY_WIDTH + ATTN_WIDTH + 2 * KV_WIDTH + 5 * HG_WIDTH + 3 * D_MODEL
D_FF = 5632
N_EXPERTS = 8
TOP_K = 2
N_DENSE = (DEPTH + 1) // 2
N_MOE = DEPTH // 2

kernel_name = 'hybrid_diffusion_trunk_step'


def rmsnorm(x, w):
    xf = x.astype(jnp.float32)
    y = xf * lax.rsqrt(jnp.mean(jnp.square(xf), axis=-1, keepdims=True) + EPS)
    return (y * w.astype(jnp.float32)).astype(x.dtype)


def short_conv3(u, w, b):
    up = jnp.pad(u, ((0, 0), (1, 1), (0, 0)))
    return up[:, :-2] * w[0] + up[:, 1:-1] * w[1] + up[:, 2:] * w[2] + b


def hyena_filter(L, w1, b1, freq, w2, b2, w3, decay):
    t = jnp.linspace(0.0, 1.0, L, dtype=jnp.float32)[:, None]
    pos = jnp.arange(L, dtype=jnp.float32)[:, None]
    bands = jnp.linspace(1e-4, HY_BANDS - 1.0, HY_BANDS, dtype=jnp.float32)[None, :]
    ang = (2.0 * math.pi / L) * pos * bands
    z = jnp.concatenate([t, jnp.cos(ang), -jnp.sin(ang)], axis=-1)
    h = jnp.sin(freq * (z @ w1 + b1))
    h = jnp.sin(freq * (h @ w2 + b2))
    h = (h @ w3).astype(jnp.float32) * jnp.exp(-t * jnp.abs(decay.reshape(-1)).astype(jnp.float32))
    h_fwd, h_bwd = h[:, :HY_WIDTH], h[:, HY_WIDTH:]
    zero = jnp.zeros((1, HY_WIDTH), jnp.float32)
    return jnp.concatenate([h_fwd, zero, h_bwd[1:][::-1]], axis=0)


def fft_long_conv(u, kern):
    L = u.shape[1]
    uf = jnp.fft.rfft(u, n=2 * L, axis=1)
    kf = jnp.fft.rfft(kern, n=2 * L, axis=0)
    return jnp.fft.irfft(uf * kf[None], n=2 * L, axis=1)[:, :L]


def axial_rope_tables(L):
    rows = L // GRID_W
    row = jnp.repeat(jnp.arange(rows, dtype=jnp.float32), GRID_W)
    col = jnp.tile(jnp.arange(GRID_W, dtype=jnp.float32), rows)
    quarter = HEAD_DIM // 4
    inv = ROPE_BASE ** (-jnp.arange(quarter, dtype=jnp.float32) / quarter)
    ar = row[:, None] * inv
    ac = col[:, None] * inv
    ang = jnp.concatenate([ar, ar, ac, ac], axis=-1)
    return jnp.cos(ang)[None, :, None, :], jnp.sin(ang)[None, :, None, :]


def apply_axial_rope(x, cos, sin):
    xs = x.reshape(x.shape[:-1] + (2, 2, HEAD_DIM // 4))
    rot = jnp.concatenate([-xs[..., 1:, :], xs[..., :1, :]], axis=-2).reshape(x.shape)
    return (x * cos + rot * sin).astype(x.dtype)


def softmax_with_sink(logits, sink):
    sink_col = jnp.broadcast_to(sink.astype(jnp.float32).reshape(1, N_KV_HEADS, GROUP, 1, 1), logits.shape[:-1] + (1,))
    return jax.nn.softmax(jnp.concatenate([logits, sink_col], axis=-1), axis=-1)[..., :-1]


def context_attention(q, k, v, sink):
    B, T = q.shape[:2]
    nb = T // BLOCK
    qb = q.reshape(B, nb, BLOCK, N_KV_HEADS, GROUP, HEAD_DIM).swapaxes(0, 1)

    def one_block(qblk):
        s = jnp.einsum('bqkgd,bskd->bkgqs', qblk, k).astype(jnp.float32) * ATTN_SCALE
        pr = softmax_with_sink(s, sink).astype(v.dtype)
        return jnp.einsum('bkgqs,bskd->bqkgd', pr, v)

    o = lax.map(one_block, qb)
    return o.swapaxes(0, 1).reshape(B, T, ATTN_WIDTH)


def latent_window_attention(q, k, v, ck, cv, sink):
    B, L = q.shape[:2]
    nb = L // BLOCK
    kp = jnp.pad(k, ((0, 0), (BLOCK, BLOCK), (0, 0), (0, 0)))
    vp = jnp.pad(v, ((0, 0), (BLOCK, BLOCK), (0, 0), (0, 0)))
    qb = q.reshape(B, nb, BLOCK, N_KV_HEADS, GROUP, HEAD_DIM).swapaxes(0, 1)
    offs_q = jnp.arange(BLOCK)
    offs_k = jnp.arange(3 * BLOCK)
    n_lat = 3 * BLOCK

    def one_block(args):
        blk, qblk = args
        kw = lax.dynamic_slice_in_dim(kp, blk * BLOCK, n_lat, axis=1)
        vw = lax.dynamic_slice_in_dim(vp, blk * BLOCK, n_lat, axis=1)
        qpos = blk * BLOCK + offs_q
        kpos = (blk - 1) * BLOCK + offs_k
        valid = (jnp.abs(qpos[:, None] - kpos[None, :]) <= WINDOW) & (kpos[None, :] >= 0) & (kpos[None, :] < L)
        s_lat = jnp.einsum('bqkgd,bskd->bkgqs', qblk, kw).astype(jnp.float32) * ATTN_SCALE
        s_lat = jnp.where(valid, s_lat, NEG_BIG)
        s_ctx = jnp.einsum('bqkgd,bskd->bkgqs', qblk, ck).astype(jnp.float32) * ATTN_SCALE
        pr = softmax_with_sink(jnp.concatenate([s_lat, s_ctx], axis=-1), sink).astype(v.dtype)
        return (jnp.einsum('bkgqs,bskd->bqkgd', pr[..., :n_lat], vw)
                + jnp.einsum('bkgqs,bskd->bqkgd', pr[..., n_lat:], cv))

    o = lax.map(one_block, (jnp.arange(nb), qb))
    return o.swapaxes(0, 1).reshape(B, L, ATTN_WIDTH)


def gla_chunk_scan(q, k, v, logf, s0):
    B, L, H, _ = q.shape
    nc = L // HG_CHUNK

    def chunks(a):
        return a.reshape(B, nc, HG_CHUNK, H, a.shape[-1]).transpose(1, 0, 3, 2, 4)

    causal = jnp.tril(jnp.ones((HG_CHUNK, HG_CHUNK), dtype=bool))[:, :, None]

    def step(S, xs):
        qc, kc, vc, gc = xs
        b = jnp.cumsum(gc, axis=2)
        o_inter = jnp.einsum('bhtd,bhde->bhte', qc * jnp.exp(b), S)
        diff = b[:, :, :, None, :] - b[:, :, None, :, :]
        rel = jnp.where(causal, jnp.exp(jnp.where(causal, diff, 0.0)), 0.0)
        att = jnp.einsum('bhtd,bhsd,bhtsd->bhts', qc, kc, rel)
        o = o_inter + jnp.einsum('bhts,bhse->bhte', att, vc)
        b_end = b[:, :, -1:, :]
        S = jnp.exp(b_end[:, :, 0, :])[..., None] * S + jnp.einsum('bhsd,bhse->bhde', kc * jnp.exp(b_end - b), vc)
        return S, o

    s_fin, o = lax.scan(step, s0, (chunks(q), chunks(k), chunks(v), chunks(logf)))
    return o.transpose(1, 0, 3, 2, 4).reshape(B, L, H, v.shape[-1]), s_fin


def hgrn2_mix(q, f_fwd, f_bwd, inp, gate, lb, norm_w, s0):
    B, T, _ = q.shape

    def heads(a):
        return a.astype(jnp.float32).reshape(B, T, HG_HEADS, -1)

    qh = heads(jax.nn.silu(q)) * (HG_DK ** -0.5)
    ih = heads(inp)
    outs = []
    finals = []
    for d, f_pre in enumerate((f_fwd, f_bwd)):
        lbd = lb[d]
        log_f = jnp.logaddexp(jnp.log(jnp.maximum(lbd, LB_FLOOR)), jnp.log1p(-lbd) + jax.nn.log_sigmoid(f_pre.astype(jnp.float32)))
        fh = heads(log_f)
        kh = heads(1.0 - jnp.exp(log_f))
        args = (qh, kh, ih, fh)
        if d == 1:
            args = tuple(jnp.flip(a, axis=1) for a in args)
        o, s_fin = gla_chunk_scan(args[0], args[1], args[2], args[3], s0[:, d].astype(jnp.float32))
        if d == 1:
            o = jnp.flip(o, axis=1)
        outs.append(o)
        finals.append(s_fin)
    o = outs[0] + outs[1]
    o = o * lax.rsqrt(jnp.mean(jnp.square(o), axis=-1, keepdims=True) + EPS) * norm_w.astype(jnp.float32) * jax.nn.silu(heads(gate))
    return o.reshape(B, T, HG_WIDTH).astype(q.dtype), jnp.stack(finals, axis=1)


def swiglu(h, w1, w3, w2):
    return (jax.nn.silu(h @ w1) * (h @ w3)) @ w2


def moe_swiglu(h, router_w, w1, w3, w2):
    shp = h.shape
    hf = h.reshape(-1, shp[-1])
    logits = (hf @ router_w).astype(jnp.float32)
    top_v, top_i = lax.top_k(logits, TOP_K)
    top_w = jax.nn.softmax(top_v, axis=-1)
    combine = jnp.sum(jax.nn.one_hot(top_i, N_EXPERTS, dtype=jnp.float32) * top_w[..., None], axis=1)
    y = jnp.zeros_like(hf)
    for e in range(N_EXPERTS):
        y = y + combine[:, e:e + 1].astype(hf.dtype) * swiglu(hf, w1[e], w3[e], w2[e])
    return y.reshape(shp)


def trunk_layer(x, cond, l, p, ctx):
    B, T, _ = x.shape
    mod = jax.nn.silu(cond) @ p['ada_w'][l] + p['ada_b'][l]
    sh1, sc1, g1, sh2, sc2, g2 = jnp.split(mod[:, None, :], 6, axis=-1)
    h = rmsnorm(x, p['norm1_w'][l]) * (1 + sc1) + sh1
    cuts = np.cumsum(IN_SPLITS)[:-1].tolist()
    hy, aq, ak, av, hq, hf_f, hf_b, hi, hg, ma, mb, mc = jnp.split(h @ p['w_in'][l], cuts, axis=-1)

    hy = short_conv3(hy, p['hy_conv_w'][l], p['hy_conv_b'][l])
    x0, x1, hv = jnp.split(hy, 3, axis=-1)
    filt = hyena_filter(T, p['hy_w1'][l], p['hy_b1'][l], p['hy_freq'][l], p['hy_w2'][l], p['hy_b2'][l], p['hy_w3'][l], p['hy_decay'][l])
    u = (hv * x1).astype(jnp.float32)
    ya = ((fft_long_conv(u, filt) + u * p['hy_bias'][l].astype(jnp.float32)) * x0.astype(jnp.float32)).astype(x.dtype)

    q = aq.reshape(B, T, N_HEADS, HEAD_DIM)
    k = ak.reshape(B, T, N_KV_HEADS, HEAD_DIM)
    v = av.reshape(B, T, N_KV_HEADS, HEAD_DIM)
    if ctx is None:
        yb = context_attention(q, k, v, p['attn_sink'][l])
        s0 = jnp.zeros((B, 2, HG_HEADS, HG_DK, HG_DV), jnp.float32)
    else:
        ck, cv, s0 = ctx
        cos, sin = axial_rope_tables(T)
        q = apply_axial_rope(q, cos, sin)
        k = apply_axial_rope(k, cos, sin)
        yb = latent_window_attention(q, k, v, ck, cv, p['attn_sink'][l])

    yc, s_fin = hgrn2_mix(hq, hf_f, hf_b, hi, hg, p['hg_lb_all'][l], p['hg_norm_w'][l], s0)

    mixed = (jax.nn.sigmoid(ma) * (ya @ p['w_branch_a'][l])
             + jax.nn.sigmoid(mb) * (yb @ p['w_branch_b'][l])
             + jax.nn.sigmoid(mc) * (yc @ p['w_branch_c'][l]))
    x = x + g1 * (mixed @ p['w_out'][l])

    h2 = rmsnorm(x, p['norm2_w'][l]) * (1 + sc2) + sh2
    j = l // 2
    if l % 2 == 0:
        f = swiglu(h2, p['ffn_w1'][j], p['ffn_w3'][j], p['ffn_w2'][j])
    else:
        f = moe_swiglu(h2, p['router_w'][j], p['moe_w1'][j], p['moe_w3'][j], p['moe_w2'][j])
    x = x + g2 * f
    return x, k, v, s_fin


def setup_inputs(seed: int = 0) -> dict:
    key = jax.random.key(seed)
    ks = iter(jax.random.split(key, 48))
    D = D_MODEL

    def nrm(shape, scale=1.0):
        return jax.random.normal(next(ks), shape, jnp.float32) * scale

    def gain(shape):
        return 1.0 + nrm(shape, 0.02)

    inputs = {}
    inputs['x_prompt'] = nrm((BATCH, SEQ, D))
    inputs['x_sample'] = nrm((DEC_BATCH, DEC_SEQ, D))
    inputs['c'] = nrm((DEC_BATCH, D))
    inputs['cache_k'] = nrm((DEC_BATCH, DEPTH, PAST_LEN, N_KV_HEADS, HEAD_DIM))
    inputs['cache_v'] = nrm((DEC_BATCH, DEPTH, PAST_LEN, N_KV_HEADS, HEAD_DIM))
    inputs['state_hgrn'] = nrm((DEC_BATCH, DEPTH, 2, HG_HEADS, HG_DK, HG_DV), 0.5)
    inputs['c_ctx'] = nrm((D,))
    inputs['ada_w'] = nrm((DEPTH, D, 6 * D), D ** -0.5)
    inputs['ada_b'] = nrm((DEPTH, 6 * D), 0.02)
    inputs['norm1_w'] = gain((DEPTH, D))
    inputs['norm2_w'] = gain((DEPTH, D))
    inputs['w_in'] = nrm((DEPTH, D, IN_WIDTH), D ** -0.5)
    inputs['hy_conv_w'] = nrm((DEPTH, 3, 3 * HY_WIDTH), 3 ** -0.5)
    inputs['hy_conv_b'] = nrm((DEPTH, 3 * HY_WIDTH), 0.02)
    inputs['hy_w1'] = nrm((DEPTH, HY_EMB_DIM, HY_HIDDEN), HY_EMB_DIM ** -0.5)
    inputs['hy_b1'] = nrm((DEPTH, HY_HIDDEN), 0.1)
    inputs['hy_freq'] = gain((DEPTH, HY_HIDDEN))
    inputs['hy_w2'] = nrm((DEPTH, HY_HIDDEN, HY_HIDDEN), HY_HIDDEN ** -0.5)
    inputs['hy_b2'] = nrm((DEPTH, HY_HIDDEN), 0.1)
    inputs['hy_w3'] = nrm((DEPTH, HY_HIDDEN, 2 * HY_WIDTH), 0.05 * HY_HIDDEN ** -0.5)
    inputs['hy_decay'] = (jnp.broadcast_to(jnp.linspace(HY_SLOW_DECAY, HY_FAST_DECAY, HY_WIDTH, dtype=jnp.float32), (DEPTH, 2, HY_WIDTH))
                          + nrm((DEPTH, 2, HY_WIDTH), 0.1))
    inputs['hy_bias'] = nrm((DEPTH, HY_WIDTH))
    inputs['attn_sink'] = nrm((DEPTH, N_HEADS), 0.5)
    inputs['hg_lb'] = nrm((DEPTH, 2, HG_WIDTH), 0.1)
    inputs['hg_norm_w'] = gain((DEPTH, HG_DV))
    inputs['w_branch_a'] = nrm((DEPTH, HY_WIDTH, D), HY_WIDTH ** -0.5)
    inputs['w_branch_b'] = nrm((DEPTH, ATTN_WIDTH, D), ATTN_WIDTH ** -0.5)
    inputs['w_branch_c'] = nrm((DEPTH, HG_WIDTH, D), HG_WIDTH ** -0.5)
    inputs['w_out'] = nrm((DEPTH, D, D), D ** -0.5)
    inputs['ffn_w1'] = nrm((N_DENSE, D, D_FF), D ** -0.5)
    inputs['ffn_w3'] = nrm((N_DENSE, D, D_FF), D ** -0.5)
    inputs['ffn_w2'] = nrm((N_DENSE, D_FF, D), D_FF ** -0.5)
    inputs['router_w'] = nrm((N_MOE, D, N_EXPERTS), D ** -0.5)
    inputs['moe_w1'] = nrm((N_MOE, N_EXPERTS, D, D_FF), D ** -0.5)
    inputs['moe_w3'] = nrm((N_MOE, N_EXPERTS, D, D_FF), D ** -0.5)
    inputs['moe_w2'] = nrm((N_MOE, N_EXPERTS, D_FF, D), D_FF ** -0.5)
    inputs['final_norm_w'] = gain((D,))
    return inputs


def reference(x_prompt, x_sample, c, cache_k, cache_v, state_hgrn, c_ctx, ada_w, ada_b, norm1_w, norm2_w, w_in,
              hy_conv_w, hy_conv_b, hy_w1, hy_b1, hy_freq, hy_w2, hy_b2, hy_w3, hy_decay, hy_bias, attn_sink,
              hg_lb, hg_norm_w, w_branch_a, w_branch_b, w_branch_c, w_out, ffn_w1, ffn_w3, ffn_w2, router_w,
              moe_w1, moe_w3, moe_w2, final_norm_w):
    lb_sm = jax.nn.softmax(hg_lb.astype(jnp.float32), axis=0)
    hg_lb_all = jnp.cumsum(lb_sm, axis=0) - lb_sm[0]
    p = dict(ada_w=ada_w, ada_b=ada_b, norm1_w=norm1_w, norm2_w=norm2_w, w_in=w_in,
             hy_conv_w=hy_conv_w, hy_conv_b=hy_conv_b, hy_w1=hy_w1, hy_b1=hy_b1, hy_freq=hy_freq,
             hy_w2=hy_w2, hy_b2=hy_b2, hy_w3=hy_w3, hy_decay=hy_decay, hy_bias=hy_bias,
             attn_sink=attn_sink, hg_lb_all=hg_lb_all, hg_norm_w=hg_norm_w,
             w_branch_a=w_branch_a, w_branch_b=w_branch_b, w_branch_c=w_branch_c, w_out=w_out,
             ffn_w1=ffn_w1, ffn_w3=ffn_w3, ffn_w2=ffn_w2, router_w=router_w,
             moe_w1=moe_w1, moe_w3=moe_w3, moe_w2=moe_w2)

    xp = x_prompt
    ks_out, vs_out, ss_out = [], [], []
    for l in range(DEPTH):
        xp, k_l, v_l, s_l = trunk_layer(xp, c_ctx[None, :], l, p, None)
        ks_out.append(k_l)
        vs_out.append(v_l)
        ss_out.append(s_l)
    y_prompt = rmsnorm(xp, final_norm_w)
    new_cache_k = jnp.stack(ks_out, axis=1)
    new_cache_v = jnp.stack(vs_out, axis=1)
    new_state_hgrn = jnp.stack(ss_out, axis=1).astype(x_prompt.dtype)

    xs = x_sample
    for l in range(DEPTH):
        xs, _, _, _ = trunk_layer(xs, c, l, p, (cache_k[:, l], cache_v[:, l], state_hgrn[:, l]))
    y_sample = rmsnorm(xs, final_norm_w)

    return (y_prompt, y_sample, new_cache_k, new_cache_v, new_state_hgrn)
```

```python
import functools
import math

import numpy as np
import jax
import jax.numpy as jnp
from jax import lax
from jax.experimental import pallas as pl
from jax.experimental.pallas import tpu as pltpu

F32 = jnp.float32
BF16 = jnp.bfloat16

VMEM_LIMIT_BYTES = 56 * 1024 * 1024
LANES = 128

EPS = 1e-6
NEG_BIG = -1e30
LB_FLOOR = 1e-30
GRID_W = 64
HY_WIDTH = 1024
HY_BANDS = 16
N_HEADS = 8
N_KV_HEADS = 2
GROUP = N_HEADS // N_KV_HEADS
HEAD_DIM = 128
ATTN_WIDTH = N_HEADS * HEAD_DIM
KV_WIDTH = N_KV_HEADS * HEAD_DIM
WINDOW = 128
ROPE_BASE = 10000.0
ATTN_SCALE = HEAD_DIM ** -0.5
HG_HEADS = 8
HG_DK = 128
HG_DV = 128
HG_WIDTH = HG_HEADS * HG_DK
HG_CHUNK = 128
N_EXPERTS = 8
TOP_K = 2

OFF_HY = 0
OFF_AQ = 3 * HY_WIDTH
OFF_AK = OFF_AQ + ATTN_WIDTH
OFF_AV = OFF_AK + KV_WIDTH
OFF_HQ = OFF_AV + KV_WIDTH
OFF_FF = OFF_HQ + HG_WIDTH
OFF_FB = OFF_FF + HG_WIDTH
OFF_HI = OFF_FB + HG_WIDTH
OFF_HG = OFF_HI + HG_WIDTH
OFF_MA = OFF_HG + HG_WIDTH


def _cparams(*sem):
    return pltpu.CompilerParams(dimension_semantics=sem, vmem_limit_bytes=VMEM_LIMIT_BYTES)


def _split3(x):
    hi = x.astype(BF16)
    r1 = x - hi.astype(F32)
    mid = r1.astype(BF16)
    lo = (r1 - mid.astype(F32)).astype(BF16)
    return hi, mid, lo


def _dot(a, b):
    return jnp.dot(a, b, preferred_element_type=F32)


def _dot_f32(a, b):
    a0, a1, a2 = _split3(a)
    b0, b1, b2 = _split3(b)
    return (_dot(a0, b0) + (_dot(a0, b1) + _dot(a1, b0))
            + (_dot(a0, b2) + _dot(a1, b1) + _dot(a2, b0)))


def _silu(x):
    return x * jax.nn.sigmoid(x)


def _ada_kernel(c_ref, w_ref, b_ref, o_ref):
    o_ref[...] = _dot_f32(_silu(c_ref[...]), w_ref[...]) + b_ref[...]


def ada_modulation(cond, ada_w, ada_b, tn=512):
    depth, d, n = ada_w.shape
    rows = cond.shape[0]
    return pl.pallas_call(
        _ada_kernel,
        grid=(depth, n // tn),
        in_specs=[pl.BlockSpec((rows, d), lambda l, j: (0, 0)),
                  pl.BlockSpec((None, d, tn), lambda l, j: (l, 0, j)),
                  pl.BlockSpec((None, 1, tn), lambda l, j: (l, 0, j))],
        out_specs=pl.BlockSpec((None, rows, tn), lambda l, j: (l, 0, j)),
        out_shape=jax.ShapeDtypeStruct((depth, rows, n), F32),
        compiler_params=_cparams("parallel", "parallel"),
        name="ada_modulation",
    )(cond, ada_w, ada_b.reshape(depth, 1, n))


def _norm_modulate(x, nw, sc, sh):
    ms = jnp.mean(x * x, axis=-1, keepdims=True)
    y = x * lax.rsqrt(ms + EPS) * nw
    return y * (1.0 + sc) + sh


def _mod_index(layer, j, tiles_per_row, row0):
    def index(i, *_):
        return (layer, row0 + i // tiles_per_row, j, 0, 0)
    return index


def _ln_mm_kernel(x_ref, nw_ref, sc_ref, sh_ref, w_ref, o_ref, h_ref):
    @pl.when(pl.program_id(1) == 0)
    def _():
        h_ref[...] = _norm_modulate(x_ref[...], nw_ref[...], sc_ref[...], sh_ref[...]).astype(BF16)

    o_ref[...] = _dot(h_ref[...], w_ref[...]).astype(o_ref.dtype)


def ln_mod_matmul(x, nw, mod, mod_idx, w, *, layer, tiles_per_row, row0, tm, tn, out_dtype):
    t, d = x.shape
    n = w.shape[1]
    sc_j, sh_j = mod_idx
    mspec = lambda j: pl.BlockSpec((None, None, None, 1, d), _mod_index(layer, j, tiles_per_row, row0))
    return pl.pallas_call(
        _ln_mm_kernel,
        grid=(t // tm, n // tn),
        in_specs=[pl.BlockSpec((tm, d), lambda i, j: (i, 0)),
                  pl.BlockSpec((1, d), lambda i, j: (0, 0)),
                  mspec(sc_j), mspec(sh_j),
                  pl.BlockSpec((d, tn), lambda i, j: (0, j))],
        out_specs=pl.BlockSpec((tm, tn), lambda i, j: (i, j)),
        out_shape=jax.ShapeDtypeStruct((t, n), out_dtype),
        scratch_shapes=[pltpu.VMEM((tm, d), BF16)],
        compiler_params=_cparams("parallel", "arbitrary"),
        name="ln_mod_matmul",
    )(x, nw.reshape(1, d), mod, mod, w)


def _ln_glu_kernel(x_ref, nw_ref, sc_ref, sh_ref, w1_ref, w3_ref, o_ref, h_ref):
    @pl.when(pl.program_id(1) == 0)
    def _():
        h_ref[...] = _norm_modulate(x_ref[...], nw_ref[...], sc_ref[...], sh_ref[...]).astype(BF16)

    h = h_ref[...]
    o_ref[...] = (_silu(_dot(h, w1_ref[...])) * _dot(h, w3_ref[...])).astype(o_ref.dtype)


def ln_mod_glu(x, nw, mod, mod_idx, w1, w3, *, layer, tiles_per_row, row0, tm, tn):
    t, d = x.shape
    n = w1.shape[1]
    sc_j, sh_j = mod_idx
    mspec = lambda j: pl.BlockSpec((None, None, None, 1, d), _mod_index(layer, j, tiles_per_row, row0))
    return pl.pallas_call(
        _ln_glu_kernel,
        grid=(t // tm, n // tn),
        in_specs=[pl.BlockSpec((tm, d), lambda i, j: (i, 0)),
                  pl.BlockSpec((1, d), lambda i, j: (0, 0)),
                  mspec(sc_j), mspec(sh_j),
                  pl.BlockSpec((d, tn), lambda i, j: (0, j)),
                  pl.BlockSpec((d, tn), lambda i, j: (0, j))],
        out_specs=pl.BlockSpec((tm, tn), lambda i, j: (i, j)),
        out_shape=jax.ShapeDtypeStruct((t, n), BF16),
        scratch_shapes=[pltpu.VMEM((tm, d), BF16)],
        compiler_params=_cparams("parallel", "arbitrary"),
        name="ln_mod_glu",
    )(x, nw.reshape(1, d), mod, mod, w1, w3)


def _merge_kernel(ya_ref, yb_ref, yc_ref, ma_ref, mb_ref, mc_ref, wa_ref, wb_ref, wc_ref, o_ref):
    acc = jax.nn.sigmoid(ma_ref[...]) * _dot(ya_ref[...], wa_ref[...])
    acc = acc + jax.nn.sigmoid(mb_ref[...]) * _dot(yb_ref[...], wb_ref[...])
    acc = acc + jax.nn.sigmoid(mc_ref[...]) * _dot(yc_ref[...], wc_ref[...])
    o_ref[...] = acc.astype(o_ref.dtype)


def branch_merge(ya, yb, yc, proj, wa, wb, wc, *, tm, tn):
    t, k = ya.shape
    n = wa.shape[1]
    gate = lambda off: pl.BlockSpec((tm, tn), lambda i, j: (i, off // tn + j))
    yspec = pl.BlockSpec((tm, k), lambda i, j: (i, 0))
    wspec = pl.BlockSpec((k, tn), lambda i, j: (0, j))
    return pl.pallas_call(
        _merge_kernel,
        grid=(t // tm, n // tn),
        in_specs=[yspec, yspec, yspec, gate(OFF_MA), gate(OFF_MA + n), gate(OFF_MA + 2 * n),
                  wspec, wspec, wspec],
        out_specs=pl.BlockSpec((tm, tn), lambda i, j: (i, j)),
        out_shape=jax.ShapeDtypeStruct((t, n), BF16),
        compiler_params=_cparams("parallel", "parallel"),
        name="branch_merge",
    )(ya, yb, yc, proj, proj, proj, wa, wb, wc)


def _mm_resid_kernel(a_ref, w_ref, x_ref, g_ref, o_ref):
    o_ref[...] = x_ref[...] + g_ref[...] * _dot(a_ref[...], w_ref[...])


def matmul_gated_residual(a, w, x, mod, g_j, *, layer, tiles_per_row, row0, tm, tn):
    t, k = a.shape
    n = w.shape[1]
    return pl.pallas_call(
        _mm_resid_kernel,
        grid=(t // tm, n // tn),
        in_specs=[pl.BlockSpec((tm, k), lambda i, j: (i, 0)),
                  pl.BlockSpec((k, tn), lambda i, j: (0, j)),
                  pl.BlockSpec((tm, tn), lambda i, j: (i, j)),
                  pl.BlockSpec((None, None, None, 1, tn),
                               lambda i, j: (layer, row0 + i // tiles_per_row, g_j, 0, j))],
        out_specs=pl.BlockSpec((tm, tn), lambda i, j: (i, j)),
        out_shape=jax.ShapeDtypeStruct((t, n), F32),
        compiler_params=_cparams("parallel", "parallel"),
        name="matmul_gated_residual",
    )(a, w, x, mod)


def _mm_rowscale_acc_kernel(expert, a_ref, w_ref, c_ref, y_ref, o_ref):
    c = c_ref[...][:, expert:expert + 1]
    o_ref[...] = y_ref[...] + c * _dot(a_ref[...], w_ref[...])


def matmul_rowscale_accumulate(a, w, comb, y, expert, *, tm, tn):
    t, k = a.shape
    n = w.shape[1]
    return pl.pallas_call(
        functools.partial(_mm_rowscale_acc_kernel, expert),
        grid=(t // tm, n // tn),
        in_specs=[pl.BlockSpec((tm, k), lambda i, j: (i, 0)),
                  pl.BlockSpec((k, tn), lambda i, j: (0, j)),
                  pl.BlockSpec((tm, LANES), lambda i, j: (i, 0)),
                  pl.BlockSpec((tm, tn), lambda i, j: (i, j))],
        out_specs=pl.BlockSpec((tm, tn), lambda i, j: (i, j)),
        out_shape=jax.ShapeDtypeStruct((t, n), F32),
        input_output_aliases={3: 0},
        compiler_params=_cparams("parallel", "parallel"),
        name="matmul_rowscale_accumulate",
    )(a, w, comb, y)


def _glu_kernel(h_ref, w1_ref, w3_ref, o_ref):
    h = h_ref[...]
    o_ref[...] = (_silu(_dot(h, w1_ref[...])) * _dot(h, w3_ref[...])).astype(o_ref.dtype)


def glu_up(h, w1, w3, *, tm, tn):
    t, d = h.shape
    n = w1.shape[1]
    return pl.pallas_call(
        _glu_kernel,
        grid=(t // tm, n // tn),
        in_specs=[pl.BlockSpec((tm, d), lambda i, j: (i, 0)),
                  pl.BlockSpec((d, tn), lambda i, j: (0, j)),
                  pl.BlockSpec((d, tn), lambda i, j: (0, j))],
        out_specs=pl.BlockSpec((tm, tn), lambda i, j: (i, j)),
        out_shape=jax.ShapeDtypeStruct((t, n), BF16),
        compiler_params=_cparams("parallel", "parallel"),
        name="glu_up",
    )(h, w1, w3)


def _resid_norm_kernel(final, x_ref, y_ref, g_ref, fw_ref, o_ref):
    x = x_ref[...] + g_ref[...] * y_ref[...]
    if final:
        x = x * lax.rsqrt(jnp.mean(x * x, axis=-1, keepdims=True) + EPS) * fw_ref[...]
    o_ref[...] = x


def gated_residual_norm(x, y, mod, g_j, fw, *, layer, tiles_per_row, row0, tm, final):
    t, d = x.shape
    return pl.pallas_call(
        functools.partial(_resid_norm_kernel, final),
        grid=(t // tm,),
        in_specs=[pl.BlockSpec((tm, d), lambda i: (i, 0)),
                  pl.BlockSpec((tm, d), lambda i: (i, 0)),
                  pl.BlockSpec((None, None, None, 1, d), _mod_index(layer, g_j, tiles_per_row, row0)),
                  pl.BlockSpec((1, d), lambda i: (0, 0))],
        out_specs=pl.BlockSpec((tm, d), lambda i: (i, 0)),
        out_shape=jax.ShapeDtypeStruct((t, d), F32),
        compiler_params=_cparams("parallel"),
        name="gated_residual_norm",
    )(x, y, mod, fw.reshape(1, d))


def _final_norm_kernel(x_ref, fw_ref, o_ref):
    x = x_ref[...]
    o_ref[...] = x * lax.rsqrt(jnp.mean(x * x, axis=-1, keepdims=True) + EPS) * fw_ref[...]


def final_norm(x, fw, *, tm):
    t, d = x.shape
    return pl.pallas_call(
        _final_norm_kernel,
        grid=(t // tm,),
        in_specs=[pl.BlockSpec((tm, d), lambda i: (i, 0)), pl.BlockSpec((1, d), lambda i: (0, 0))],
        out_specs=pl.BlockSpec((tm, d), lambda i: (i, 0)),
        out_shape=jax.ShapeDtypeStruct((t, d), F32),
        compiler_params=_cparams("parallel"),
        name="final_norm",
    )(x, fw.reshape(1, d))


def _router_kernel(x_ref, nw_ref, sc_ref, sh_ref, rw_ref, h_ref, c_ref):
    h = _norm_modulate(x_ref[...], nw_ref[...], sc_ref[...], sh_ref[...])
    h_ref[...] = h.astype(BF16)
    lane = lax.broadcasted_iota(jnp.int32, (h.shape[0], LANES), 1)
    logits = jnp.where(lane < N_EXPERTS, _dot_f32(h, rw_ref[...]), -jnp.inf)
    m1 = jnp.max(logits, axis=-1, keepdims=True)
    i1 = jnp.min(jnp.where(logits == m1, lane, LANES), axis=-1, keepdims=True)
    rest = jnp.where(lane == i1, -jnp.inf, logits)
    m2 = jnp.max(rest, axis=-1, keepdims=True)
    i2 = jnp.min(jnp.where(rest == m2, lane, LANES), axis=-1, keepdims=True)
    e2 = jnp.exp(m2 - m1)
    inv = 1.0 / (1.0 + e2)
    c_ref[...] = jnp.where(lane == i1, inv, 0.0) + jnp.where(lane == i2, e2 * inv, 0.0)


def router(x, nw, mod, mod_idx, rw, *, layer, tiles_per_row, row0, tm):
    t, d = x.shape
    sc_j, sh_j = mod_idx
    rw_pad = jnp.zeros((d, LANES), F32).at[:, :N_EXPERTS].set(rw)
    mspec = lambda j: pl.BlockSpec((None, None, None, 1, d), _mod_index(layer, j, tiles_per_row, row0))
    return pl.pallas_call(
        _router_kernel,
        grid=(t // tm,),
        in_specs=[pl.BlockSpec((tm, d), lambda i: (i, 0)),
                  pl.BlockSpec((1, d), lambda i: (0, 0)),
                  mspec(sc_j), mspec(sh_j),
                  pl.BlockSpec((d, LANES), lambda i: (0, 0))],
        out_specs=[pl.BlockSpec((tm, d), lambda i: (i, 0)),
                   pl.BlockSpec((tm, LANES), lambda i: (i, 0))],
        out_shape=[jax.ShapeDtypeStruct((t, d), BF16), jax.ShapeDtypeStruct((t, LANES), F32)],
        compiler_params=_cparams("parallel"),
        name="router",
    )(x, nw.reshape(1, d), mod, mod, rw_pad)


def _hyena_filter_kernel(z_ref, w1_ref, b1_ref, fr_ref, w2_ref, b2_ref, w3_ref, t_ref, dec_ref, o_ref):
    fr = fr_ref[...]
    h = jnp.sin(fr * (_dot_f32(z_ref[...], w1_ref[...]) + b1_ref[...]))
    h = jnp.sin(fr * (_dot_f32(h, w2_ref[...]) + b2_ref[...]))
    o_ref[...] = _dot_f32(h, w3_ref[...]) * jnp.exp(-t_ref[...] * jnp.abs(dec_ref[...]))


def hyena_filter(seq, w1, b1, freq, w2, b2, w3, decay, tl=256):
    t = jnp.linspace(0.0, 1.0, seq, dtype=F32)[:, None]
    pos = jnp.arange(seq, dtype=F32)[:, None]
    bands = jnp.linspace(1e-4, HY_BANDS - 1.0, HY_BANDS, dtype=F32)[None, :]
    ang = (2.0 * math.pi / seq) * pos * bands
    z = jnp.concatenate([t, jnp.cos(ang), -jnp.sin(ang)], axis=-1)
    emb = z.shape[1]
    emb_pad = LANES
    z = jnp.pad(z, ((0, 0), (0, emb_pad - emb)))
    w1p = jnp.pad(w1, ((0, emb_pad - emb), (0, 0)))
    hid = w1.shape[1]
    n = w3.shape[1]
    tl = min(tl, seq)
    full = lambda shape: pl.BlockSpec(shape, lambda i: (0, 0))
    return pl.pallas_call(
        _hyena_filter_kernel,
        grid=(seq // tl,),
        in_specs=[pl.BlockSpec((tl, emb_pad), lambda i: (i, 0)),
                  full((emb_pad, hid)), full((1, hid)), full((1, hid)),
                  full((hid, hid)), full((1, hid)), full((hid, n)),
                  pl.BlockSpec((tl, 1), lambda i: (i, 0)), full((1, n))],
        out_specs=pl.BlockSpec((tl, n), lambda i: (i, 0)),
        out_shape=jax.ShapeDtypeStruct((seq, n), F32),
        compiler_params=_cparams("parallel"),
        name="hyena_filter",
    )(z, w1p, b1.reshape(1, hid), freq.reshape(1, hid), w2, b2.reshape(1, hid), w3, t,
      decay.reshape(1, n))


def _dft_tables(seq, kb):
    n = 2 * seq
    k = jnp.arange(seq, dtype=jnp.int32)[:, None]
    s = jnp.arange(seq, dtype=jnp.int32)[None, :]
    ang = ((k * s) % n).astype(F32) * (2.0 * math.pi / n)
    cos = jnp.cos(ang)
    sin = jnp.sin(ang)
    nyq = jnp.where(s % 2 == 0, 1.0, -1.0).astype(F32)
    is0 = k == 0
    f_re = cos
    f_im = jnp.where(is0, nyq, -sin)
    i_re = jnp.where(is0, 1.0 / n, (2.0 / n) * cos)
    i_im = jnp.where(is0, nyq / n, -(2.0 / n) * sin)
    nkb = seq // kb
    fwd = jnp.concatenate([f_re.reshape(nkb, kb, seq), f_im.reshape(nkb, kb, seq)], axis=1)
    inv = jnp.concatenate([i_re.reshape(nkb, kb, seq), i_im.reshape(nkb, kb, seq)], axis=1)
    return fwd, jnp.swapaxes(inv, 1, 2)


def _spectrum_kernel(kb, f_ref, hf_ref, hb_ref, o_ref):
    f = f_ref[...]
    row = lax.broadcasted_iota(jnp.int32, hb_ref.shape, 0)
    hb0 = jnp.where(row == 0, 0.0, hb_ref[...])
    a = _dot_f32(f, hf_ref[...])
    b = _dot_f32(f, hb0)
    orow = lax.broadcasted_iota(jnp.int32, a.shape, 0)
    nyq_slot = (orow == kb) & (pl.program_id(0) == 0)
    o_ref[...] = jnp.where((orow < kb) | nyq_slot, a + b, a - b)


def hyena_spectrum(fwd_f32, filt, kb, tc=256):
    nkb, kb2, seq = fwd_f32.shape
    c = filt.shape[1] // 2
    return pl.pallas_call(
        functools.partial(_spectrum_kernel, kb),
        grid=(nkb, c // tc),
        in_specs=[pl.BlockSpec((None, kb2, seq), lambda j, i: (j, 0, 0)),
                  pl.BlockSpec((seq, tc), lambda j, i: (0, i)),
                  pl.BlockSpec((seq, tc), lambda j, i: (0, c // tc + i))],
        out_specs=pl.BlockSpec((None, kb2, tc), lambda j, i: (j, 0, i)),
        out_shape=jax.ShapeDtypeStruct((nkb, kb2, c), F32),
        compiler_params=_cparams("parallel", "parallel"),
        name="hyena_spectrum",
    )(fwd_f32, filt, filt)


def _hyena_conv_kernel(kb, x0_ref, x1_ref, v_ref, cw0_ref, cw1_ref, cwv_ref, cb0_ref, cb1_ref, cbv_ref,
                       bias_ref, f_ref, i_ref, kf_ref, o_ref, u16_ref, u32_ref, x0c_ref, acc_ref):
    j = pl.program_id(2)
    seq = x0_ref.shape[0]

    def conv3(x_ref, w_ref, b_ref):
        x = x_ref[...]
        row = lax.broadcasted_iota(jnp.int32, x.shape, 0)
        prev = jnp.where(row == 0, 0.0, pltpu.roll(x, 1, 0))
        nxt = jnp.where(row == seq - 1, 0.0, pltpu.roll(x, seq - 1, 0))
        w = w_ref[...]
        return prev * w[0:1, :] + x * w[1:2, :] + nxt * w[2:3, :] + b_ref[...]

    @pl.when(j == 0)
    def _():
        u = conv3(v_ref, cwv_ref, cbv_ref) * conv3(x1_ref, cw1_ref, cb1_ref)
        u32_ref[...] = u
        u16_ref[...] = u.astype(BF16)
        x0c_ref[...] = conv3(x0_ref, cw0_ref, cb0_ref)
        acc_ref[...] = jnp.zeros_like(acc_ref)

    spec = _dot(f_ref[...], u16_ref[...])
    xr, xi = spec[:kb], spec[kb:]
    kf = kf_ref[...]
    kr, ki = kf[:kb], kf[kb:]
    packed = (lax.broadcasted_iota(jnp.int32, xr.shape, 0) == 0) & (j == 0)
    yr = xr * kr - jnp.where(packed, 0.0, xi * ki)
    yi = jnp.where(packed, xi * ki, xr * ki + xi * kr)
    y = jnp.concatenate([yr, yi], axis=0).astype(BF16)
    acc_ref[...] += _dot(i_ref[...], y)

    @pl.when(j == pl.num_programs(2) - 1)
    def _():
        o_ref[...] = ((acc_ref[...] + u32_ref[...] * bias_ref[...]) * x0c_ref[...]).astype(o_ref.dtype)


def hyena_conv(proj, conv_w, conv_b, bias, fwd, inv, spectrum, *, batch, seq, ct, kb):
    c = HY_WIDTH
    nkb = seq // kb
    ncb = c // ct
    col = lambda part: pl.BlockSpec((seq, ct), lambda b, i, j: (b, part * ncb + i))
    cw = lambda part: pl.BlockSpec((3, ct), lambda b, i, j: (0, part * ncb + i))
    cb = lambda part: pl.BlockSpec((1, ct), lambda b, i, j: (0, part * ncb + i))
    conv_b = conv_b.reshape(1, 3 * c)
    return pl.pallas_call(
        functools.partial(_hyena_conv_kernel, kb),
        grid=(batch, ncb, nkb),
        in_specs=[col(0), col(1), col(2), cw(0), cw(1), cw(2), cb(0), cb(1), cb(2),
                  pl.BlockSpec((1, ct), lambda b, i, j: (0, i)),
                  pl.BlockSpec((None, 2 * kb, seq), lambda b, i, j: (j, 0, 0)),
                  pl.BlockSpec((None, seq, 2 * kb), lambda b, i, j: (j, 0, 0)),
                  pl.BlockSpec((None, 2 * kb, ct), lambda b, i, j: (j, 0, i))],
        out_specs=pl.BlockSpec((seq, ct), lambda b, i, j: (b, i)),
        out_shape=jax.ShapeDtypeStruct((batch * seq, c), BF16),
        scratch_shapes=[pltpu.VMEM((seq, ct), BF16), pltpu.VMEM((seq, ct), F32),
                        pltpu.VMEM((seq, ct), F32), pltpu.VMEM((seq, ct), F32)],
        compiler_params=_cparams("parallel", "parallel", "arbitrary"),
        name="hyena_conv",
    )(proj, proj, proj, conv_w, conv_w, conv_w, conv_b, conv_b, conv_b, bias.reshape(1, c),
      fwd, inv, spectrum)


def _rope(x, cos, sin_a, sin_b):
    return x * cos + pltpu.roll(x, HEAD_DIM - HEAD_DIM // 4, 1) * sin_a + pltpu.roll(x, HEAD_DIM // 4, 1) * sin_b


def _softmax_pv(scores, values, sink):
    m = sink
    for s in scores:
        m = jnp.maximum(m, jnp.max(s, axis=-1, keepdims=True))
    den = jnp.exp(sink - m)
    out = None
    for s, v in zip(scores, values):
        p = jnp.exp(s - m)
        den = den + jnp.sum(p, axis=-1, keepdims=True)
        pv = _dot(p.astype(BF16), v)
        out = pv if out is None else out + pv
    return out / den


def _dot_nt(a, b):
    return lax.dot_general(a, b, (((1,), (1,)), ((), ())), preferred_element_type=F32)


def _ctx_attn_kernel(sink_ref, q_ref, k_ref, v_ref, o_ref):
    kvh = pl.program_id(1)
    k = k_ref[...].astype(BF16)
    v = v_ref[...].astype(BF16)
    for g in range(GROUP):
        q = q_ref[:, g * HEAD_DIM:(g + 1) * HEAD_DIM].astype(BF16)
        s = _dot_nt(q, k) * ATTN_SCALE
        o = _softmax_pv([s], [v], sink_ref[kvh * GROUP + g])
        o_ref[:, g * HEAD_DIM:(g + 1) * HEAD_DIM] = o.astype(o_ref.dtype)


def context_attention(proj, sink, *, batch, seq):
    qw = GROUP * HEAD_DIM
    return pl.pallas_call(
        _ctx_attn_kernel,
        grid=(batch, N_KV_HEADS),
        in_specs=[pl.BlockSpec(memory_space=pltpu.SMEM),
                  pl.BlockSpec((seq, qw), lambda b, h: (b, OFF_AQ // qw + h)),
                  pl.BlockSpec((seq, HEAD_DIM), lambda b, h: (b, OFF_AK // HEAD_DIM + h)),
                  pl.BlockSpec((seq, HEAD_DIM), lambda b, h: (b, OFF_AV // HEAD_DIM + h))],
        out_specs=pl.BlockSpec((seq, qw), lambda b, h: (b, h)),
        out_shape=jax.ShapeDtypeStruct((batch * seq, ATTN_WIDTH), BF16),
        compiler_params=_cparams("parallel", "parallel"),
        name="context_attention",
    )(sink, proj, proj, proj)


def _lat_attn_kernel(seq, sink_ref, q_ref, k_ref, v_ref, ck_ref, cv_ref, cos_ref, sa_ref, sb_ref,
                     o_ref, kr_ref, vb_ref):
    kvh = pl.program_id(1)
    qb = pl.program_id(2)
    blk = q_ref.shape[0]
    nwin = 3 * blk

    @pl.when(qb == 0)
    def _():
        kr_ref[...] = _rope(k_ref[...], cos_ref[...], sa_ref[...], sb_ref[...]).astype(BF16)
        vb_ref[...] = v_ref[...].astype(BF16)

    start = pl.multiple_of(jnp.clip((qb - 1) * blk, 0, seq - nwin), blk)
    kw = kr_ref[pl.ds(start, nwin), :]
    vw = vb_ref[pl.ds(start, nwin), :]
    ck = ck_ref[...].astype(BF16)
    cv = cv_ref[...].astype(BF16)
    rows = pl.ds(pl.multiple_of(qb * blk, blk), blk)
    cos, sa, sb = cos_ref[rows, :], sa_ref[rows, :], sb_ref[rows, :]
    qpos = qb * blk + lax.broadcasted_iota(jnp.int32, (blk, nwin), 0)
    kpos = start + lax.broadcasted_iota(jnp.int32, (blk, nwin), 1)
    valid = jnp.abs(qpos - kpos) <= WINDOW
    for g in range(GROUP):
        q = _rope(q_ref[:, g * HEAD_DIM:(g + 1) * HEAD_DIM], cos, sa, sb).astype(BF16)
        s_lat = jnp.where(valid, _dot_nt(q, kw) * ATTN_SCALE, NEG_BIG)
        s_ctx = _dot_nt(q, ck) * ATTN_SCALE
        o = _softmax_pv([s_lat, s_ctx], [vw, cv], sink_ref[kvh * GROUP + g])
        o_ref[:, g * HEAD_DIM:(g + 1) * HEAD_DIM] = o.astype(o_ref.dtype)


def _rope_tables(seq):
    rows = seq // GRID_W
    row = jnp.repeat(jnp.arange(rows, dtype=F32), GRID_W)
    col = jnp.tile(jnp.arange(GRID_W, dtype=F32), rows)
    quarter = HEAD_DIM // 4
    inv = ROPE_BASE ** (-jnp.arange(quarter, dtype=F32) / quarter)
    ar = row[:, None] * inv
    ac = col[:, None] * inv
    ang = jnp.concatenate([ar, ar, ac, ac], axis=-1)
    cos, sin = jnp.cos(ang), jnp.sin(ang)
    first = (jnp.arange(HEAD_DIM) % (2 * quarter)) < quarter
    return cos, jnp.where(first, -sin, 0.0), jnp.where(first, 0.0, sin)


def latent_attention(proj, cache_k, cache_v, sink, *, layer, batch, seq, blk=128):
    qw = GROUP * HEAD_DIM
    past = cache_k.shape[2]
    nqb = seq // blk
    cos, sa, sb = _rope_tables(seq)
    table = pl.BlockSpec((seq, HEAD_DIM), lambda b, h, i: (0, 0))
    cache = pl.BlockSpec((None, None, past, HEAD_DIM), lambda b, h, i: (b, layer, 0, h))
    return pl.pallas_call(
        functools.partial(_lat_attn_kernel, seq),
        grid=(batch, N_KV_HEADS, nqb),
        in_specs=[pl.BlockSpec(memory_space=pltpu.SMEM),
                  pl.BlockSpec((blk, qw), lambda b, h, i: (b * nqb + i, OFF_AQ // qw + h)),
                  pl.BlockSpec((seq, HEAD_DIM), lambda b, h, i: (b, OFF_AK // HEAD_DIM + h)),
                  pl.BlockSpec((seq, HEAD_DIM), lambda b, h, i: (b, OFF_AV // HEAD_DIM + h)),
                  cache, cache, table, table, table],
        out_specs=pl.BlockSpec((blk, qw), lambda b, h, i: (b * nqb + i, h)),
        out_shape=jax.ShapeDtypeStruct((batch * seq, ATTN_WIDTH), BF16),
        scratch_shapes=[pltpu.VMEM((seq, HEAD_DIM), BF16), pltpu.VMEM((seq, HEAD_DIM), BF16)],
        compiler_params=_cparams("parallel", "parallel", "arbitrary"),
        name="latent_attention",
    )(sink, proj, proj, proj, cache_k, cache_v, cos, sa, sb)


_HG_LEVELS = tuple(HG_CHUNK >> (i + 1) for i in range(int(math.log2(HG_CHUNK))))


def _hgrn_tables():
    c = HG_CHUNK
    t = np.arange(c)[:, None]
    u = np.arange(c)[None, :]
    blocks = [(u <= t), (u > t)]
    for m in _HG_LEVELS:
        ref = (t // (2 * m)) * (2 * m) + m - 1
        second = (t % (2 * m)) >= m
        blocks.append(np.where(second, (u > ref) & (u <= t), (u > t) & (u <= ref)))
    fwd = np.concatenate(blocks, axis=0).astype(np.float32)
    bwd = np.concatenate([b[::-1, ::-1] for b in blocks], axis=0).astype(np.float32)
    s = u
    level = np.full((c, c), len(_HG_LEVELS) + 1, np.int32)
    level[t == s] = len(_HG_LEVELS)
    for i, m in enumerate(_HG_LEVELS):
        hit = (t // (2 * m) == s // (2 * m)) & ((t % (2 * m)) >= m) & ((s % (2 * m)) < m)
        level[hit] = i
    a = np.stack([fwd, bwd])
    a = np.concatenate([a, a, a], axis=2)
    lv = np.stack([level, level.T])
    return jnp.asarray(a, BF16), jnp.asarray(lv, jnp.int32)


def _hgrn_kernel(layer, has_s0, seq, *refs):
    if has_s0:
        (q_ref, ff_ref, fb_ref, i_ref, g_ref, lb_ref, nw_ref, a_ref, lv_ref, s0_ref,
         y_ref, sfin_ref, of_ref, ob_ref, st_ref) = refs
    else:
        (q_ref, ff_ref, fb_ref, i_ref, g_ref, lb_ref, nw_ref, a_ref, lv_ref,
         y_ref, sfin_ref, of_ref, ob_ref, st_ref) = refs
    c = HG_CHUNK
    nlev = len(_HG_LEVELS)
    nc = seq // c

    lbs = lb_ref[...]
    mx = jnp.max(lbs, axis=0, keepdims=True)
    ex = jnp.exp(lbs - mx)
    sm = ex / jnp.sum(ex, axis=0, keepdims=True)
    lb = jnp.zeros(sm.shape[1:], F32)
    for j in range(1, layer + 1):
        lb = lb + sm[j]

    for d in range(2):
        if has_s0:
            st_ref[d] = s0_ref[d].T
        else:
            st_ref[d] = jnp.zeros((HG_DV, HG_DK), F32)

    def chunk(d, ci, o_ref):
        rows = pl.ds(pl.multiple_of(ci * c, c), c)
        q = _silu(q_ref[rows, :]) * (HG_DK ** -0.5)
        fpre = (ff_ref if d == 0 else fb_ref)[rows, :]
        lbd = lb[d:d + 1, :]
        f = jnp.maximum(lbd, LB_FLOOR) + (1.0 - lbd) * jax.nn.sigmoid(fpre)
        lf = jnp.log(f)
        k = 1.0 - f
        v = i_ref[rows, :].astype(BF16)
        e = jnp.exp(_dot(a_ref[d], jnp.concatenate(_split3(lf), axis=0)))
        q_in = (q * e[0:c]).astype(BF16)
        k_out = (k * e[c:2 * c]).astype(BF16)
        lv = lv_ref[d]
        att = jnp.where(lv == nlev, _dot_nt(q.astype(BF16), k.astype(BF16)), 0.0)
        for i in range(nlev):
            em = e[(2 + i) * c:(3 + i) * c]
            att = att + jnp.where(lv == i, _dot_nt((q * em).astype(BF16), (k * em).astype(BF16)), 0.0)
        st = st_ref[d]
        o_ref[rows, :] = _dot_nt(q_in, st.astype(BF16)) + _dot(att.astype(BF16), v)
        last = (c - 1) if d == 0 else 0
        upd = lax.dot_general(v, k_out, (((0,), (0,)), ((), ())), preferred_element_type=F32)
        st_ref[d] = st * e[last:last + 1] + upd

    def body(ci, carry):
        chunk(0, ci, of_ref)
        chunk(1, nc - 1 - ci, ob_ref)
        return carry

    lax.fori_loop(0, nc, body, 0)

    o = of_ref[...] + ob_ref[...]
    o = o * lax.rsqrt(jnp.mean(o * o, axis=-1, keepdims=True) + EPS) * nw_ref[...] * _silu(g_ref[...])
    y_ref[...] = o.astype(y_ref.dtype)
    for d in range(2):
        sfin_ref[d] = st_ref[d].T


def hgrn2_mix(proj, hg_lb, norm_w, s0, *, layer, batch, seq):
    a_tab, lv_tab = _hgrn_tables()
    depth = hg_lb.shape[0]
    col = lambda off: pl.BlockSpec((seq, HG_DK), lambda b, h: (b, off // HG_DK + h))
    state = pl.BlockSpec((None, 2, None, HG_DK, HG_DV), lambda b, h: (b, 0, h, 0, 0))
    in_specs = [col(OFF_HQ), col(OFF_FF), col(OFF_FB), col(OFF_HI), col(OFF_HG),
                pl.BlockSpec((depth, 2, HG_DK), lambda b, h: (0, 0, h)),
                pl.BlockSpec((1, HG_DV), lambda b, h: (0, 0)),
                pl.BlockSpec(a_tab.shape, lambda b, h: (0, 0, 0)),
                pl.BlockSpec(lv_tab.shape, lambda b, h: (0, 0, 0))]
    args = [proj, proj, proj, proj, proj, hg_lb, norm_w.reshape(1, HG_DV), a_tab, lv_tab]
    if s0 is not None:
        in_specs.append(state)
        args.append(s0)
    return pl.pallas_call(
        functools.partial(_hgrn_kernel, layer, s0 is not None, seq),
        grid=(batch, HG_HEADS),
        in_specs=in_specs,
        out_specs=[pl.BlockSpec((seq, HG_DV), lambda b, h: (b, h)), state],
        out_shape=[jax.ShapeDtypeStruct((batch * seq, HG_WIDTH), BF16),
                   jax.ShapeDtypeStruct((batch, 2, HG_HEADS, HG_DK, HG_DV), F32)],
        scratch_shapes=[pltpu.VMEM((seq, HG_DV), F32), pltpu.VMEM((seq, HG_DV), F32),
                        pltpu.VMEM((2, HG_DV, HG_DK), F32)],
        compiler_params=_cparams("parallel", "parallel"),
        name="hgrn2_mix",
    )(*args)


def _trunk_layer(x, l, p, *, batch, seq, mod, row0, ctx, final):
    t, d = x.shape
    tm = 1024
    per_row = (t if ctx is None else seq) // tm
    geo = dict(layer=l, tiles_per_row=per_row, row0=row0)

    proj = ln_mod_matmul(x, p['norm1_w'][l], mod, (1, 0), p['w_in'][l], tm=tm, tn=512, out_dtype=F32, **geo)

    filt = hyena_filter(seq, p['hy_w1'][l], p['hy_b1'][l], p['hy_freq'][l], p['hy_w2'][l], p['hy_b2'][l],
                        p['hy_w3'][l], p['hy_decay'][l])
    kb = min(seq, 512)
    fwd, inv = _dft_tables(seq, kb)
    spectrum = hyena_spectrum(fwd, filt, kb)
    ya = hyena_conv(proj, p['hy_conv_w'][l], p['hy_conv_b'][l], p['hy_bias'][l], fwd.astype(BF16),
                    inv.astype(BF16), spectrum, batch=batch, seq=seq, ct=256, kb=kb)

    if ctx is None:
        yb = context_attention(proj, p['attn_sink'][l], batch=batch, seq=seq)
        s0 = None
    else:
        cache_k, cache_v, s0 = ctx
        yb = latent_attention(proj, cache_k, cache_v, p['attn_sink'][l], layer=l, batch=batch, seq=seq)
        s0 = s0[:, l]

    yc, s_fin = hgrn2_mix(proj, p['hg_lb'], p['hg_norm_w'][l], s0, layer=l, batch=batch, seq=seq)

    mixed = branch_merge(ya, yb, yc, proj, p['w_branch_a'][l], p['w_branch_b'][l], p['w_branch_c'][l],
                         tm=tm, tn=512)
    x = matmul_gated_residual(mixed, p['w_out'][l], x, mod, 2, tm=tm, tn=512, **geo)

    j = l // 2
    if l % 2 == 0:
        f = ln_mod_glu(x, p['norm2_w'][l], mod, (4, 3), p['ffn_w1'][j], p['ffn_w3'][j], tm=tm, tn=512, **geo)
        x = matmul_gated_residual(f, p['ffn_w2'][j], x, mod, 5, tm=512, tn=512,
                                  layer=l, tiles_per_row=per_row * 2, row0=row0)
        if final:
            x = final_norm(x, p['final_norm_w'], tm=512)
    else:
        h2, comb = router(x, p['norm2_w'][l], mod, (4, 3), p['router_w'][j], tm=512,
                          layer=l, tiles_per_row=per_row * 2, row0=row0)
        y = jnp.zeros((t, d), F32)
        for e in range(N_EXPERTS):
            f = glu_up(h2, p['moe_w1'][j, e], p['moe_w3'][j, e], tm=tm, tn=512)
            y = matmul_rowscale_accumulate(f, p['moe_w2'][j, e], comb, y, e, tm=512, tn=512)
        x = gated_residual_norm(x, y, mod, 5, p['final_norm_w'], tm=512, final=final,
                                layer=l, tiles_per_row=per_row * 2, row0=row0)
    k = proj[:, OFF_AK:OFF_AK + KV_WIDTH]
    v = proj[:, OFF_AV:OFF_AV + KV_WIDTH]
    return x, k, v, s_fin


def kernel(x_prompt, x_sample, c, cache_k, cache_v, state_hgrn, c_ctx, ada_w, ada_b, norm1_w, norm2_w, w_in,
           hy_conv_w, hy_conv_b, hy_w1, hy_b1, hy_freq, hy_w2, hy_b2, hy_w3, hy_decay, hy_bias, attn_sink,
           hg_lb, hg_norm_w, w_branch_a, w_branch_b, w_branch_c, w_out, ffn_w1, ffn_w3, ffn_w2, router_w,
           moe_w1, moe_w3, moe_w2, final_norm_w):
    batch, seq, d = x_prompt.shape
    dbatch, dseq, _ = x_sample.shape
    depth = ada_w.shape[0]
    bf = lambda a: a.astype(BF16)
    p = dict(norm1_w=norm1_w, norm2_w=norm2_w, w_in=bf(w_in), hy_conv_w=hy_conv_w, hy_conv_b=hy_conv_b,
             hy_w1=hy_w1, hy_b1=hy_b1, hy_freq=hy_freq, hy_w2=hy_w2, hy_b2=hy_b2, hy_w3=hy_w3,
             hy_decay=hy_decay, hy_bias=hy_bias, attn_sink=attn_sink, hg_lb=hg_lb, hg_norm_w=hg_norm_w,
             w_branch_a=bf(w_branch_a), w_branch_b=bf(w_branch_b), w_branch_c=bf(w_branch_c),
             w_out=bf(w_out), ffn_w1=bf(ffn_w1), ffn_w3=bf(ffn_w3), ffn_w2=bf(ffn_w2), router_w=router_w,
             moe_w1=bf(moe_w1), moe_w3=bf(moe_w3), moe_w2=bf(moe_w2), final_norm_w=final_norm_w)

    nrows = 16
    cond = jnp.zeros((nrows, d), F32).at[:dbatch].set(c).at[dbatch].set(c_ctx)
    mod = ada_modulation(cond, ada_w, ada_b).reshape(depth, nrows, 6, 1, d)

    xp = x_prompt.reshape(batch * seq, d)
    ks, vs, ss = [], [], []
    for l in range(depth):
        xp, k_l, v_l, s_l = _trunk_layer(xp, l, p, batch=batch, seq=seq, mod=mod, row0=dbatch, ctx=None,
                                         final=(l == depth - 1))
        ks.append(k_l.reshape(batch, seq, N_KV_HEADS, HEAD_DIM))
        vs.append(v_l.reshape(batch, seq, N_KV_HEADS, HEAD_DIM))
        ss.append(s_l)
    y_prompt = xp.reshape(batch, seq, d)
    new_cache_k = jnp.stack(ks, axis=1)
    new_cache_v = jnp.stack(vs, axis=1)
    new_state = jnp.stack(ss, axis=1)

    past = cache_k.shape[2]
    ck = cache_k.reshape(dbatch, depth, past, KV_WIDTH)
    cv = cache_v.reshape(dbatch, depth, past, KV_WIDTH)
    xs = x_sample.reshape(dbatch * dseq, d)
    for l in range(depth):
        xs, _, _, _ = _trunk_layer(xs, l, p, batch=dbatch, seq=dseq, mod=mod, row0=0,
                                   ctx=(ck, cv, state_hgrn), final=(l == depth - 1))
    y_sample = xs.reshape(dbatch, dseq, d)
    return (y_prompt, y_sample, new_cache_k, new_cache_v, new_state)
```

```python
import functools
import math

import numpy as np
import jax
import jax.numpy as jnp
from jax import lax
from jax.experimental import pallas as pl
from jax.experimental.pallas import tpu as pltpu

F32 = jnp.float32
BF16 = jnp.bfloat16

VMEM_LIMIT_BYTES = 56 * 1024 * 1024
LANES = 128

EPS = 1e-6
NEG_BIG = -1e30
LB_FLOOR = 1e-30
GRID_W = 64
HY_WIDTH = 1024
HY_BANDS = 16
N_HEADS = 8
N_KV_HEADS = 2
GROUP = N_HEADS // N_KV_HEADS
HEAD_DIM = 128
ATTN_WIDTH = N_HEADS * HEAD_DIM
KV_WIDTH = N_KV_HEADS * HEAD_DIM
WINDOW = 128
ROPE_BASE = 10000.0
ATTN_SCALE = HEAD_DIM ** -0.5
HG_HEADS = 8
HG_DK = 128
HG_DV = 128
HG_WIDTH = HG_HEADS * HG_DK
HG_CHUNK = 128
N_EXPERTS = 8
TOP_K = 2

OFF_HY = 0
OFF_AQ = 3 * HY_WIDTH
OFF_AK = OFF_AQ + ATTN_WIDTH
OFF_AV = OFF_AK + KV_WIDTH
OFF_HQ = OFF_AV + KV_WIDTH
OFF_FF = OFF_HQ + HG_WIDTH
OFF_FB = OFF_FF + HG_WIDTH
OFF_HI = OFF_FB + HG_WIDTH
OFF_HG = OFF_HI + HG_WIDTH
OFF_MA = OFF_HG + HG_WIDTH


def _cparams(*sem):
    return pltpu.CompilerParams(dimension_semantics=sem, vmem_limit_bytes=VMEM_LIMIT_BYTES)


def _split3(x):
    hi = x.astype(BF16)
    r1 = x - hi.astype(F32)
    mid = r1.astype(BF16)
    lo = (r1 - mid.astype(F32)).astype(BF16)
    return hi, mid, lo


def _dot(a, b):
    return jnp.dot(a, b, preferred_element_type=F32)


def _dot_f32(a, b):
    a0, a1, a2 = _split3(a)
    b0, b1, b2 = _split3(b)
    return (_dot(a0, b0) + (_dot(a0, b1) + _dot(a1, b0))
            + (_dot(a0, b2) + _dot(a1, b1) + _dot(a2, b0)))


def _silu(x):
    return x * jax.nn.sigmoid(x)


def _ada_kernel(c_ref, w_ref, b_ref, o_ref):
    o_ref[...] = _dot_f32(_silu(c_ref[...]), w_ref[...]) + b_ref[...]


def ada_modulation(cond, ada_w, ada_b, tn=512):
    depth, d, n = ada_w.shape
    rows = cond.shape[0]
    return pl.pallas_call(
        _ada_kernel,
        grid=(depth, n // tn),
        in_specs=[pl.BlockSpec((rows, d), lambda l, j: (0, 0)),
                  pl.BlockSpec((None, d, tn), lambda l, j: (l, 0, j)),
                  pl.BlockSpec((None, 1, tn), lambda l, j: (l, 0, j))],
        out_specs=pl.BlockSpec((None, rows, tn), lambda l, j: (l, 0, j)),
        out_shape=jax.ShapeDtypeStruct((depth, rows, n), F32),
        compiler_params=_cparams("parallel", "parallel"),
        name="ada_modulation",
    )(cond, ada_w, ada_b.reshape(depth, 1, n))


def _norm_modulate(x, nw, sc, sh):
    ms = jnp.mean(x * x, axis=-1, keepdims=True)
    y = x * lax.rsqrt(ms + EPS) * nw
    return y * (1.0 + sc) + sh


def _mod_index(layer, j, tiles_per_row, row0):
    def index(i, *_):
        return (layer, row0 + i // tiles_per_row, j, 0, 0)
    return index


def _ln_mm_kernel(x_ref, nw_ref, sc_ref, sh_ref, w_ref, o_ref, h_ref):
    @pl.when(pl.program_id(1) == 0)
    def _():
        h_ref[...] = _norm_modulate(x_ref[...], nw_ref[...], sc_ref[...], sh_ref[...]).astype(BF16)

    o_ref[...] = _dot(h_ref[...], w_ref[...]).astype(o_ref.dtype)


def ln_mod_matmul(x, nw, mod, mod_idx, w, *, layer, tiles_per_row, row0, tm, tn, out_dtype):
    t, d = x.shape
    n = w.shape[1]
    sc_j, sh_j = mod_idx
    mspec = lambda j: pl.BlockSpec((None, None, None, 1, d), _mod_index(layer, j, tiles_per_row, row0))
    return pl.pallas_call(
        _ln_mm_kernel,
        grid=(t // tm, n // tn),
        in_specs=[pl.BlockSpec((tm, d), lambda i, j: (i, 0)),
                  pl.BlockSpec((1, d), lambda i, j: (0, 0)),
                  mspec(sc_j), mspec(sh_j),
                  pl.BlockSpec((d, tn), lambda i, j: (0, j))],
        out_specs=pl.BlockSpec((tm, tn), lambda i, j: (i, j)),
        out_shape=jax.ShapeDtypeStruct((t, n), out_dtype),
        scratch_shapes=[pltpu.VMEM((tm, d), BF16)],
        compiler_params=_cparams("parallel", "arbitrary"),
        name="ln_mod_matmul",
    )(x, nw.reshape(1, d), mod, mod, w)


def _ln_glu_kernel(x_ref, nw_ref, sc_ref, sh_ref, w1_ref, w3_ref, o_ref, h_ref):
    @pl.when(pl.program_id(1) == 0)
    def _():
        h_ref[...] = _norm_modulate(x_ref[...], nw_ref[...], sc_ref[...], sh_ref[...]).astype(BF16)

    h = h_ref[...]
    o_ref[...] = (_silu(_dot(h, w1_ref[...])) * _dot(h, w3_ref[...])).astype(o_ref.dtype)


def ln_mod_glu(x, nw, mod, mod_idx, w1, w3, *, layer, tiles_per_row, row0, tm, tn):
    t, d = x.shape
    n = w1.shape[1]
    sc_j, sh_j = mod_idx
    mspec = lambda j: pl.BlockSpec((None, None, None, 1, d), _mod_index(layer, j, tiles_per_row, row0))
    return pl.pallas_call(
        _ln_glu_kernel,
        grid=(t // tm, n // tn),
        in_specs=[pl.BlockSpec((tm, d), lambda i, j: (i, 0)),
                  pl.BlockSpec((1, d), lambda i, j: (0, 0)),
                  mspec(sc_j), mspec(sh_j),
                  pl.BlockSpec((d, tn), lambda i, j: (0, j)),
                  pl.BlockSpec((d, tn), lambda i, j: (0, j))],
        out_specs=pl.BlockSpec((tm, tn), lambda i, j: (i, j)),
        out_shape=jax.ShapeDtypeStruct((t, n), BF16),
        scratch_shapes=[pltpu.VMEM((tm, d), BF16)],
        compiler_params=_cparams("parallel", "arbitrary"),
        name="ln_mod_glu",
    )(x, nw.reshape(1, d), mod, mod, w1, w3)


def _merge_kernel(ya_ref, yb_ref, yc_ref, ma_ref, mb_ref, mc_ref, wa_ref, wb_ref, wc_ref, o_ref):
    acc = jax.nn.sigmoid(ma_ref[...]) * _dot(ya_ref[...], wa_ref[...])
    acc = acc + jax.nn.sigmoid(mb_ref[...]) * _dot(yb_ref[...], wb_ref[...])
    acc = acc + jax.nn.sigmoid(mc_ref[...]) * _dot(yc_ref[...], wc_ref[...])
    o_ref[...] = acc.astype(o_ref.dtype)


def branch_merge(ya, yb, yc, proj, wa, wb, wc, *, tm, tn):
    t, k = ya.shape
    n = wa.shape[1]
    gate = lambda off: pl.BlockSpec((tm, tn), lambda i, j: (i, off // tn + j))
    yspec = pl.BlockSpec((tm, k), lambda i, j: (i, 0))
    wspec = pl.BlockSpec((k, tn), lambda i, j: (0, j))
    return pl.pallas_call(
        _merge_kernel,
        grid=(t // tm, n // tn),
        in_specs=[yspec, yspec, yspec, gate(OFF_MA), gate(OFF_MA + n), gate(OFF_MA + 2 * n),
                  wspec, wspec, wspec],
        out_specs=pl.BlockSpec((tm, tn), lambda i, j: (i, j)),
        out_shape=jax.ShapeDtypeStruct((t, n), BF16),
        compiler_params=_cparams("parallel", "parallel"),
        name="branch_merge",
    )(ya, yb, yc, proj, proj, proj, wa, wb, wc)


def _mm_resid_kernel(a_ref, w_ref, x_ref, g_ref, o_ref):
    o_ref[...] = x_ref[...] + g_ref[...] * _dot(a_ref[...], w_ref[...])


def matmul_gated_residual(a, w, x, mod, g_j, *, layer, tiles_per_row, row0, tm, tn):
    t, k = a.shape
    n = w.shape[1]
    return pl.pallas_call(
        _mm_resid_kernel,
        grid=(t // tm, n // tn),
        in_specs=[pl.BlockSpec((tm, k), lambda i, j: (i, 0)),
                  pl.BlockSpec((k, tn), lambda i, j: (0, j)),
                  pl.BlockSpec((tm, tn), lambda i, j: (i, j)),
                  pl.BlockSpec((None, None, None, 1, tn),
                               lambda i, j: (layer, row0 + i // tiles_per_row, g_j, 0, j))],
        out_specs=pl.BlockSpec((tm, tn), lambda i, j: (i, j)),
        out_shape=jax.ShapeDtypeStruct((t, n), F32),
        compiler_params=_cparams("parallel", "parallel"),
        name="matmul_gated_residual",
    )(a, w, x, mod)


def _final_norm_kernel(x_ref, fw_ref, o_ref):
    x = x_ref[...]
    o_ref[...] = x * lax.rsqrt(jnp.mean(x * x, axis=-1, keepdims=True) + EPS) * fw_ref[...]


def final_norm(x, fw, *, tm):
    t, d = x.shape
    return pl.pallas_call(
        _final_norm_kernel,
        grid=(t // tm,),
        in_specs=[pl.BlockSpec((tm, d), lambda i: (i, 0)), pl.BlockSpec((1, d), lambda i: (0, 0))],
        out_specs=pl.BlockSpec((tm, d), lambda i: (i, 0)),
        out_shape=jax.ShapeDtypeStruct((t, d), F32),
        compiler_params=_cparams("parallel"),
        name="final_norm",
    )(x, fw.reshape(1, d))


MOE_ROW_TILE = 512
MOE_TOKEN_TILE = 256


def _router_kernel(x_ref, nw_ref, sc_ref, sh_ref, rw_ref, h_ref, tw_ref, route_ref, cnt_ref, run_ref):
    @pl.when(pl.program_id(0) == 0)
    def _():
        run_ref[...] = jnp.zeros_like(run_ref)

    h = _norm_modulate(x_ref[...], nw_ref[...], sc_ref[...], sh_ref[...])
    h_ref[...] = h
    tm = h.shape[0]
    lane = lax.broadcasted_iota(jnp.int32, (tm, LANES), 1)
    logits = jnp.where(lane < N_EXPERTS, _dot_f32(h, rw_ref[...]), -jnp.inf)
    m1 = jnp.max(logits, axis=-1, keepdims=True)
    i1 = jnp.min(jnp.where(logits == m1, lane, LANES), axis=-1, keepdims=True)
    rest = jnp.where(lane == i1, -jnp.inf, logits)
    m2 = jnp.max(rest, axis=-1, keepdims=True)
    i2 = jnp.min(jnp.where(rest == m2, lane, LANES), axis=-1, keepdims=True)
    e2 = jnp.exp(m2 - m1)
    inv = 1.0 / (1.0 + e2)
    tw_ref[...] = jnp.where(lane == 0, inv, jnp.where(lane == 1, e2 * inv, 0.0))

    sel = jnp.where((lane == i1) | (lane == i2), 1.0, 0.0)
    before = lax.broadcasted_iota(jnp.int32, (tm, tm), 1) < lax.broadcasted_iota(jnp.int32, (tm, tm), 0)
    rank = _dot(jnp.where(before, 1.0, 0.0).astype(BF16), sel.astype(BF16)) + run_ref[0:1, :]
    r1 = jnp.sum(jnp.where(lane == i1, rank, 0.0), axis=-1, keepdims=True).astype(jnp.int32)
    r2 = jnp.sum(jnp.where(lane == i2, rank, 0.0), axis=-1, keepdims=True).astype(jnp.int32)
    route_ref[...] = jnp.where(lane == 0, i1, jnp.where(lane == 1, i2, jnp.where(lane == 2, r1, r2)))
    run_ref[...] = run_ref[...] + jnp.sum(sel, axis=0, keepdims=True)
    cnt_ref[...] = run_ref[...]


def router(x, nw, mod, mod_idx, rw, *, layer, tiles_per_row, row0, tm):
    t, d = x.shape
    sc_j, sh_j = mod_idx
    rw_pad = jnp.zeros((d, LANES), F32).at[:, :N_EXPERTS].set(rw)
    mspec = lambda j: pl.BlockSpec((None, None, None, 1, d), _mod_index(layer, j, tiles_per_row, row0))
    return pl.pallas_call(
        _router_kernel,
        grid=(t // tm,),
        in_specs=[pl.BlockSpec((tm, d), lambda i: (i, 0)),
                  pl.BlockSpec((1, d), lambda i: (0, 0)),
                  mspec(sc_j), mspec(sh_j),
                  pl.BlockSpec((d, LANES), lambda i: (0, 0))],
        out_specs=[pl.BlockSpec((tm, d), lambda i: (i, 0)),
                   pl.BlockSpec((tm, LANES), lambda i: (i, 0)),
                   pl.BlockSpec((tm, LANES), lambda i: (i, 0)),
                   pl.BlockSpec((8, LANES), lambda i: (0, 0))],
        out_shape=[jax.ShapeDtypeStruct((t, d), F32), jax.ShapeDtypeStruct((t, LANES), F32),
                   jax.ShapeDtypeStruct((t, LANES), jnp.int32), jax.ShapeDtypeStruct((8, LANES), F32)],
        scratch_shapes=[pltpu.VMEM((8, LANES), F32)],
        compiler_params=_cparams("arbitrary"),
        name="router",
    )(x, nw.reshape(1, d), mod, mod, rw_pad)


def _row_copy(src, src_row, dst, dst_row, sem):
    return pltpu.make_async_copy(src.at[pl.ds(src_row, 1), :], dst.at[pl.ds(dst_row, 1), :], sem)


def _dispatch_kernel(pos_ref, h_ref, xs_in_ref, xs_ref, sem):
    del xs_in_ref
    nt = h_ref.shape[0]

    def start(r, carry):
        for k in range(TOP_K):
            _row_copy(h_ref, r, xs_ref, pos_ref[0, k * nt + r], sem).start()
        return carry

    def wait(r, carry):
        for k in range(TOP_K):
            _row_copy(h_ref, r, xs_ref, pos_ref[0, k * nt + r], sem).wait()
        return carry

    lax.fori_loop(0, nt, start, 0)
    lax.fori_loop(0, nt, wait, 0)


def moe_dispatch(h, pos, rows):
    t, d = h.shape
    nt = MOE_TOKEN_TILE
    return pl.pallas_call(
        _dispatch_kernel,
        grid=(t // nt,),
        in_specs=[pl.BlockSpec((None, 1, TOP_K * nt), lambda i: (i, 0, 0), memory_space=pltpu.SMEM),
                  pl.BlockSpec((nt, d), lambda i: (i, 0)),
                  pl.BlockSpec(memory_space=pl.ANY)],
        out_specs=pl.BlockSpec(memory_space=pl.ANY),
        out_shape=jax.ShapeDtypeStruct((rows, d), F32),
        scratch_shapes=[pltpu.SemaphoreType.DMA(())],
        input_output_aliases={2: 0},
        compiler_params=_cparams("arbitrary"),
        name="moe_dispatch",
    )(pos, h, jnp.zeros((rows, d), F32))


def _gmm_up_kernel(te_ref, nu_ref, xs_ref, w1_ref, w3_ref, o_ref, a_ref):
    del te_ref
    used = pl.program_id(0) < nu_ref[0]

    @pl.when(used)
    def _():
        @pl.when(pl.program_id(1) == 0)
        def _():
            a_ref[...] = xs_ref[...].astype(BF16)

        a = a_ref[...]
        o_ref[...] = (_silu(_dot(a, w1_ref[...])) * _dot(a, w3_ref[...])).astype(o_ref.dtype)

    @pl.when(jnp.logical_not(used))
    def _():
        o_ref[...] = jnp.zeros_like(o_ref)


def gmm_up(xs, w1, w3, tile_expert, n_used, *, tn):
    rows, d = xs.shape
    n = w1.shape[2]
    tm = MOE_ROW_TILE
    wspec = pl.BlockSpec((None, d, tn), lambda i, j, te, nu: (te[i], 0, j))
    return pl.pallas_call(
        _gmm_up_kernel,
        grid_spec=pltpu.PrefetchScalarGridSpec(
            num_scalar_prefetch=2,
            grid=(rows // tm, n // tn),
            in_specs=[pl.BlockSpec((tm, d), lambda i, j, te, nu: (i, 0)), wspec, wspec],
            out_specs=pl.BlockSpec((tm, tn), lambda i, j, te, nu: (i, j)),
            scratch_shapes=[pltpu.VMEM((tm, d), BF16)]),
        out_shape=jax.ShapeDtypeStruct((rows, n), BF16),
        compiler_params=_cparams("parallel", "arbitrary"),
        name="gmm_up",
    )(tile_expert, n_used, xs, w1, w3)


def _gmm_down_kernel(te_ref, nu_ref, f_ref, w_ref, o_ref):
    del te_ref
    used = pl.program_id(0) < nu_ref[0]

    @pl.when(used)
    def _():
        o_ref[...] = _dot(f_ref[...], w_ref[...])

    @pl.when(jnp.logical_not(used))
    def _():
        o_ref[...] = jnp.zeros_like(o_ref)


def gmm_down(f, w2, tile_expert, n_used, *, tn):
    rows, k = f.shape
    n = w2.shape[2]
    tm = MOE_ROW_TILE
    return pl.pallas_call(
        _gmm_down_kernel,
        grid_spec=pltpu.PrefetchScalarGridSpec(
            num_scalar_prefetch=2,
            grid=(rows // tm, n // tn),
            in_specs=[pl.BlockSpec((tm, k), lambda i, j, te, nu: (i, 0)),
                      pl.BlockSpec((None, k, tn), lambda i, j, te, nu: (te[i], 0, j))],
            out_specs=pl.BlockSpec((tm, tn), lambda i, j, te, nu: (i, j))),
        out_shape=jax.ShapeDtypeStruct((rows, n), F32),
        compiler_params=_cparams("parallel", "parallel"),
        name="gmm_down",
    )(tile_expert, n_used, f, w2)


def _combine_kernel(final, pos_ref, x_ref, tw_ref, g_ref, fw_ref, ys_ref, o_ref, y0_ref, y1_ref, sem):
    nt = x_ref.shape[0]
    bufs = (y0_ref, y1_ref)

    def start(r, carry):
        for k in range(TOP_K):
            _row_copy(ys_ref, pos_ref[0, k * nt + r], bufs[k], r, sem).start()
        return carry

    def wait(r, carry):
        for k in range(TOP_K):
            _row_copy(ys_ref, pos_ref[0, k * nt + r], bufs[k], r, sem).wait()
        return carry

    lax.fori_loop(0, nt, start, 0)
    lax.fori_loop(0, nt, wait, 0)
    tw = tw_ref[...]
    x = x_ref[...] + g_ref[...] * (tw[:, 0:1] * y0_ref[...] + tw[:, 1:2] * y1_ref[...])
    if final:
        x = x * lax.rsqrt(jnp.mean(x * x, axis=-1, keepdims=True) + EPS) * fw_ref[...]
    o_ref[...] = x


def moe_combine(x, ys, pos, tw, mod, g_j, fw, *, layer, tiles_per_row, row0, final):
    t, d = x.shape
    nt = MOE_TOKEN_TILE
    return pl.pallas_call(
        functools.partial(_combine_kernel, final),
        grid=(t // nt,),
        in_specs=[pl.BlockSpec((None, 1, TOP_K * nt), lambda i: (i, 0, 0), memory_space=pltpu.SMEM),
                  pl.BlockSpec((nt, d), lambda i: (i, 0)),
                  pl.BlockSpec((nt, LANES), lambda i: (i, 0)),
                  pl.BlockSpec((None, None, None, 1, d), _mod_index(layer, g_j, tiles_per_row, row0)),
                  pl.BlockSpec((1, d), lambda i: (0, 0)),
                  pl.BlockSpec(memory_space=pl.ANY)],
        out_specs=pl.BlockSpec((nt, d), lambda i: (i, 0)),
        out_shape=jax.ShapeDtypeStruct((t, d), F32),
        scratch_shapes=[pltpu.VMEM((nt, d), F32), pltpu.VMEM((nt, d), F32), pltpu.SemaphoreType.DMA(())],
        compiler_params=_cparams("arbitrary"),
        name="moe_combine",
    )(pos, x, tw, mod, fw.reshape(1, d), ys)


def _moe_plan(route, counts, t):
    tm = MOE_ROW_TILE
    nt = MOE_TOKEN_TILE
    n_tiles = TOP_K * t // tm + N_EXPERTS
    cnt = counts[0, :N_EXPERTS].astype(jnp.int32)
    tiles = (cnt + tm - 1) // tm
    ends = jnp.cumsum(tiles)
    offs = (ends - tiles) * tm
    pos = [jnp.take(offs, route[:, k]) + route[:, TOP_K + k] for k in range(TOP_K)]
    pos = jnp.concatenate([p.reshape(t // nt, nt) for p in pos], axis=1).reshape(t // nt, 1, TOP_K * nt)
    tile_expert = jnp.sum(jnp.arange(n_tiles, dtype=jnp.int32)[:, None] >= ends[None, :], axis=1)
    tile_expert = jnp.minimum(tile_expert, N_EXPERTS - 1).astype(jnp.int32)
    return pos, tile_expert, ends[-1:].astype(jnp.int32), n_tiles * tm


def _hyena_filter_kernel(z_ref, w1_ref, b1_ref, fr_ref, w2_ref, b2_ref, w3_ref, t_ref, dec_ref, o_ref):
    fr = fr_ref[...]
    h = jnp.sin(fr * (_dot_f32(z_ref[...], w1_ref[...]) + b1_ref[...]))
    h = jnp.sin(fr * (_dot_f32(h, w2_ref[...]) + b2_ref[...]))
    o_ref[...] = _dot_f32(h, w3_ref[...]) * jnp.exp(-t_ref[...] * jnp.abs(dec_ref[...]))


def hyena_filter(seq, w1, b1, freq, w2, b2, w3, decay, tl=256):
    t = jnp.linspace(0.0, 1.0, seq, dtype=F32)[:, None]
    pos = jnp.arange(seq, dtype=F32)[:, None]
    bands = jnp.linspace(1e-4, HY_BANDS - 1.0, HY_BANDS, dtype=F32)[None, :]
    ang = (2.0 * math.pi / seq) * pos * bands
    z = jnp.concatenate([t, jnp.cos(ang), -jnp.sin(ang)], axis=-1)
    emb = z.shape[1]
    emb_pad = LANES
    z = jnp.pad(z, ((0, 0), (0, emb_pad - emb)))
    w1p = jnp.pad(w1, ((0, emb_pad - emb), (0, 0)))
    hid = w1.shape[1]
    n = w3.shape[1]
    tl = min(tl, seq)
    full = lambda shape: pl.BlockSpec(shape, lambda i: (0, 0))
    return pl.pallas_call(
        _hyena_filter_kernel,
        grid=(seq // tl,),
        in_specs=[pl.BlockSpec((tl, emb_pad), lambda i: (i, 0)),
                  full((emb_pad, hid)), full((1, hid)), full((1, hid)),
                  full((hid, hid)), full((1, hid)), full((hid, n)),
                  pl.BlockSpec((tl, 1), lambda i: (i, 0)), full((1, n))],
        out_specs=pl.BlockSpec((tl, n), lambda i: (i, 0)),
        out_shape=jax.ShapeDtypeStruct((seq, n), F32),
        compiler_params=_cparams("parallel"),
        name="hyena_filter",
    )(z, w1p, b1.reshape(1, hid), freq.reshape(1, hid), w2, b2.reshape(1, hid), w3, t,
      decay.reshape(1, n))


def _dft_tables(seq, kb):
    n = 2 * seq
    k = jnp.arange(seq, dtype=jnp.int32)[:, None]
    s = jnp.arange(seq, dtype=jnp.int32)[None, :]
    ang = ((k * s) % n).astype(F32) * (2.0 * math.pi / n)
    cos = jnp.cos(ang)
    sin = jnp.sin(ang)
    nyq = jnp.where(s % 2 == 0, 1.0, -1.0).astype(F32)
    is0 = k == 0
    f_re = cos
    f_im = jnp.where(is0, nyq, -sin)
    i_re = jnp.where(is0, 1.0 / n, (2.0 / n) * cos)
    i_im = jnp.where(is0, nyq / n, -(2.0 / n) * sin)
    nkb = seq // kb
    fwd = jnp.concatenate([f_re.reshape(nkb, kb, seq), f_im.reshape(nkb, kb, seq)], axis=1)
    inv = jnp.concatenate([i_re.reshape(nkb, kb, seq), i_im.reshape(nkb, kb, seq)], axis=1)
    return fwd, jnp.swapaxes(inv, 1, 2)


def _spectrum_kernel(kb, f_ref, hf_ref, hb_ref, o_ref):
    f = f_ref[...]
    row = lax.broadcasted_iota(jnp.int32, hb_ref.shape, 0)
    hb0 = jnp.where(row == 0, 0.0, hb_ref[...])
    a = _dot_f32(f, hf_ref[...])
    b = _dot_f32(f, hb0)
    orow = lax.broadcasted_iota(jnp.int32, a.shape, 0)
    nyq_slot = (orow == kb) & (pl.program_id(0) == 0)
    o_ref[...] = jnp.where((orow < kb) | nyq_slot, a + b, a - b)


def hyena_spectrum(fwd_f32, filt, kb, tc=256):
    nkb, kb2, seq = fwd_f32.shape
    c = filt.shape[1] // 2
    return pl.pallas_call(
        functools.partial(_spectrum_kernel, kb),
        grid=(nkb, c // tc),
        in_specs=[pl.BlockSpec((None, kb2, seq), lambda j, i: (j, 0, 0)),
                  pl.BlockSpec((seq, tc), lambda j, i: (0, i)),
                  pl.BlockSpec((seq, tc), lambda j, i: (0, c // tc + i))],
        out_specs=pl.BlockSpec((None, kb2, tc), lambda j, i: (j, 0, i)),
        out_shape=jax.ShapeDtypeStruct((nkb, kb2, c), F32),
        compiler_params=_cparams("parallel", "parallel"),
        name="hyena_spectrum",
    )(fwd_f32, filt, filt)


def _hyena_conv_kernel(kb, x0_ref, x1_ref, v_ref, cw0_ref, cw1_ref, cwv_ref, cb0_ref, cb1_ref, cbv_ref,
                       bias_ref, f_ref, i_ref, kf_ref, o_ref, u16_ref, u32_ref, x0c_ref, acc_ref):
    j = pl.program_id(2)
    seq = x0_ref.shape[0]

    def conv3(x_ref, w_ref, b_ref):
        x = x_ref[...]
        row = lax.broadcasted_iota(jnp.int32, x.shape, 0)
        prev = jnp.where(row == 0, 0.0, pltpu.roll(x, 1, 0))
        nxt = jnp.where(row == seq - 1, 0.0, pltpu.roll(x, seq - 1, 0))
        w = w_ref[...]
        return prev * w[0:1, :] + x * w[1:2, :] + nxt * w[2:3, :] + b_ref[...]

    @pl.when(j == 0)
    def _():
        u = conv3(v_ref, cwv_ref, cbv_ref) * conv3(x1_ref, cw1_ref, cb1_ref)
        u32_ref[...] = u
        u16_ref[...] = u.astype(BF16)
        x0c_ref[...] = conv3(x0_ref, cw0_ref, cb0_ref)
        acc_ref[...] = jnp.zeros_like(acc_ref)

    spec = _dot(f_ref[...], u16_ref[...])
    xr, xi = spec[:kb], spec[kb:]
    kf = kf_ref[...]
    kr, ki = kf[:kb], kf[kb:]
    packed = (lax.broadcasted_iota(jnp.int32, xr.shape, 0) == 0) & (j == 0)
    yr = xr * kr - jnp.where(packed, 0.0, xi * ki)
    yi = jnp.where(packed, xi * ki, xr * ki + xi * kr)
    y = jnp.concatenate([yr, yi], axis=0).astype(BF16)
    acc_ref[...] += _dot(i_ref[...], y)

    @pl.when(j == pl.num_programs(2) - 1)
    def _():
        o_ref[...] = ((acc_ref[...] + u32_ref[...] * bias_ref[...]) * x0c_ref[...]).astype(o_ref.dtype)


def hyena_conv(proj, conv_w, conv_b, bias, fwd, inv, spectrum, *, batch, seq, ct, kb):
    c = HY_WIDTH
    nkb = seq // kb
    ncb = c // ct
    col = lambda part: pl.BlockSpec((seq, ct), lambda b, i, j: (b, part * ncb + i))
    cw = lambda part: pl.BlockSpec((3, ct), lambda b, i, j: (0, part * ncb + i))
    cb = lambda part: pl.BlockSpec((1, ct), lambda b, i, j: (0, part * ncb + i))
    conv_b = conv_b.reshape(1, 3 * c)
    return pl.pallas_call(
        functools.partial(_hyena_conv_kernel, kb),
        grid=(batch, ncb, nkb),
        in_specs=[col(0), col(1), col(2), cw(0), cw(1), cw(2), cb(0), cb(1), cb(2),
                  pl.BlockSpec((1, ct), lambda b, i, j: (0, i)),
                  pl.BlockSpec((None, 2 * kb, seq), lambda b, i, j: (j, 0, 0)),
                  pl.BlockSpec((None, seq, 2 * kb), lambda b, i, j: (j, 0, 0)),
                  pl.BlockSpec((None, 2 * kb, ct), lambda b, i, j: (j, 0, i))],
        out_specs=pl.BlockSpec((seq, ct), lambda b, i, j: (b, i)),
        out_shape=jax.ShapeDtypeStruct((batch * seq, c), BF16),
        scratch_shapes=[pltpu.VMEM((seq, ct), BF16), pltpu.VMEM((seq, ct), F32),
                        pltpu.VMEM((seq, ct), F32), pltpu.VMEM((seq, ct), F32)],
        compiler_params=_cparams("parallel", "parallel", "arbitrary"),
        name="hyena_conv",
    )(proj, proj, proj, conv_w, conv_w, conv_w, conv_b, conv_b, conv_b, bias.reshape(1, c),
      fwd, inv, spectrum)


def _rope(x, cos, sin_a, sin_b):
    return x * cos + pltpu.roll(x, HEAD_DIM - HEAD_DIM // 4, 1) * sin_a + pltpu.roll(x, HEAD_DIM // 4, 1) * sin_b


def _softmax_pv(scores, values, sink):
    m = sink
    for s in scores:
        m = jnp.maximum(m, jnp.max(s, axis=-1, keepdims=True))
    den = jnp.exp(sink - m)
    out = None
    for s, v in zip(scores, values):
        p = jnp.exp(s - m)
        den = den + jnp.sum(p, axis=-1, keepdims=True)
        pv = _dot(p.astype(BF16), v)
        out = pv if out is None else out + pv
    return out / den


def _dot_nt(a, b):
    return lax.dot_general(a, b, (((1,), (1,)), ((), ())), preferred_element_type=F32)


def _ctx_attn_kernel(sink_ref, q_ref, k_ref, v_ref, o_ref):
    kvh = pl.program_id(1)
    k = k_ref[...].astype(BF16)
    v = v_ref[...].astype(BF16)
    for g in range(GROUP):
        q = q_ref[:, g * HEAD_DIM:(g + 1) * HEAD_DIM].astype(BF16)
        s = _dot_nt(q, k) * ATTN_SCALE
        o = _softmax_pv([s], [v], sink_ref[kvh * GROUP + g])
        o_ref[:, g * HEAD_DIM:(g + 1) * HEAD_DIM] = o.astype(o_ref.dtype)


def context_attention(proj, sink, *, batch, seq):
    qw = GROUP * HEAD_DIM
    return pl.pallas_call(
        _ctx_attn_kernel,
        grid=(batch, N_KV_HEADS),
        in_specs=[pl.BlockSpec(memory_space=pltpu.SMEM),
                  pl.BlockSpec((seq, qw), lambda b, h: (b, OFF_AQ // qw + h)),
                  pl.BlockSpec((seq, HEAD_DIM), lambda b, h: (b, OFF_AK // HEAD_DIM + h)),
                  pl.BlockSpec((seq, HEAD_DIM), lambda b, h: (b, OFF_AV // HEAD_DIM + h))],
        out_specs=pl.BlockSpec((seq, qw), lambda b, h: (b, h)),
        out_shape=jax.ShapeDtypeStruct((batch * seq, ATTN_WIDTH), BF16),
        compiler_params=_cparams("parallel", "parallel"),
        name="context_attention",
    )(sink, proj, proj, proj)


def _lat_attn_kernel(seq, sink_ref, q_ref, k_ref, v_ref, ck_ref, cv_ref, cos_ref, sa_ref, sb_ref,
                     o_ref, kr_ref, vb_ref):
    kvh = pl.program_id(1)
    qb = pl.program_id(2)
    blk = q_ref.shape[0]
    nwin = 3 * blk

    @pl.when(qb == 0)
    def _():
        kr_ref[...] = _rope(k_ref[...], cos_ref[...], sa_ref[...], sb_ref[...]).astype(BF16)
        vb_ref[...] = v_ref[...].astype(BF16)

    start = pl.multiple_of(jnp.clip((qb - 1) * blk, 0, seq - nwin), blk)
    kw = kr_ref[pl.ds(start, nwin), :]
    vw = vb_ref[pl.ds(start, nwin), :]
    ck = ck_ref[...].astype(BF16)
    cv = cv_ref[...].astype(BF16)
    rows = pl.ds(pl.multiple_of(qb * blk, blk), blk)
    cos, sa, sb = cos_ref[rows, :], sa_ref[rows, :], sb_ref[rows, :]
    qpos = qb * blk + lax.broadcasted_iota(jnp.int32, (blk, nwin), 0)
    kpos = start + lax.broadcasted_iota(jnp.int32, (blk, nwin), 1)
    valid = jnp.abs(qpos - kpos) <= WINDOW
    for g in range(GROUP):
        q = _rope(q_ref[:, g * HEAD_DIM:(g + 1) * HEAD_DIM], cos, sa, sb).astype(BF16)
        s_lat = jnp.where(valid, _dot_nt(q, kw) * ATTN_SCALE, NEG_BIG)
        s_ctx = _dot_nt(q, ck) * ATTN_SCALE
        o = _softmax_pv([s_lat, s_ctx], [vw, cv], sink_ref[kvh * GROUP + g])
        o_ref[:, g * HEAD_DIM:(g + 1) * HEAD_DIM] = o.astype(o_ref.dtype)


def _rope_tables(seq):
    rows = seq // GRID_W
    row = jnp.repeat(jnp.arange(rows, dtype=F32), GRID_W)
    col = jnp.tile(jnp.arange(GRID_W, dtype=F32), rows)
    quarter = HEAD_DIM // 4
    inv = ROPE_BASE ** (-jnp.arange(quarter, dtype=F32) / quarter)
    ar = row[:, None] * inv
    ac = col[:, None] * inv
    ang = jnp.concatenate([ar, ar, ac, ac], axis=-1)
    cos, sin = jnp.cos(ang), jnp.sin(ang)
    first = (jnp.arange(HEAD_DIM) % (2 * quarter)) < quarter
    return cos, jnp.where(first, -sin, 0.0), jnp.where(first, 0.0, sin)


def latent_attention(proj, cache_k, cache_v, sink, *, layer, batch, seq, blk=128):
    qw = GROUP * HEAD_DIM
    past = cache_k.shape[2]
    nqb = seq // blk
    cos, sa, sb = _rope_tables(seq)
    table = pl.BlockSpec((seq, HEAD_DIM), lambda b, h, i: (0, 0))
    cache = pl.BlockSpec((None, None, past, HEAD_DIM), lambda b, h, i: (b, layer, 0, h))
    return pl.pallas_call(
        functools.partial(_lat_attn_kernel, seq),
        grid=(batch, N_KV_HEADS, nqb),
        in_specs=[pl.BlockSpec(memory_space=pltpu.SMEM),
                  pl.BlockSpec((blk, qw), lambda b, h, i: (b * nqb + i, OFF_AQ // qw + h)),
                  pl.BlockSpec((seq, HEAD_DIM), lambda b, h, i: (b, OFF_AK // HEAD_DIM + h)),
                  pl.BlockSpec((seq, HEAD_DIM), lambda b, h, i: (b, OFF_AV // HEAD_DIM + h)),
                  cache, cache, table, table, table],
        out_specs=pl.BlockSpec((blk, qw), lambda b, h, i: (b * nqb + i, h)),
        out_shape=jax.ShapeDtypeStruct((batch * seq, ATTN_WIDTH), BF16),
        scratch_shapes=[pltpu.VMEM((seq, HEAD_DIM), BF16), pltpu.VMEM((seq, HEAD_DIM), BF16)],
        compiler_params=_cparams("parallel", "parallel", "arbitrary"),
        name="latent_attention",
    )(sink, proj, proj, proj, cache_k, cache_v, cos, sa, sb)


_HG_LEVELS = tuple(HG_CHUNK >> (i + 1) for i in range(int(math.log2(HG_CHUNK))))


def _hgrn_tables():
    c = HG_CHUNK
    t = np.arange(c)[:, None]
    u = np.arange(c)[None, :]
    blocks = [(u <= t), (u > t)]
    for m in _HG_LEVELS:
        ref = (t // (2 * m)) * (2 * m) + m - 1
        second = (t % (2 * m)) >= m
        blocks.append(np.where(second, (u > ref) & (u <= t), (u > t) & (u <= ref)))
    fwd = np.concatenate(blocks, axis=0).astype(np.float32)
    bwd = np.concatenate([b[::-1, ::-1] for b in blocks], axis=0).astype(np.float32)
    s = u
    level = np.full((c, c), len(_HG_LEVELS) + 1, np.int32)
    level[t == s] = len(_HG_LEVELS)
    for i, m in enumerate(_HG_LEVELS):
        hit = (t // (2 * m) == s // (2 * m)) & ((t % (2 * m)) >= m) & ((s % (2 * m)) < m)
        level[hit] = i
    a = np.stack([fwd, bwd])
    a = np.concatenate([a, a], axis=2)
    lv = np.stack([level, level.T])
    return jnp.asarray(a, BF16), jnp.asarray(lv, jnp.int32)


def _hgrn_kernel(layer, has_s0, seq, *refs):
    if has_s0:
        (q_ref, ff_ref, fb_ref, i_ref, g_ref, lb_ref, nw_ref, a_ref, lv_ref, s0_ref,
         y_ref, sfin_ref, of_ref, ob_ref, st_ref) = refs
    else:
        (q_ref, ff_ref, fb_ref, i_ref, g_ref, lb_ref, nw_ref, a_ref, lv_ref,
         y_ref, sfin_ref, of_ref, ob_ref, st_ref) = refs
    c = HG_CHUNK
    nlev = len(_HG_LEVELS)
    nc = seq // c

    lbs = lb_ref[...]
    mx = jnp.max(lbs, axis=0, keepdims=True)
    ex = jnp.exp(lbs - mx)
    sm = ex / jnp.sum(ex, axis=0, keepdims=True)
    lb = jnp.zeros(sm.shape[1:], F32)
    for j in range(1, layer + 1):
        lb = lb + sm[j]

    for d in range(2):
        if has_s0:
            st_ref[d] = s0_ref[d].T
        else:
            st_ref[d] = jnp.zeros((HG_DV, HG_DK), F32)

    def chunk(d, ci, o_ref):
        rows = pl.ds(pl.multiple_of(ci * c, c), c)
        q = _silu(q_ref[rows, :]) * (HG_DK ** -0.5)
        fpre = (ff_ref if d == 0 else fb_ref)[rows, :]
        lbd = lb[d:d + 1, :]
        f = jnp.maximum(lbd, LB_FLOOR) + (1.0 - lbd) * jax.nn.sigmoid(fpre)
        lf = jnp.log(f)
        k = 1.0 - f
        v = i_ref[rows, :].astype(BF16)
        e = jnp.exp(_dot(a_ref[d], jnp.concatenate(_split3(lf)[:2], axis=0)))
        q_in = (q * e[0:c]).astype(BF16)
        k_out = (k * e[c:2 * c]).astype(BF16)
        lv = lv_ref[d]
        att = jnp.where(lv == nlev, _dot_nt(q.astype(BF16), k.astype(BF16)), 0.0)
        for i in range(nlev):
            em = e[(2 + i) * c:(3 + i) * c]
            att = att + jnp.where(lv == i, _dot_nt((q * em).astype(BF16), (k * em).astype(BF16)), 0.0)
        st = st_ref[d]
        o_ref[rows, :] = _dot_nt(q_in, st.astype(BF16)) + _dot(att.astype(BF16), v)
        last = (c - 1) if d == 0 else 0
        upd = lax.dot_general(v, k_out, (((0,), (0,)), ((), ())), preferred_element_type=F32)
        st_ref[d] = st * e[last:last + 1] + upd

    def body(ci, carry):
        chunk(0, ci, of_ref)
        chunk(1, nc - 1 - ci, ob_ref)
        return carry

    lax.fori_loop(0, nc, body, 0)

    o = of_ref[...] + ob_ref[...]
    o = o * lax.rsqrt(jnp.mean(o * o, axis=-1, keepdims=True) + EPS) * nw_ref[...] * _silu(g_ref[...])
    y_ref[...] = o.astype(y_ref.dtype)
    for d in range(2):
        sfin_ref[d] = st_ref[d].T


def hgrn2_mix(proj, hg_lb, norm_w, s0, *, layer, batch, seq):
    a_tab, lv_tab = _hgrn_tables()
    depth = hg_lb.shape[0]
    col = lambda off: pl.BlockSpec((seq, HG_DK), lambda b, h: (b, off // HG_DK + h))
    state = pl.BlockSpec((None, 2, None, HG_DK, HG_DV), lambda b, h: (b, 0, h, 0, 0))
    in_specs = [col(OFF_HQ), col(OFF_FF), col(OFF_FB), col(OFF_HI), col(OFF_HG),
                pl.BlockSpec((depth, 2, HG_DK), lambda b, h: (0, 0, h)),
                pl.BlockSpec((1, HG_DV), lambda b, h: (0, 0)),
                pl.BlockSpec(a_tab.shape, lambda b, h: (0, 0, 0)),
                pl.BlockSpec(lv_tab.shape, lambda b, h: (0, 0, 0))]
    args = [proj, proj, proj, proj, proj, hg_lb, norm_w.reshape(1, HG_DV), a_tab, lv_tab]
    if s0 is not None:
        in_specs.append(state)
        args.append(s0)
    return pl.pallas_call(
        functools.partial(_hgrn_kernel, layer, s0 is not None, seq),
        grid=(batch, HG_HEADS),
        in_specs=in_specs,
        out_specs=[pl.BlockSpec((seq, HG_DV), lambda b, h: (b, h)), state],
        out_shape=[jax.ShapeDtypeStruct((batch * seq, HG_WIDTH), BF16),
                   jax.ShapeDtypeStruct((batch, 2, HG_HEADS, HG_DK, HG_DV), F32)],
        scratch_shapes=[pltpu.VMEM((seq, HG_DV), F32), pltpu.VMEM((seq, HG_DV), F32),
                        pltpu.VMEM((2, HG_DV, HG_DK), F32)],
        compiler_params=_cparams("parallel", "parallel"),
        name="hgrn2_mix",
    )(*args)


def _trunk_layer(x, l, p, *, batch, seq, mod, row0, ctx, final):
    t, d = x.shape
    tm = 1024
    per_row = (t if ctx is None else seq) // tm
    geo = dict(layer=l, tiles_per_row=per_row, row0=row0)

    proj = ln_mod_matmul(x, p['norm1_w'][l], mod, (1, 0), p['w_in'][l], tm=tm, tn=512, out_dtype=F32, **geo)

    filt = hyena_filter(seq, p['hy_w1'][l], p['hy_b1'][l], p['hy_freq'][l], p['hy_w2'][l], p['hy_b2'][l],
                        p['hy_w3'][l], p['hy_decay'][l])
    kb = min(seq, 512)
    fwd, inv = _dft_tables(seq, kb)
    spectrum = hyena_spectrum(fwd, filt, kb)
    ya = hyena_conv(proj, p['hy_conv_w'][l], p['hy_conv_b'][l], p['hy_bias'][l], fwd.astype(BF16),
                    inv.astype(BF16), spectrum, batch=batch, seq=seq, ct=256, kb=kb)

    if ctx is None:
        yb = context_attention(proj, p['attn_sink'][l], batch=batch, seq=seq)
        s0 = None
    else:
        cache_k, cache_v, s0 = ctx
        yb = latent_attention(proj, cache_k, cache_v, p['attn_sink'][l], layer=l, batch=batch, seq=seq)
        s0 = s0[:, l]

    yc, s_fin = hgrn2_mix(proj, p['hg_lb'], p['hg_norm_w'][l], s0, layer=l, batch=batch, seq=seq)

    mixed = branch_merge(ya, yb, yc, proj, p['w_branch_a'][l], p['w_branch_b'][l], p['w_branch_c'][l],
                         tm=tm, tn=512)
    x = matmul_gated_residual(mixed, p['w_out'][l], x, mod, 2, tm=tm, tn=512, **geo)

    j = l // 2
    if l % 2 == 0:
        f = ln_mod_glu(x, p['norm2_w'][l], mod, (4, 3), p['ffn_w1'][j], p['ffn_w3'][j], tm=tm, tn=512, **geo)
        x = matmul_gated_residual(f, p['ffn_w2'][j], x, mod, 5, tm=512, tn=512,
                                  layer=l, tiles_per_row=per_row * 2, row0=row0)
        if final:
            x = final_norm(x, p['final_norm_w'], tm=512)
    else:
        h2, tw, route, counts = router(x, p['norm2_w'][l], mod, (4, 3), p['router_w'][j], tm=512,
                                       layer=l, tiles_per_row=per_row * 2, row0=row0)
        pos, tile_expert, n_used, rows = _moe_plan(route, counts, t)
        xs = moe_dispatch(h2, pos, rows)
        f = gmm_up(xs, p['moe_w1'][j], p['moe_w3'][j], tile_expert, n_used, tn=512)
        ys = gmm_down(f, p['moe_w2'][j], tile_expert, n_used, tn=512)
        x = moe_combine(x, ys, pos, tw, mod, 5, p['final_norm_w'], final=final, layer=l,
                        tiles_per_row=per_row * (tm // MOE_TOKEN_TILE), row0=row0)
    k = proj[:, OFF_AK:OFF_AK + KV_WIDTH]
    v = proj[:, OFF_AV:OFF_AV + KV_WIDTH]
    return x, k, v, s_fin


def kernel(x_prompt, x_sample, c, cache_k, cache_v, state_hgrn, c_ctx, ada_w, ada_b, norm1_w, norm2_w, w_in,
           hy_conv_w, hy_conv_b, hy_w1, hy_b1, hy_freq, hy_w2, hy_b2, hy_w3, hy_decay, hy_bias, attn_sink,
           hg_lb, hg_norm_w, w_branch_a, w_branch_b, w_branch_c, w_out, ffn_w1, ffn_w3, ffn_w2, router_w,
           moe_w1, moe_w3, moe_w2, final_norm_w):
    batch, seq, d = x_prompt.shape
    dbatch, dseq, _ = x_sample.shape
    depth = ada_w.shape[0]
    bf = lambda a: a.astype(BF16)
    p = dict(norm1_w=norm1_w, norm2_w=norm2_w, w_in=bf(w_in), hy_conv_w=hy_conv_w, hy_conv_b=hy_conv_b,
             hy_w1=hy_w1, hy_b1=hy_b1, hy_freq=hy_freq, hy_w2=hy_w2, hy_b2=hy_b2, hy_w3=hy_w3,
             hy_decay=hy_decay, hy_bias=hy_bias, attn_sink=attn_sink, hg_lb=hg_lb, hg_norm_w=hg_norm_w,
             w_branch_a=bf(w_branch_a), w_branch_b=bf(w_branch_b), w_branch_c=bf(w_branch_c),
             w_out=bf(w_out), ffn_w1=bf(ffn_w1), ffn_w3=bf(ffn_w3), ffn_w2=bf(ffn_w2), router_w=router_w,
             moe_w1=bf(moe_w1), moe_w3=bf(moe_w3), moe_w2=bf(moe_w2), final_norm_w=final_norm_w)

    nrows = 16
    cond = jnp.zeros((nrows, d), F32).at[:dbatch].set(c).at[dbatch].set(c_ctx)
    mod = ada_modulation(cond, ada_w, ada_b).reshape(depth, nrows, 6, 1, d)

    xp = x_prompt.reshape(batch * seq, d)
    ks, vs, ss = [], [], []
    for l in range(depth):
        xp, k_l, v_l, s_l = _trunk_layer(xp, l, p, batch=batch, seq=seq, mod=mod, row0=dbatch, ctx=None,
                                         final=(l == depth - 1))
        ks.append(k_l.reshape(batch, seq, N_KV_HEADS, HEAD_DIM))
        vs.append(v_l.reshape(batch, seq, N_KV_HEADS, HEAD_DIM))
        ss.append(s_l)
    y_prompt = xp.reshape(batch, seq, d)
    new_cache_k = jnp.stack(ks, axis=1)
    new_cache_v = jnp.stack(vs, axis=1)
    new_state = jnp.stack(ss, axis=1)

    past = cache_k.shape[2]
    ck = cache_k.reshape(dbatch, depth, past, KV_WIDTH)
    cv = cache_v.reshape(dbatch, depth, past, KV_WIDTH)
    xs = x_sample.reshape(dbatch * dseq, d)
    for l in range(depth):
        xs, _, _, _ = _trunk_layer(xs, l, p, batch=dbatch, seq=dseq, mod=mod, row0=0,
                                   ctx=(ck, cv, state_hgrn), final=(l == depth - 1))
    y_sample = xs.reshape(dbatch, dseq, d)
    return (y_prompt, y_sample, new_cache_k, new_cache_v, new_state)
```

```python
import functools
import math

import numpy as np
import jax
import jax.numpy as jnp
from jax import lax
from jax.experimental import pallas as pl
from jax.experimental.pallas import tpu as pltpu

F32 = jnp.float32
BF16 = jnp.bfloat16

VMEM_LIMIT_BYTES = 56 * 1024 * 1024
LANES = 128

EPS = 1e-6
NEG_BIG = -1e30
LB_FLOOR = 1e-30
GRID_W = 64
HY_WIDTH = 1024
HY_BANDS = 16
N_HEADS = 8
N_KV_HEADS = 2
GROUP = N_HEADS // N_KV_HEADS
HEAD_DIM = 128
ATTN_WIDTH = N_HEADS * HEAD_DIM
KV_WIDTH = N_KV_HEADS * HEAD_DIM
WINDOW = 128
ROPE_BASE = 10000.0
ATTN_SCALE = HEAD_DIM ** -0.5
HG_HEADS = 8
HG_DK = 128
HG_DV = 128
HG_WIDTH = HG_HEADS * HG_DK
HG_CHUNK = 128
N_EXPERTS = 8
TOP_K = 2

OFF_HY = 0
OFF_AQ = 3 * HY_WIDTH
OFF_AK = OFF_AQ + ATTN_WIDTH
OFF_AV = OFF_AK + KV_WIDTH
OFF_HQ = OFF_AV + KV_WIDTH
OFF_FF = OFF_HQ + HG_WIDTH
OFF_FB = OFF_FF + HG_WIDTH
OFF_HI = OFF_FB + HG_WIDTH
OFF_HG = OFF_HI + HG_WIDTH
OFF_MA = OFF_HG + HG_WIDTH


def _cparams(*sem):
    return pltpu.CompilerParams(dimension_semantics=sem, vmem_limit_bytes=VMEM_LIMIT_BYTES)


def _split3(x):
    hi = x.astype(BF16)
    r1 = x - hi.astype(F32)
    mid = r1.astype(BF16)
    lo = (r1 - mid.astype(F32)).astype(BF16)
    return hi, mid, lo


def _dot(a, b):
    return jnp.dot(a, b, preferred_element_type=F32)


def _dot_f32(a, b):
    a0, a1, a2 = _split3(a)
    b0, b1, b2 = _split3(b)
    return (_dot(a0, b0) + (_dot(a0, b1) + _dot(a1, b0))
            + (_dot(a0, b2) + _dot(a1, b1) + _dot(a2, b0)))


def _dot_f32_3pass(a, b):
    a0, a1, _ = _split3(a)
    b0, b1, _ = _split3(b)
    return _dot(a0, b0) + (_dot(a0, b1) + _dot(a1, b0))


def _silu(x):
    return x * jax.nn.sigmoid(x)


def _ada_kernel(c_ref, w_ref, b_ref, o_ref):
    o_ref[...] = _dot_f32(_silu(c_ref[...]), w_ref[...]) + b_ref[...]


def ada_modulation(cond, ada_w, ada_b, tn=512):
    depth, d, n = ada_w.shape
    rows = cond.shape[0]
    return pl.pallas_call(
        _ada_kernel,
        grid=(depth, n // tn),
        in_specs=[pl.BlockSpec((rows, d), lambda l, j: (0, 0)),
                  pl.BlockSpec((None, d, tn), lambda l, j: (l, 0, j)),
                  pl.BlockSpec((None, 1, tn), lambda l, j: (l, 0, j))],
        out_specs=pl.BlockSpec((None, rows, tn), lambda l, j: (l, 0, j)),
        out_shape=jax.ShapeDtypeStruct((depth, rows, n), F32),
        compiler_params=_cparams("parallel", "parallel"),
        name="ada_modulation",
    )(cond, ada_w, ada_b.reshape(depth, 1, n))


def _norm_modulate(x, nw, sc, sh):
    ms = jnp.mean(x * x, axis=-1, keepdims=True)
    y = x * lax.rsqrt(ms + EPS) * nw
    return y * (1.0 + sc) + sh


def _mod_index(layer, j, tiles_per_row, row0):
    def index(i, *_):
        return (layer, row0 + i // tiles_per_row, j, 0, 0)
    return index


PROJ_TN = 512
F32_TILES = (OFF_AK // PROJ_TN,) + tuple(range(OFF_FF // PROJ_TN, OFF_HI // PROJ_TN))
OFF32_KV = 0
OFF32_FF = PROJ_TN
OFF32_FB = OFF32_FF + HG_WIDTH
assert OFF_AK % PROJ_TN == 0 and 2 * KV_WIDTH == PROJ_TN and OFF_FF % PROJ_TN == 0 and OFF_HI % PROJ_TN == 0


def _f32_tile_slot(j):
    return jnp.maximum(sum((j >= tile).astype(jnp.int32) for tile in F32_TILES) - 1, 0)


def _ln_mm_kernel(x_ref, nw_ref, sc_ref, sh_ref, w_ref, o_ref, o32_ref, h_ref):
    j = pl.program_id(1)

    @pl.when(j == 0)
    def _():
        h_ref[...] = _norm_modulate(x_ref[...], nw_ref[...], sc_ref[...], sh_ref[...]).astype(BF16)

    acc = _dot(h_ref[...], w_ref[...])
    o_ref[...] = acc.astype(o_ref.dtype)
    keep = functools.reduce(jnp.logical_or, [j == tile for tile in F32_TILES])

    @pl.when(keep)
    def _():
        o32_ref[...] = acc


def ln_mod_matmul(x, nw, mod, mod_idx, w, *, layer, tiles_per_row, row0, tm):
    t, d = x.shape
    n = w.shape[1]
    tn = PROJ_TN
    sc_j, sh_j = mod_idx
    mspec = lambda j: pl.BlockSpec((None, None, None, 1, d), _mod_index(layer, j, tiles_per_row, row0))
    return pl.pallas_call(
        _ln_mm_kernel,
        grid=(t // tm, n // tn),
        in_specs=[pl.BlockSpec((tm, d), lambda i, j: (i, 0)),
                  pl.BlockSpec((1, d), lambda i, j: (0, 0)),
                  mspec(sc_j), mspec(sh_j),
                  pl.BlockSpec((d, tn), lambda i, j: (0, j))],
        out_specs=[pl.BlockSpec((tm, tn), lambda i, j: (i, j)),
                   pl.BlockSpec((tm, tn), lambda i, j: (i, _f32_tile_slot(j)))],
        out_shape=[jax.ShapeDtypeStruct((t, n), BF16),
                   jax.ShapeDtypeStruct((t, len(F32_TILES) * tn), F32)],
        scratch_shapes=[pltpu.VMEM((tm, d), BF16)],
        compiler_params=_cparams("parallel", "arbitrary"),
        name="ln_mod_matmul",
    )(x, nw.reshape(1, d), mod, mod, w)


def _ln_glu_kernel(x_ref, nw_ref, sc_ref, sh_ref, w1_ref, w3_ref, o_ref, h_ref):
    @pl.when(pl.program_id(1) == 0)
    def _():
        h_ref[...] = _norm_modulate(x_ref[...], nw_ref[...], sc_ref[...], sh_ref[...]).astype(BF16)

    h = h_ref[...]
    o_ref[...] = (_silu(_dot(h, w1_ref[...])) * _dot(h, w3_ref[...])).astype(o_ref.dtype)


def ln_mod_glu(x, nw, mod, mod_idx, w1, w3, *, layer, tiles_per_row, row0, tm, tn):
    t, d = x.shape
    n = w1.shape[1]
    sc_j, sh_j = mod_idx
    mspec = lambda j: pl.BlockSpec((None, None, None, 1, d), _mod_index(layer, j, tiles_per_row, row0))
    return pl.pallas_call(
        _ln_glu_kernel,
        grid=(t // tm, n // tn),
        in_specs=[pl.BlockSpec((tm, d), lambda i, j: (i, 0)),
                  pl.BlockSpec((1, d), lambda i, j: (0, 0)),
                  mspec(sc_j), mspec(sh_j),
                  pl.BlockSpec((d, tn), lambda i, j: (0, j)),
                  pl.BlockSpec((d, tn), lambda i, j: (0, j))],
        out_specs=pl.BlockSpec((tm, tn), lambda i, j: (i, j)),
        out_shape=jax.ShapeDtypeStruct((t, n), BF16),
        scratch_shapes=[pltpu.VMEM((tm, d), BF16)],
        compiler_params=_cparams("parallel", "arbitrary"),
        name="ln_mod_glu",
    )(x, nw.reshape(1, d), mod, mod, w1, w3)


def _merge_kernel(ya_ref, yb_ref, yc_ref, ma_ref, mb_ref, mc_ref, wa_ref, wb_ref, wc_ref, o_ref):
    gate = lambda m_ref: jax.nn.sigmoid(m_ref[...].astype(F32))
    acc = gate(ma_ref) * _dot(ya_ref[...], wa_ref[...])
    acc = acc + gate(mb_ref) * _dot(yb_ref[...], wb_ref[...])
    acc = acc + gate(mc_ref) * _dot(yc_ref[...], wc_ref[...])
    o_ref[...] = acc.astype(o_ref.dtype)


def branch_merge(ya, yb, yc, proj, wa, wb, wc, *, tm, tn):
    t, k = ya.shape
    n = wa.shape[1]
    gate = lambda off: pl.BlockSpec((tm, tn), lambda i, j: (i, off // tn + j))
    yspec = pl.BlockSpec((tm, k), lambda i, j: (i, 0))
    wspec = pl.BlockSpec((k, tn), lambda i, j: (0, j))
    return pl.pallas_call(
        _merge_kernel,
        grid=(t // tm, n // tn),
        in_specs=[yspec, yspec, yspec, gate(OFF_MA), gate(OFF_MA + n), gate(OFF_MA + 2 * n),
                  wspec, wspec, wspec],
        out_specs=pl.BlockSpec((tm, tn), lambda i, j: (i, j)),
        out_shape=jax.ShapeDtypeStruct((t, n), BF16),
        compiler_params=_cparams("parallel", "parallel"),
        name="branch_merge",
    )(ya, yb, yc, proj, proj, proj, wa, wb, wc)


def _mm_resid_kernel(a_ref, w_ref, x_ref, g_ref, o_ref):
    o_ref[...] = x_ref[...] + g_ref[...] * _dot(a_ref[...], w_ref[...])


def matmul_gated_residual(a, w, x, mod, g_j, *, layer, tiles_per_row, row0, tm, tn):
    t, k = a.shape
    n = w.shape[1]
    return pl.pallas_call(
        _mm_resid_kernel,
        grid=(t // tm, n // tn),
        in_specs=[pl.BlockSpec((tm, k), lambda i, j: (i, 0)),
                  pl.BlockSpec((k, tn), lambda i, j: (0, j)),
                  pl.BlockSpec((tm, tn), lambda i, j: (i, j)),
                  pl.BlockSpec((None, None, None, 1, tn),
                               lambda i, j: (layer, row0 + i // tiles_per_row, g_j, 0, j))],
        out_specs=pl.BlockSpec((tm, tn), lambda i, j: (i, j)),
        out_shape=jax.ShapeDtypeStruct((t, n), F32),
        compiler_params=_cparams("parallel", "parallel"),
        name="matmul_gated_residual",
    )(a, w, x, mod)


def _final_norm_kernel(x_ref, fw_ref, o_ref):
    x = x_ref[...]
    o_ref[...] = x * lax.rsqrt(jnp.mean(x * x, axis=-1, keepdims=True) + EPS) * fw_ref[...]


def final_norm(x, fw, *, tm):
    t, d = x.shape
    return pl.pallas_call(
        _final_norm_kernel,
        grid=(t // tm,),
        in_specs=[pl.BlockSpec((tm, d), lambda i: (i, 0)), pl.BlockSpec((1, d), lambda i: (0, 0))],
        out_specs=pl.BlockSpec((tm, d), lambda i: (i, 0)),
        out_shape=jax.ShapeDtypeStruct((t, d), F32),
        compiler_params=_cparams("parallel"),
        name="final_norm",
    )(x, fw.reshape(1, d))


MOE_ROW_TILE = 512
MOE_TOKEN_TILE = 256


def _router_kernel(x_ref, nw_ref, sc_ref, sh_ref, rw_ref, h_ref, tw_ref, route_ref, cnt_ref, run_ref):
    @pl.when(pl.program_id(0) == 0)
    def _():
        run_ref[...] = jnp.zeros_like(run_ref)

    h = _norm_modulate(x_ref[...], nw_ref[...], sc_ref[...], sh_ref[...])
    h_ref[...] = h
    tm = h.shape[0]
    lane = lax.broadcasted_iota(jnp.int32, (tm, LANES), 1)
    logits = jnp.where(lane < N_EXPERTS, _dot_f32(h, rw_ref[...]), -jnp.inf)
    m1 = jnp.max(logits, axis=-1, keepdims=True)
    i1 = jnp.min(jnp.where(logits == m1, lane, LANES), axis=-1, keepdims=True)
    rest = jnp.where(lane == i1, -jnp.inf, logits)
    m2 = jnp.max(rest, axis=-1, keepdims=True)
    i2 = jnp.min(jnp.where(rest == m2, lane, LANES), axis=-1, keepdims=True)
    e2 = jnp.exp(m2 - m1)
    inv = 1.0 / (1.0 + e2)
    tw_ref[...] = jnp.where(lane == 0, inv, jnp.where(lane == 1, e2 * inv, 0.0))

    sel = jnp.where((lane == i1) | (lane == i2), 1.0, 0.0)
    before = lax.broadcasted_iota(jnp.int32, (tm, tm), 1) < lax.broadcasted_iota(jnp.int32, (tm, tm), 0)
    rank = _dot(jnp.where(before, 1.0, 0.0).astype(BF16), sel.astype(BF16)) + run_ref[0:1, :]
    r1 = jnp.sum(jnp.where(lane == i1, rank, 0.0), axis=-1, keepdims=True).astype(jnp.int32)
    r2 = jnp.sum(jnp.where(lane == i2, rank, 0.0), axis=-1, keepdims=True).astype(jnp.int32)
    route_ref[...] = jnp.where(lane == 0, i1, jnp.where(lane == 1, i2, jnp.where(lane == 2, r1, r2)))
    run_ref[...] = run_ref[...] + jnp.sum(sel, axis=0, keepdims=True)
    cnt_ref[...] = run_ref[...]


def router(x, nw, mod, mod_idx, rw, *, layer, tiles_per_row, row0, tm):
    t, d = x.shape
    sc_j, sh_j = mod_idx
    rw_pad = jnp.zeros((d, LANES), F32).at[:, :N_EXPERTS].set(rw)
    mspec = lambda j: pl.BlockSpec((None, None, None, 1, d), _mod_index(layer, j, tiles_per_row, row0))
    return pl.pallas_call(
        _router_kernel,
        grid=(t // tm,),
        in_specs=[pl.BlockSpec((tm, d), lambda i: (i, 0)),
                  pl.BlockSpec((1, d), lambda i: (0, 0)),
                  mspec(sc_j), mspec(sh_j),
                  pl.BlockSpec((d, LANES), lambda i: (0, 0))],
        out_specs=[pl.BlockSpec((tm, d), lambda i: (i, 0)),
                   pl.BlockSpec((tm, LANES), lambda i: (i, 0)),
                   pl.BlockSpec((tm, LANES), lambda i: (i, 0)),
                   pl.BlockSpec((8, LANES), lambda i: (0, 0))],
        out_shape=[jax.ShapeDtypeStruct((t, d), F32), jax.ShapeDtypeStruct((t, LANES), F32),
                   jax.ShapeDtypeStruct((t, LANES), jnp.int32), jax.ShapeDtypeStruct((8, LANES), F32)],
        scratch_shapes=[pltpu.VMEM((8, LANES), F32)],
        compiler_params=_cparams("arbitrary"),
        name="router",
    )(x, nw.reshape(1, d), mod, mod, rw_pad)


def _row_copy(src, src_row, dst, dst_row, sem):
    return pltpu.make_async_copy(src.at[pl.ds(src_row, 1), :], dst.at[pl.ds(dst_row, 1), :], sem)


def _dispatch_kernel(pos_ref, h_ref, xs_in_ref, xs_ref, sem):
    del xs_in_ref
    nt = h_ref.shape[0]

    def start(r, carry):
        for k in range(TOP_K):
            _row_copy(h_ref, r, xs_ref, pos_ref[0, k * nt + r], sem).start()
        return carry

    def wait(r, carry):
        for k in range(TOP_K):
            _row_copy(h_ref, r, xs_ref, pos_ref[0, k * nt + r], sem).wait()
        return carry

    lax.fori_loop(0, nt, start, 0, unroll=8)
    lax.fori_loop(0, nt, wait, 0, unroll=8)


def moe_dispatch(h, pos, rows):
    t, d = h.shape
    nt = MOE_TOKEN_TILE
    return pl.pallas_call(
        _dispatch_kernel,
        grid=(t // nt,),
        in_specs=[pl.BlockSpec((None, 1, TOP_K * nt), lambda i: (i, 0, 0), memory_space=pltpu.SMEM),
                  pl.BlockSpec((nt, d), lambda i: (i, 0)),
                  pl.BlockSpec(memory_space=pl.ANY)],
        out_specs=pl.BlockSpec(memory_space=pl.ANY),
        out_shape=jax.ShapeDtypeStruct((rows, d), F32),
        scratch_shapes=[pltpu.SemaphoreType.DMA(())],
        input_output_aliases={2: 0},
        compiler_params=_cparams("arbitrary"),
        name="moe_dispatch",
    )(pos, h, jnp.zeros((rows, d), F32))


def _gmm_up_kernel(te_ref, nu_ref, xs_ref, w1_ref, w3_ref, o_ref, a_ref):
    del te_ref
    used = pl.program_id(0) < nu_ref[0]

    @pl.when(used)
    def _():
        @pl.when(pl.program_id(1) == 0)
        def _():
            a_ref[...] = xs_ref[...].astype(BF16)

        a = a_ref[...]
        o_ref[...] = (_silu(_dot(a, w1_ref[...])) * _dot(a, w3_ref[...])).astype(o_ref.dtype)

    @pl.when(jnp.logical_not(used))
    def _():
        o_ref[...] = jnp.zeros_like(o_ref)


def gmm_up(xs, w1, w3, tile_expert, n_used, *, tn):
    rows, d = xs.shape
    n = w1.shape[2]
    tm = MOE_ROW_TILE
    wspec = pl.BlockSpec((None, d, tn), lambda i, j, te, nu: (te[i], 0, j))
    return pl.pallas_call(
        _gmm_up_kernel,
        grid_spec=pltpu.PrefetchScalarGridSpec(
            num_scalar_prefetch=2,
            grid=(rows // tm, n // tn),
            in_specs=[pl.BlockSpec((tm, d), lambda i, j, te, nu: (i, 0)), wspec, wspec],
            out_specs=pl.BlockSpec((tm, tn), lambda i, j, te, nu: (i, j)),
            scratch_shapes=[pltpu.VMEM((tm, d), BF16)]),
        out_shape=jax.ShapeDtypeStruct((rows, n), BF16),
        compiler_params=_cparams("parallel", "arbitrary"),
        name="gmm_up",
    )(tile_expert, n_used, xs, w1, w3)


def _gmm_down_kernel(te_ref, nu_ref, f_ref, w_ref, o_ref):
    del te_ref
    used = pl.program_id(0) < nu_ref[0]

    @pl.when(used)
    def _():
        o_ref[...] = _dot(f_ref[...], w_ref[...])

    @pl.when(jnp.logical_not(used))
    def _():
        o_ref[...] = jnp.zeros_like(o_ref)


def gmm_down(f, w2, tile_expert, n_used, *, tn):
    rows, k = f.shape
    n = w2.shape[2]
    tm = MOE_ROW_TILE
    return pl.pallas_call(
        _gmm_down_kernel,
        grid_spec=pltpu.PrefetchScalarGridSpec(
            num_scalar_prefetch=2,
            grid=(rows // tm, n // tn),
            in_specs=[pl.BlockSpec((tm, k), lambda i, j, te, nu: (i, 0)),
                      pl.BlockSpec((None, k, tn), lambda i, j, te, nu: (te[i], 0, j))],
            out_specs=pl.BlockSpec((tm, tn), lambda i, j, te, nu: (i, j))),
        out_shape=jax.ShapeDtypeStruct((rows, n), F32),
        compiler_params=_cparams("parallel", "parallel"),
        name="gmm_down",
    )(tile_expert, n_used, f, w2)


def _combine_kernel(final, pos_ref, x_ref, tw_ref, g_ref, fw_ref, ys_ref, o_ref, y0_ref, y1_ref, sem):
    nt = x_ref.shape[0]
    bufs = (y0_ref, y1_ref)

    def start(r, carry):
        for k in range(TOP_K):
            _row_copy(ys_ref, pos_ref[0, k * nt + r], bufs[k], r, sem).start()
        return carry

    def wait(r, carry):
        for k in range(TOP_K):
            _row_copy(ys_ref, pos_ref[0, k * nt + r], bufs[k], r, sem).wait()
        return carry

    lax.fori_loop(0, nt, start, 0, unroll=8)
    lax.fori_loop(0, nt, wait, 0, unroll=8)
    tw = tw_ref[...]
    x = x_ref[...] + g_ref[...] * (tw[:, 0:1] * y0_ref[...] + tw[:, 1:2] * y1_ref[...])
    if final:
        x = x * lax.rsqrt(jnp.mean(x * x, axis=-1, keepdims=True) + EPS) * fw_ref[...]
    o_ref[...] = x


def moe_combine(x, ys, pos, tw, mod, g_j, fw, *, layer, tiles_per_row, row0, final):
    t, d = x.shape
    nt = MOE_TOKEN_TILE
    return pl.pallas_call(
        functools.partial(_combine_kernel, final),
        grid=(t // nt,),
        in_specs=[pl.BlockSpec((None, 1, TOP_K * nt), lambda i: (i, 0, 0), memory_space=pltpu.SMEM),
                  pl.BlockSpec((nt, d), lambda i: (i, 0)),
                  pl.BlockSpec((nt, LANES), lambda i: (i, 0)),
                  pl.BlockSpec((None, None, None, 1, d), _mod_index(layer, g_j, tiles_per_row, row0)),
                  pl.BlockSpec((1, d), lambda i: (0, 0)),
                  pl.BlockSpec(memory_space=pl.ANY)],
        out_specs=pl.BlockSpec((nt, d), lambda i: (i, 0)),
        out_shape=jax.ShapeDtypeStruct((t, d), F32),
        scratch_shapes=[pltpu.VMEM((nt, d), F32), pltpu.VMEM((nt, d), F32), pltpu.SemaphoreType.DMA(())],
        compiler_params=_cparams("arbitrary"),
        name="moe_combine",
    )(pos, x, tw, mod, fw.reshape(1, d), ys)


def _moe_plan(route, counts, t):
    tm = MOE_ROW_TILE
    nt = MOE_TOKEN_TILE
    n_tiles = TOP_K * t // tm + N_EXPERTS
    cnt = counts[0, :N_EXPERTS].astype(jnp.int32)
    tiles = (cnt + tm - 1) // tm
    ends = jnp.cumsum(tiles)
    offs = (ends - tiles) * tm
    pos = [jnp.take(offs, route[:, k]) + route[:, TOP_K + k] for k in range(TOP_K)]
    pos = jnp.concatenate([p.reshape(t // nt, nt) for p in pos], axis=1).reshape(t // nt, 1, TOP_K * nt)
    tile_expert = jnp.sum(jnp.arange(n_tiles, dtype=jnp.int32)[:, None] >= ends[None, :], axis=1)
    tile_expert = jnp.minimum(tile_expert, N_EXPERTS - 1).astype(jnp.int32)
    return pos, tile_expert, ends[-1:].astype(jnp.int32), n_tiles * tm


def _hyena_filter_kernel(z_ref, w1_ref, b1_ref, fr_ref, w2_ref, b2_ref, w3_ref, t_ref, dec_ref, o_ref):
    fr = fr_ref[...]
    h = jnp.sin(fr * (_dot_f32(z_ref[...], w1_ref[...]) + b1_ref[...]))
    h = jnp.sin(fr * (_dot_f32(h, w2_ref[...]) + b2_ref[...]))
    o_ref[...] = _dot_f32(h, w3_ref[...]) * jnp.exp(-t_ref[...] * jnp.abs(dec_ref[...]))


def hyena_filter(seq, w1, b1, freq, w2, b2, w3, decay, tl=256):
    t = jnp.linspace(0.0, 1.0, seq, dtype=F32)[:, None]
    pos = jnp.arange(seq, dtype=F32)[:, None]
    bands = jnp.linspace(1e-4, HY_BANDS - 1.0, HY_BANDS, dtype=F32)[None, :]
    ang = (2.0 * math.pi / seq) * pos * bands
    z = jnp.concatenate([t, jnp.cos(ang), -jnp.sin(ang)], axis=-1)
    emb = z.shape[1]
    emb_pad = LANES
    z = jnp.pad(z, ((0, 0), (0, emb_pad - emb)))
    w1p = jnp.pad(w1, ((0, emb_pad - emb), (0, 0)))
    hid = w1.shape[1]
    n = w3.shape[1]
    tl = min(tl, seq)
    full = lambda shape: pl.BlockSpec(shape, lambda i: (0, 0))
    return pl.pallas_call(
        _hyena_filter_kernel,
        grid=(seq // tl,),
        in_specs=[pl.BlockSpec((tl, emb_pad), lambda i: (i, 0)),
                  full((emb_pad, hid)), full((1, hid)), full((1, hid)),
                  full((hid, hid)), full((1, hid)), full((hid, n)),
                  pl.BlockSpec((tl, 1), lambda i: (i, 0)), full((1, n))],
        out_specs=pl.BlockSpec((tl, n), lambda i: (i, 0)),
        out_shape=jax.ShapeDtypeStruct((seq, n), F32),
        compiler_params=_cparams("parallel"),
        name="hyena_filter",
    )(z, w1p, b1.reshape(1, hid), freq.reshape(1, hid), w2, b2.reshape(1, hid), w3, t,
      decay.reshape(1, n))


def _dft_tables(seq, kb):
    n = 2 * seq
    k = jnp.arange(seq, dtype=jnp.int32)[:, None]
    s = jnp.arange(seq, dtype=jnp.int32)[None, :]
    ang = ((k * s) % n).astype(F32) * (2.0 * math.pi / n)
    cos = jnp.cos(ang)
    sin = jnp.sin(ang)
    nyq = jnp.where(s % 2 == 0, 1.0, -1.0).astype(F32)
    is0 = k == 0
    f_re = cos
    f_im = jnp.where(is0, nyq, -sin)
    i_re = jnp.where(is0, 1.0 / n, (2.0 / n) * cos)
    i_im = jnp.where(is0, nyq / n, -(2.0 / n) * sin)
    nkb = seq // kb
    fwd = jnp.concatenate([f_re.reshape(nkb, kb, seq), f_im.reshape(nkb, kb, seq)], axis=1)
    inv = jnp.concatenate([i_re.reshape(nkb, kb, seq), i_im.reshape(nkb, kb, seq)], axis=1)
    return fwd, jnp.swapaxes(inv, 1, 2)


def _spectrum_kernel(kb, f_ref, hf_ref, hb_ref, o_ref):
    f = f_ref[...]
    row = lax.broadcasted_iota(jnp.int32, hb_ref.shape, 0)
    hb0 = jnp.where(row == 0, 0.0, hb_ref[...])
    a = _dot_f32_3pass(f, hf_ref[...])
    b = _dot_f32_3pass(f, hb0)
    orow = lax.broadcasted_iota(jnp.int32, a.shape, 0)
    nyq_slot = (orow == kb) & (pl.program_id(0) == 0)
    o_ref[...] = jnp.where((orow < kb) | nyq_slot, a + b, a - b)


def hyena_spectrum(fwd_f32, filt, kb, tc=256):
    nkb, kb2, seq = fwd_f32.shape
    c = filt.shape[1] // 2
    return pl.pallas_call(
        functools.partial(_spectrum_kernel, kb),
        grid=(nkb, c // tc),
        in_specs=[pl.BlockSpec((None, kb2, seq), lambda j, i: (j, 0, 0)),
                  pl.BlockSpec((seq, tc), lambda j, i: (0, i)),
                  pl.BlockSpec((seq, tc), lambda j, i: (0, c // tc + i))],
        out_specs=pl.BlockSpec((None, kb2, tc), lambda j, i: (j, 0, i)),
        out_shape=jax.ShapeDtypeStruct((nkb, kb2, c), F32),
        compiler_params=_cparams("parallel", "parallel"),
        name="hyena_spectrum",
    )(fwd_f32, filt, filt)


def _hyena_conv_kernel(kb, x0_ref, x1_ref, v_ref, cw0_ref, cw1_ref, cwv_ref, cb0_ref, cb1_ref, cbv_ref,
                       bias_ref, f_ref, i_ref, kf_ref, o_ref, u16_ref, acc_ref):
    j = pl.program_id(1)
    b = pl.program_id(2)
    seq = x0_ref.shape[0]

    def conv3(x_ref, w_ref, b_ref):
        x = x_ref[...].astype(F32)
        row = lax.broadcasted_iota(jnp.int32, x.shape, 0)
        prev = jnp.where(row == 0, 0.0, pltpu.roll(x, 1, 0))
        nxt = jnp.where(row == seq - 1, 0.0, pltpu.roll(x, seq - 1, 0))
        w = w_ref[...]
        return prev * w[0:1, :] + x * w[1:2, :] + nxt * w[2:3, :] + b_ref[...]

    def gated_input():
        return conv3(v_ref, cwv_ref, cbv_ref) * conv3(x1_ref, cw1_ref, cb1_ref)

    @pl.when(j == 0)
    def _():
        u16_ref[b] = gated_input().astype(BF16)
        acc_ref[b] = jnp.zeros(acc_ref.shape[1:], F32)

    spec = _dot(f_ref[...], u16_ref[b])
    xr, xi = spec[:kb], spec[kb:]
    kf = kf_ref[...]
    kr, ki = kf[:kb], kf[kb:]
    packed = (lax.broadcasted_iota(jnp.int32, xr.shape, 0) == 0) & (j == 0)
    yr = xr * kr - jnp.where(packed, 0.0, xi * ki)
    yi = jnp.where(packed, xi * ki, xr * ki + xi * kr)
    y = jnp.concatenate([yr, yi], axis=0).astype(BF16)
    acc_ref[b] += _dot(i_ref[...], y)

    @pl.when(j == pl.num_programs(1) - 1)
    def _():
        x0c = conv3(x0_ref, cw0_ref, cb0_ref)
        o_ref[...] = ((acc_ref[b] + gated_input() * bias_ref[...]) * x0c).astype(o_ref.dtype)


def hyena_conv(proj, conv_w, conv_b, bias, fwd, inv, spectrum, *, batch, seq, ct, kb):
    c = HY_WIDTH
    nkb = seq // kb
    ncb = c // ct
    last = nkb - 1
    col = lambda part: pl.BlockSpec((seq, ct), lambda i, j, b: (b, part * ncb + i))
    cw = lambda part: pl.BlockSpec((3, ct), lambda i, j, b: (0, part * ncb + i))
    cb = lambda part: pl.BlockSpec((1, ct), lambda i, j, b: (0, part * ncb + i))
    conv_b = conv_b.reshape(1, 3 * c)
    return pl.pallas_call(
        functools.partial(_hyena_conv_kernel, kb),
        grid=(ncb, nkb, batch),
        in_specs=[col(0), col(1), col(2), cw(0), cw(1), cw(2), cb(0), cb(1), cb(2),
                  pl.BlockSpec((1, ct), lambda i, j, b: (0, i)),
                  pl.BlockSpec((None, 2 * kb, seq), lambda i, j, b: (j, 0, 0)),
                  pl.BlockSpec((None, seq, 2 * kb), lambda i, j, b: (j, 0, 0)),
                  pl.BlockSpec((None, 2 * kb, ct), lambda i, j, b: (j, 0, i))],
        out_specs=pl.BlockSpec((seq, ct), lambda i, j, b: (jnp.where(j == last, b, 0), i)),
        out_shape=jax.ShapeDtypeStruct((batch * seq, c), BF16),
        scratch_shapes=[pltpu.VMEM((batch, seq, ct), BF16), pltpu.VMEM((batch, seq, ct), F32)],
        compiler_params=_cparams("parallel", "arbitrary", "arbitrary"),
        name="hyena_conv",
    )(proj, proj, proj, conv_w, conv_w, conv_w, conv_b, conv_b, conv_b, bias.reshape(1, c),
      fwd, inv, spectrum)


def _rope(x, cos, sin_a, sin_b):
    return x * cos + pltpu.roll(x, HEAD_DIM - HEAD_DIM // 4, 1) * sin_a + pltpu.roll(x, HEAD_DIM // 4, 1) * sin_b


def _dot_nt(a, b):
    return lax.dot_general(a, b, (((1,), (1,)), ((), ())), preferred_element_type=F32)


def _grouped_softmax_pv(qs, keys, values, sinks, valid, o_ref):
    scores = [_dot_nt(q, keys) * ATTN_SCALE for q in qs]
    probs, dens = [], []
    for s, sink in zip(scores, sinks):
        if valid is not None:
            s = jnp.where(valid, s, NEG_BIG)
        m = jnp.maximum(jnp.max(s, axis=-1, keepdims=True), sink)
        p = jnp.exp(s - m)
        dens.append(jnp.sum(p, axis=-1, keepdims=True) + jnp.exp(sink - m))
        probs.append(p.astype(BF16))
    outs = [_dot(p, values) for p in probs]
    for g, (o, den) in enumerate(zip(outs, dens)):
        o_ref[:, g * HEAD_DIM:(g + 1) * HEAD_DIM] = (o / den).astype(o_ref.dtype)


def _ctx_attn_kernel(sink_ref, q_ref, k_ref, v_ref, o_ref):
    kvh = pl.program_id(1)
    qs = [q_ref[:, g * HEAD_DIM:(g + 1) * HEAD_DIM] for g in range(GROUP)]
    sinks = [sink_ref[kvh * GROUP + g] for g in range(GROUP)]
    _grouped_softmax_pv(qs, k_ref[...].astype(BF16), v_ref[...].astype(BF16), sinks, None, o_ref)


def context_attention(proj, proj32, sink, *, batch, seq):
    qw = GROUP * HEAD_DIM
    kv = lambda part: pl.BlockSpec((seq, HEAD_DIM),
                                   lambda b, h: (b, OFF32_KV // HEAD_DIM + part * N_KV_HEADS + h))
    return pl.pallas_call(
        _ctx_attn_kernel,
        grid=(batch, N_KV_HEADS),
        in_specs=[pl.BlockSpec(memory_space=pltpu.SMEM),
                  pl.BlockSpec((seq, qw), lambda b, h: (b, OFF_AQ // qw + h)),
                  kv(0), kv(1)],
        out_specs=pl.BlockSpec((seq, qw), lambda b, h: (b, h)),
        out_shape=jax.ShapeDtypeStruct((batch * seq, ATTN_WIDTH), BF16),
        compiler_params=_cparams("parallel", "parallel"),
        name="context_attention",
    )(sink, proj, proj32, proj32)


def _lat_attn_kernel(seq, sink_ref, q_ref, k_ref, v_ref, ck_ref, cv_ref, cos_ref, sa_ref, sb_ref,
                     o_ref, kr_ref, vb_ref):
    kvh = pl.program_id(1)
    qb = pl.program_id(2)
    blk = q_ref.shape[0]
    nwin = 3 * blk
    past = ck_ref.shape[0]

    @pl.when(qb == 0)
    def _():
        kr_ref[...] = _rope(k_ref[...], cos_ref[...], sa_ref[...], sb_ref[...]).astype(BF16)
        vb_ref[...] = v_ref[...].astype(BF16)

    start = pl.multiple_of(jnp.clip((qb - 1) * blk, 0, seq - nwin), blk)
    keys = jnp.concatenate([kr_ref[pl.ds(start, nwin), :], ck_ref[...].astype(BF16)], axis=0)
    values = jnp.concatenate([vb_ref[pl.ds(start, nwin), :], cv_ref[...].astype(BF16)], axis=0)
    rows = pl.ds(pl.multiple_of(qb * blk, blk), blk)
    cos, sa, sb = cos_ref[rows, :], sa_ref[rows, :], sb_ref[rows, :]
    qpos = qb * blk + lax.broadcasted_iota(jnp.int32, (blk, nwin + past), 0)
    col = lax.broadcasted_iota(jnp.int32, (blk, nwin + past), 1)
    valid = (col >= nwin) | (jnp.abs(qpos - (start + col)) <= WINDOW)
    qs = [_rope(q_ref[:, g * HEAD_DIM:(g + 1) * HEAD_DIM].astype(F32), cos, sa, sb).astype(BF16)
          for g in range(GROUP)]
    sinks = [sink_ref[kvh * GROUP + g] for g in range(GROUP)]
    _grouped_softmax_pv(qs, keys, values, sinks, valid, o_ref)


def _rope_tables(seq):
    rows = seq // GRID_W
    row = jnp.repeat(jnp.arange(rows, dtype=F32), GRID_W)
    col = jnp.tile(jnp.arange(GRID_W, dtype=F32), rows)
    quarter = HEAD_DIM // 4
    inv = ROPE_BASE ** (-jnp.arange(quarter, dtype=F32) / quarter)
    ar = row[:, None] * inv
    ac = col[:, None] * inv
    ang = jnp.concatenate([ar, ar, ac, ac], axis=-1)
    cos, sin = jnp.cos(ang), jnp.sin(ang)
    first = (jnp.arange(HEAD_DIM) % (2 * quarter)) < quarter
    return cos, jnp.where(first, -sin, 0.0), jnp.where(first, 0.0, sin)


def latent_attention(proj, proj32, cache_k, cache_v, sink, *, layer, batch, seq, blk=128):
    qw = GROUP * HEAD_DIM
    past = cache_k.shape[2]
    nqb = seq // blk
    cos, sa, sb = _rope_tables(seq)
    table = pl.BlockSpec((seq, HEAD_DIM), lambda b, h, i: (0, 0))
    cache = pl.BlockSpec((None, None, past, HEAD_DIM), lambda b, h, i: (b, layer, 0, h))
    kv = lambda part: pl.BlockSpec((seq, HEAD_DIM),
                                   lambda b, h, i: (b, OFF32_KV // HEAD_DIM + part * N_KV_HEADS + h))
    return pl.pallas_call(
        functools.partial(_lat_attn_kernel, seq),
        grid=(batch, N_KV_HEADS, nqb),
        in_specs=[pl.BlockSpec(memory_space=pltpu.SMEM),
                  pl.BlockSpec((blk, qw), lambda b, h, i: (b * nqb + i, OFF_AQ // qw + h)),
                  kv(0), kv(1),
                  cache, cache, table, table, table],
        out_specs=pl.BlockSpec((blk, qw), lambda b, h, i: (b * nqb + i, h)),
        out_shape=jax.ShapeDtypeStruct((batch * seq, ATTN_WIDTH), BF16),
        scratch_shapes=[pltpu.VMEM((seq, HEAD_DIM), BF16), pltpu.VMEM((seq, HEAD_DIM), BF16)],
        compiler_params=_cparams("parallel", "parallel", "arbitrary"),
        name="latent_attention",
    )(sink, proj, proj32, proj32, cache_k, cache_v, cos, sa, sb)


_HG_LEVELS = tuple(HG_CHUNK >> (i + 1) for i in range(int(math.log2(HG_CHUNK))))


def _hgrn_tables():
    c = HG_CHUNK
    t = np.arange(c)[:, None]
    u = np.arange(c)[None, :]
    blocks = [(u <= t), (u > t)]
    for m in _HG_LEVELS:
        ref = (t // (2 * m)) * (2 * m) + m - 1
        second = (t % (2 * m)) >= m
        blocks.append(np.where(second, (u > ref) & (u <= t), (u > t) & (u <= ref)))
    fwd = np.concatenate(blocks, axis=0).astype(np.float32)
    bwd = np.concatenate([b[::-1, ::-1] for b in blocks], axis=0).astype(np.float32)
    s = u
    level = np.full((c, c), len(_HG_LEVELS) + 1, np.int32)
    level[t == s] = len(_HG_LEVELS)
    for i, m in enumerate(_HG_LEVELS):
        hit = (t // (2 * m) == s // (2 * m)) & ((t % (2 * m)) >= m) & ((s % (2 * m)) < m)
        level[hit] = i
    a = np.stack([fwd, bwd])
    a = np.concatenate([a, a], axis=2)
    lv = np.stack([level, level.T])
    return jnp.asarray(a, BF16), jnp.asarray(lv, jnp.int32)


def _hgrn_kernel(layer, has_s0, seq, heads, *refs):
    if has_s0:
        (q_ref, ff_ref, fb_ref, i_ref, g_ref, lb_ref, nw_ref, a_ref, lv_ref, s0_ref,
         y_ref, sfin_ref, of_ref, ob_ref, st_ref) = refs
    else:
        (q_ref, ff_ref, fb_ref, i_ref, g_ref, lb_ref, nw_ref, a_ref, lv_ref,
         y_ref, sfin_ref, of_ref, ob_ref, st_ref) = refs
    c = HG_CHUNK
    nlev = len(_HG_LEVELS)
    nc = seq // c

    lbs = lb_ref[...]
    mx = jnp.max(lbs, axis=0, keepdims=True)
    ex = jnp.exp(lbs - mx)
    sm = ex / jnp.sum(ex, axis=0, keepdims=True)
    lb = jnp.zeros(sm.shape[1:], F32)
    for j in range(1, layer + 1):
        lb = lb + sm[j]

    for d in range(2):
        for h in range(heads):
            if has_s0:
                st_ref[d, h] = s0_ref[d, h].T
            else:
                st_ref[d, h] = jnp.zeros((HG_DV, HG_DK), F32)

    def decay_exponents(d, h, ci):
        rows = pl.ds(pl.multiple_of(ci * c, c), c)
        cols = slice(h * HG_DK, (h + 1) * HG_DK)
        q = _silu(q_ref[rows, cols].astype(F32)) * (HG_DK ** -0.5)
        fpre = (ff_ref if d == 0 else fb_ref)[rows, cols]
        lbd = lb[d:d + 1, cols]
        f = jnp.maximum(lbd, LB_FLOOR) + (1.0 - lbd) * jax.nn.sigmoid(fpre)
        k = 1.0 - f
        v = i_ref[rows, cols].astype(BF16)
        args = _dot(a_ref[d], jnp.concatenate(_split3(jnp.log(f))[:2], axis=0))
        return dict(d=d, h=h, rows=rows, cols=cols, q=q, k=k, v=v, args=args)

    def level_products(s):
        q, k = s['q'], s['k']
        e = jnp.exp(s['args'])
        s['q_in'] = (q * e[0:c]).astype(BF16)
        s['k_out'] = (k * e[c:2 * c]).astype(BF16)
        last = (c - 1) if s['d'] == 0 else 0
        s['total'] = e[last:last + 1]
        prods = [_dot_nt((q * e[(2 + i) * c:(3 + i) * c]).astype(BF16), (k * e[(2 + i) * c:(3 + i) * c]).astype(BF16))
                 for i in range(nlev)]
        s['prods'] = prods + [_dot_nt(q.astype(BF16), k.astype(BF16))]
        return s

    def outputs_and_state(s, o_ref):
        d, h = s['d'], s['h']
        lv = lv_ref[d]
        att = jnp.where(lv == 0, s['prods'][0], 0.0)
        for i in range(1, nlev + 1):
            att = att + jnp.where(lv == i, s['prods'][i], 0.0)
        st = st_ref[d, h]
        o_ref[s['rows'], s['cols']] = _dot_nt(s['q_in'], st.astype(BF16)) + _dot(att.astype(BF16), s['v'])
        upd = lax.dot_general(s['v'], s['k_out'], (((0,), (0,)), ((), ())), preferred_element_type=F32)
        st_ref[d, h] = st * s['total'] + upd

    def body(ci, carry):
        chains = [(d, h) for h in range(heads) for d in range(2)]
        stage = [decay_exponents(d, h, ci if d == 0 else nc - 1 - ci) for d, h in chains]
        stage = [level_products(s) for s in stage]
        for s in stage:
            outputs_and_state(s, of_ref if s['d'] == 0 else ob_ref)
        return carry

    lax.fori_loop(0, nc, body, 0)

    for h in range(heads):
        cols = slice(h * HG_DK, (h + 1) * HG_DK)
        o = of_ref[:, cols] + ob_ref[:, cols]
        o = o * lax.rsqrt(jnp.mean(o * o, axis=-1, keepdims=True) + EPS) * nw_ref[...] * _silu(g_ref[:, cols].astype(F32))
        y_ref[:, cols] = o.astype(y_ref.dtype)
        for d in range(2):
            sfin_ref[d, h] = st_ref[d, h].T


HG_HEADS_PER_STEP = 2


def hgrn2_mix(proj, proj32, hg_lb, norm_w, s0, *, layer, batch, seq):
    a_tab, lv_tab = _hgrn_tables()
    depth = hg_lb.shape[0]
    hps = HG_HEADS_PER_STEP
    w = hps * HG_DK
    col = lambda off: pl.BlockSpec((seq, w), lambda b, h: (b, off // w + h))
    state = pl.BlockSpec((None, 2, hps, HG_DK, HG_DV), lambda b, h: (b, 0, h, 0, 0))
    in_specs = [col(OFF_HQ), col(OFF32_FF), col(OFF32_FB), col(OFF_HI), col(OFF_HG),
                pl.BlockSpec((depth, 2, w), lambda b, h: (0, 0, h)),
                pl.BlockSpec((1, HG_DV), lambda b, h: (0, 0)),
                pl.BlockSpec(a_tab.shape, lambda b, h: (0, 0, 0)),
                pl.BlockSpec(lv_tab.shape, lambda b, h: (0, 0, 0))]
    args = [proj, proj32, proj32, proj, proj, hg_lb, norm_w.reshape(1, HG_DV), a_tab, lv_tab]
    if s0 is not None:
        in_specs.append(state)
        args.append(s0)
    return pl.pallas_call(
        functools.partial(_hgrn_kernel, layer, s0 is not None, seq, hps),
        grid=(batch, HG_HEADS // hps),
        in_specs=in_specs,
        out_specs=[pl.BlockSpec((seq, w), lambda b, h: (b, h)), state],
        out_shape=[jax.ShapeDtypeStruct((batch * seq, HG_WIDTH), BF16),
                   jax.ShapeDtypeStruct((batch, 2, HG_HEADS, HG_DK, HG_DV), F32)],
        scratch_shapes=[pltpu.VMEM((seq, w), F32), pltpu.VMEM((seq, w), F32),
                        pltpu.VMEM((2, hps, HG_DV, HG_DK), F32)],
        compiler_params=_cparams("parallel", "parallel"),
        name="hgrn2_mix",
    )(*args)


def _trunk_layer(x, l, p, *, batch, seq, mod, row0, ctx, final):
    t, d = x.shape
    tm = 1024
    per_row = (t if ctx is None else seq) // tm
    geo = dict(layer=l, tiles_per_row=per_row, row0=row0)

    proj, proj32 = ln_mod_matmul(x, p['norm1_w'][l], mod, (1, 0), p['w_in'][l], tm=tm, **geo)

    filt = hyena_filter(seq, p['hy_w1'][l], p['hy_b1'][l], p['hy_freq'][l], p['hy_w2'][l], p['hy_b2'][l],
                        p['hy_w3'][l], p['hy_decay'][l])
    kb = min(seq, 512)
    fwd, inv = _dft_tables(seq, kb)
    spectrum = hyena_spectrum(fwd, filt, kb)
    ya = hyena_conv(proj, p['hy_conv_w'][l], p['hy_conv_b'][l], p['hy_bias'][l], fwd.astype(BF16),
                    inv.astype(BF16), spectrum, batch=batch, seq=seq, ct=256, kb=kb)

    if ctx is None:
        yb = context_attention(proj, proj32, p['attn_sink'][l], batch=batch, seq=seq)
        s0 = None
    else:
        cache_k, cache_v, s0 = ctx
        yb = latent_attention(proj, proj32, cache_k, cache_v, p['attn_sink'][l], layer=l, batch=batch,
                              seq=seq)
        s0 = s0[:, l]

    yc, s_fin = hgrn2_mix(proj, proj32, p['hg_lb'], p['hg_norm_w'][l], s0, layer=l, batch=batch, seq=seq)

    mixed = branch_merge(ya, yb, yc, proj, p['w_branch_a'][l], p['w_branch_b'][l], p['w_branch_c'][l],
                         tm=tm, tn=512)
    x = matmul_gated_residual(mixed, p['w_out'][l], x, mod, 2, tm=tm, tn=512, **geo)

    j = l // 2
    if l % 2 == 0:
        f = ln_mod_glu(x, p['norm2_w'][l], mod, (4, 3), p['ffn_w1'][j], p['ffn_w3'][j], tm=tm, tn=512, **geo)
        x = matmul_gated_residual(f, p['ffn_w2'][j], x, mod, 5, tm=tm, tn=512, **geo)
        if final:
            x = final_norm(x, p['final_norm_w'], tm=512)
    else:
        h2, tw, route, counts = router(x, p['norm2_w'][l], mod, (4, 3), p['router_w'][j], tm=512,
                                       layer=l, tiles_per_row=per_row * 2, row0=row0)
        pos, tile_expert, n_used, rows = _moe_plan(route, counts, t)
        xs = moe_dispatch(h2, pos, rows)
        f = gmm_up(xs, p['moe_w1'][j], p['moe_w3'][j], tile_expert, n_used, tn=512)
        ys = gmm_down(f, p['moe_w2'][j], tile_expert, n_used, tn=512)
        x = moe_combine(x, ys, pos, tw, mod, 5, p['final_norm_w'], final=final, layer=l,
                        tiles_per_row=per_row * (tm // MOE_TOKEN_TILE), row0=row0)
    k = proj32[:, OFF32_KV:OFF32_KV + KV_WIDTH]
    v = proj32[:, OFF32_KV + KV_WIDTH:OFF32_KV + 2 * KV_WIDTH]
    return x, k, v, s_fin


def kernel(x_prompt, x_sample, c, cache_k, cache_v, state_hgrn, c_ctx, ada_w, ada_b, norm1_w, norm2_w, w_in,
           hy_conv_w, hy_conv_b, hy_w1, hy_b1, hy_freq, hy_w2, hy_b2, hy_w3, hy_decay, hy_bias, attn_sink,
           hg_lb, hg_norm_w, w_branch_a, w_branch_b, w_branch_c, w_out, ffn_w1, ffn_w3, ffn_w2, router_w,
           moe_w1, moe_w3, moe_w2, final_norm_w):
    batch, seq, d = x_prompt.shape
    dbatch, dseq, _ = x_sample.shape
    depth = ada_w.shape[0]
    bf = lambda a: a.astype(BF16)
    p = dict(norm1_w=norm1_w, norm2_w=norm2_w, w_in=bf(w_in), hy_conv_w=hy_conv_w, hy_conv_b=hy_conv_b,
             hy_w1=hy_w1, hy_b1=hy_b1, hy_freq=hy_freq, hy_w2=hy_w2, hy_b2=hy_b2, hy_w3=hy_w3,
             hy_decay=hy_decay, hy_bias=hy_bias, attn_sink=attn_sink, hg_lb=hg_lb, hg_norm_w=hg_norm_w,
             w_branch_a=bf(w_branch_a), w_branch_b=bf(w_branch_b), w_branch_c=bf(w_branch_c),
             w_out=bf(w_out), ffn_w1=bf(ffn_w1), ffn_w3=bf(ffn_w3), ffn_w2=bf(ffn_w2), router_w=router_w,
             moe_w1=bf(moe_w1), moe_w3=bf(moe_w3), moe_w2=bf(moe_w2), final_norm_w=final_norm_w)

    nrows = 16
    cond = jnp.zeros((nrows, d), F32).at[:dbatch].set(c).at[dbatch].set(c_ctx)
    mod = ada_modulation(cond, ada_w, ada_b).reshape(depth, nrows, 6, 1, d)

    xp = x_prompt.reshape(batch * seq, d)
    ks, vs, ss = [], [], []
    for l in range(depth):
        xp, k_l, v_l, s_l = _trunk_layer(xp, l, p, batch=batch, seq=seq, mod=mod, row0=dbatch, ctx=None,
                                         final=(l == depth - 1))
        ks.append(k_l.reshape(batch, seq, N_KV_HEADS, HEAD_DIM))
        vs.append(v_l.reshape(batch, seq, N_KV_HEADS, HEAD_DIM))
        ss.append(s_l)
    y_prompt = xp.reshape(batch, seq, d)
    new_cache_k = jnp.stack(ks, axis=1)
    new_cache_v = jnp.stack(vs, axis=1)
    new_state = jnp.stack(ss, axis=1)

    past = cache_k.shape[2]
    ck = cache_k.reshape(dbatch, depth, past, KV_WIDTH)
    cv = cache_v.reshape(dbatch, depth, past, KV_WIDTH)
    xs = x_sample.reshape(dbatch * dseq, d)
    for l in range(depth):
        xs, _, _, _ = _trunk_layer(xs, l, p, batch=dbatch, seq=dseq, mod=mod, row0=0,
                                   ctx=(ck, cv, state_hgrn), final=(l == depth - 1))
    y_sample = xs.reshape(dbatch, dseq, d)
    return (y_prompt, y_sample, new_cache_k, new_cache_v, new_state)
```

```python
import functools
import math

import numpy as np
import jax
import jax.numpy as jnp
from jax import lax
from jax.experimental import pallas as pl
from jax.experimental.pallas import tpu as pltpu

F32 = jnp.float32
BF16 = jnp.bfloat16

VMEM_LIMIT_BYTES = 56 * 1024 * 1024
LANES = 128

EPS = 1e-6
NEG_BIG = -1e30
LB_FLOOR = 1e-30
GRID_W = 64
HY_WIDTH = 1024
HY_BANDS = 16
N_HEADS = 8
N_KV_HEADS = 2
GROUP = N_HEADS // N_KV_HEADS
HEAD_DIM = 128
ATTN_WIDTH = N_HEADS * HEAD_DIM
KV_WIDTH = N_KV_HEADS * HEAD_DIM
WINDOW = 128
ROPE_BASE = 10000.0
ATTN_SCALE = HEAD_DIM ** -0.5
HG_HEADS = 8
HG_DK = 128
HG_DV = 128
HG_WIDTH = HG_HEADS * HG_DK
HG_CHUNK = 128
N_EXPERTS = 8
TOP_K = 2

OFF_HY = 0
OFF_AQ = 3 * HY_WIDTH
OFF_AK = OFF_AQ + ATTN_WIDTH
OFF_AV = OFF_AK + KV_WIDTH
OFF_HQ = OFF_AV + KV_WIDTH
OFF_FF = OFF_HQ + HG_WIDTH
OFF_FB = OFF_FF + HG_WIDTH
OFF_HI = OFF_FB + HG_WIDTH
OFF_HG = OFF_HI + HG_WIDTH
OFF_MA = OFF_HG + HG_WIDTH


def _cparams(*sem):
    return pltpu.CompilerParams(dimension_semantics=sem, vmem_limit_bytes=VMEM_LIMIT_BYTES)


def _split3(x):
    hi = x.astype(BF16)
    r1 = x - hi.astype(F32)
    mid = r1.astype(BF16)
    lo = (r1 - mid.astype(F32)).astype(BF16)
    return hi, mid, lo


def _dot(a, b):
    return jnp.dot(a, b, preferred_element_type=F32)


def _dot_f32(a, b):
    a0, a1, a2 = _split3(a)
    b0, b1, b2 = _split3(b)
    return (_dot(a0, b0) + (_dot(a0, b1) + _dot(a1, b0))
            + (_dot(a0, b2) + _dot(a1, b1) + _dot(a2, b0)))


def _dot_f32_3pass(a, b):
    a0, a1, _ = _split3(a)
    b0, b1, _ = _split3(b)
    return _dot(a0, b0) + (_dot(a0, b1) + _dot(a1, b0))


def _silu(x):
    return x * jax.nn.sigmoid(x)


def _ada_kernel(c_ref, w_ref, b_ref, o_ref):
    o_ref[...] = _dot_f32(_silu(c_ref[...]), w_ref[...]) + b_ref[...]


def ada_modulation(cond, ada_w, ada_b, tn=512):
    depth, d, n = ada_w.shape
    rows = cond.shape[0]
    return pl.pallas_call(
        _ada_kernel,
        grid=(depth, n // tn),
        in_specs=[pl.BlockSpec((rows, d), lambda l, j: (0, 0)),
                  pl.BlockSpec((None, d, tn), lambda l, j: (l, 0, j)),
                  pl.BlockSpec((None, 1, tn), lambda l, j: (l, 0, j))],
        out_specs=pl.BlockSpec((None, rows, tn), lambda l, j: (l, 0, j)),
        out_shape=jax.ShapeDtypeStruct((depth, rows, n), F32),
        compiler_params=_cparams("parallel", "parallel"),
        name="ada_modulation",
    )(cond, ada_w, ada_b.reshape(depth, 1, n))


def _norm_modulate(x, nw, sc, sh):
    ms = jnp.mean(x * x, axis=-1, keepdims=True)
    y = x * lax.rsqrt(ms + EPS) * nw
    return y * (1.0 + sc) + sh


NORM_ROWS = 32


def _fill_norm_modulate(h_ref, x_ref, nw_ref, sc_ref, sh_ref):
    def strip(r, carry):
        rows = pl.ds(pl.multiple_of(r * NORM_ROWS, NORM_ROWS), NORM_ROWS)
        h_ref[rows, :] = _norm_modulate(x_ref[rows, :], nw_ref[...], sc_ref[...], sh_ref[...]).astype(h_ref.dtype)
        return carry

    lax.fori_loop(0, x_ref.shape[0] // NORM_ROWS, strip, 0)


def _mod_index(layer, j, tiles_per_row, row0):
    def index(i, *_):
        return (layer, row0 + i // tiles_per_row, j, 0, 0)
    return index


PROJ_TN = 512
F32_TILES = (OFF_AK // PROJ_TN,) + tuple(range(OFF_FF // PROJ_TN, OFF_HI // PROJ_TN))
OFF32_KV = 0
OFF32_FF = PROJ_TN
OFF32_FB = OFF32_FF + HG_WIDTH
assert OFF_AK % PROJ_TN == 0 and 2 * KV_WIDTH == PROJ_TN and OFF_FF % PROJ_TN == 0 and OFF_HI % PROJ_TN == 0


def _f32_tile_slot(j):
    return jnp.maximum(sum((j >= tile).astype(jnp.int32) for tile in F32_TILES) - 1, 0)


def _ln_mm_kernel(x_ref, nw_ref, sc_ref, sh_ref, w_ref, o_ref, o32_ref, h_ref):
    j = pl.program_id(1)

    @pl.when(j == 0)
    def _():
        _fill_norm_modulate(h_ref, x_ref, nw_ref, sc_ref, sh_ref)

    acc = _dot(h_ref[...], w_ref[...])
    o_ref[...] = acc.astype(o_ref.dtype)
    keep = functools.reduce(jnp.logical_or, [j == tile for tile in F32_TILES])

    @pl.when(keep)
    def _():
        o32_ref[...] = acc


def ln_mod_matmul(x, nw, mod, mod_idx, w, *, layer, tiles_per_row, row0, tm):
    t, d = x.shape
    n = w.shape[1]
    tn = PROJ_TN
    sc_j, sh_j = mod_idx
    mspec = lambda j: pl.BlockSpec((None, None, None, 1, d), _mod_index(layer, j, tiles_per_row, row0))
    return pl.pallas_call(
        _ln_mm_kernel,
        grid=(t // tm, n // tn),
        in_specs=[pl.BlockSpec((tm, d), lambda i, j: (i, 0)),
                  pl.BlockSpec((1, d), lambda i, j: (0, 0)),
                  mspec(sc_j), mspec(sh_j),
                  pl.BlockSpec((d, tn), lambda i, j: (0, j))],
        out_specs=[pl.BlockSpec((tm, tn), lambda i, j: (i, j)),
                   pl.BlockSpec((tm, tn), lambda i, j: (i, _f32_tile_slot(j)))],
        out_shape=[jax.ShapeDtypeStruct((t, n), BF16),
                   jax.ShapeDtypeStruct((t, len(F32_TILES) * tn), F32)],
        scratch_shapes=[pltpu.VMEM((tm, d), BF16)],
        compiler_params=_cparams("parallel", "arbitrary"),
        name="ln_mod_matmul",
    )(x, nw.reshape(1, d), mod, mod, w)


def _ln_glu_kernel(x_ref, nw_ref, sc_ref, sh_ref, w1_ref, w3_ref, o_ref, h_ref):
    @pl.when(pl.program_id(1) == 0)
    def _():
        _fill_norm_modulate(h_ref, x_ref, nw_ref, sc_ref, sh_ref)

    h = h_ref[...]
    o_ref[...] = (_silu(_dot(h, w1_ref[...])) * _dot(h, w3_ref[...])).astype(o_ref.dtype)


def ln_mod_glu(x, nw, mod, mod_idx, w1, w3, *, layer, tiles_per_row, row0, tm, tn):
    t, d = x.shape
    n = w1.shape[1]
    sc_j, sh_j = mod_idx
    mspec = lambda j: pl.BlockSpec((None, None, None, 1, d), _mod_index(layer, j, tiles_per_row, row0))
    return pl.pallas_call(
        _ln_glu_kernel,
        grid=(t // tm, n // tn),
        in_specs=[pl.BlockSpec((tm, d), lambda i, j: (i, 0)),
                  pl.BlockSpec((1, d), lambda i, j: (0, 0)),
                  mspec(sc_j), mspec(sh_j),
                  pl.BlockSpec((d, tn), lambda i, j: (0, j)),
                  pl.BlockSpec((d, tn), lambda i, j: (0, j))],
        out_specs=pl.BlockSpec((tm, tn), lambda i, j: (i, j)),
        out_shape=jax.ShapeDtypeStruct((t, n), BF16),
        scratch_shapes=[pltpu.VMEM((tm, d), BF16)],
        compiler_params=_cparams("parallel", "arbitrary"),
        name="ln_mod_glu",
    )(x, nw.reshape(1, d), mod, mod, w1, w3)


def _merge_kernel(ya_ref, yb_ref, yc_ref, ma_ref, mb_ref, mc_ref, wa_ref, wb_ref, wc_ref, o_ref):
    gate = lambda m_ref: jax.nn.sigmoid(m_ref[...].astype(F32))
    acc = gate(ma_ref) * _dot(ya_ref[...], wa_ref[...])
    acc = acc + gate(mb_ref) * _dot(yb_ref[...], wb_ref[...])
    acc = acc + gate(mc_ref) * _dot(yc_ref[...], wc_ref[...])
    o_ref[...] = acc.astype(o_ref.dtype)


def branch_merge(ya, yb, yc, proj, wa, wb, wc, *, tm, tn):
    t, k = ya.shape
    n = wa.shape[1]
    gate = lambda off: pl.BlockSpec((tm, tn), lambda i, j: (i, off // tn + j))
    yspec = pl.BlockSpec((tm, k), lambda i, j: (i, 0))
    wspec = pl.BlockSpec((k, tn), lambda i, j: (0, j))
    return pl.pallas_call(
        _merge_kernel,
        grid=(t // tm, n // tn),
        in_specs=[yspec, yspec, yspec, gate(OFF_MA), gate(OFF_MA + n), gate(OFF_MA + 2 * n),
                  wspec, wspec, wspec],
        out_specs=pl.BlockSpec((tm, tn), lambda i, j: (i, j)),
        out_shape=jax.ShapeDtypeStruct((t, n), BF16),
        compiler_params=_cparams("parallel", "parallel"),
        name="branch_merge",
    )(ya, yb, yc, proj, proj, proj, wa, wb, wc)


def _mm_resid_kernel(a_ref, w_ref, x_ref, g_ref, o_ref):
    o_ref[...] = x_ref[...] + g_ref[...] * _dot(a_ref[...], w_ref[...])


def matmul_gated_residual(a, w, x, mod, g_j, *, layer, tiles_per_row, row0, tm, tn):
    t, k = a.shape
    n = w.shape[1]
    return pl.pallas_call(
        _mm_resid_kernel,
        grid=(t // tm, n // tn),
        in_specs=[pl.BlockSpec((tm, k), lambda i, j: (i, 0)),
                  pl.BlockSpec((k, tn), lambda i, j: (0, j)),
                  pl.BlockSpec((tm, tn), lambda i, j: (i, j)),
                  pl.BlockSpec((None, None, None, 1, tn),
                               lambda i, j: (layer, row0 + i // tiles_per_row, g_j, 0, j))],
        out_specs=pl.BlockSpec((tm, tn), lambda i, j: (i, j)),
        out_shape=jax.ShapeDtypeStruct((t, n), F32),
        compiler_params=_cparams("parallel", "parallel"),
        name="matmul_gated_residual",
    )(a, w, x, mod)


def _final_norm_kernel(x_ref, fw_ref, o_ref):
    x = x_ref[...]
    o_ref[...] = x * lax.rsqrt(jnp.mean(x * x, axis=-1, keepdims=True) + EPS) * fw_ref[...]


def final_norm(x, fw, *, tm):
    t, d = x.shape
    return pl.pallas_call(
        _final_norm_kernel,
        grid=(t // tm,),
        in_specs=[pl.BlockSpec((tm, d), lambda i: (i, 0)), pl.BlockSpec((1, d), lambda i: (0, 0))],
        out_specs=pl.BlockSpec((tm, d), lambda i: (i, 0)),
        out_shape=jax.ShapeDtypeStruct((t, d), F32),
        compiler_params=_cparams("parallel"),
        name="final_norm",
    )(x, fw.reshape(1, d))


MOE_ROW_TILE = 512
MOE_TOKEN_TILE = 256


def _router_kernel(x_ref, nw_ref, sc_ref, sh_ref, rw_ref, h_ref, tw_ref, route_ref, cnt_ref, run_ref):
    @pl.when(pl.program_id(0) == 0)
    def _():
        run_ref[...] = jnp.zeros_like(run_ref)

    h = _norm_modulate(x_ref[...], nw_ref[...], sc_ref[...], sh_ref[...])
    h_ref[...] = h
    tm = h.shape[0]
    lane = lax.broadcasted_iota(jnp.int32, (tm, LANES), 1)
    logits = jnp.where(lane < N_EXPERTS, _dot_f32(h, rw_ref[...]), -jnp.inf)
    m1 = jnp.max(logits, axis=-1, keepdims=True)
    i1 = jnp.min(jnp.where(logits == m1, lane, LANES), axis=-1, keepdims=True)
    rest = jnp.where(lane == i1, -jnp.inf, logits)
    m2 = jnp.max(rest, axis=-1, keepdims=True)
    i2 = jnp.min(jnp.where(rest == m2, lane, LANES), axis=-1, keepdims=True)
    e2 = jnp.exp(m2 - m1)
    inv = 1.0 / (1.0 + e2)
    tw_ref[...] = jnp.where(lane == 0, inv, jnp.where(lane == 1, e2 * inv, 0.0))

    sel = jnp.where((lane == i1) | (lane == i2), 1.0, 0.0)
    before = lax.broadcasted_iota(jnp.int32, (tm, tm), 1) < lax.broadcasted_iota(jnp.int32, (tm, tm), 0)
    rank = _dot(jnp.where(before, 1.0, 0.0).astype(BF16), sel.astype(BF16)) + run_ref[0:1, :]
    r1 = jnp.sum(jnp.where(lane == i1, rank, 0.0), axis=-1, keepdims=True).astype(jnp.int32)
    r2 = jnp.sum(jnp.where(lane == i2, rank, 0.0), axis=-1, keepdims=True).astype(jnp.int32)
    route_ref[...] = jnp.where(lane == 0, i1, jnp.where(lane == 1, i2, jnp.where(lane == 2, r1, r2)))
    run_ref[...] = run_ref[...] + jnp.sum(sel, axis=0, keepdims=True)
    cnt_ref[...] = run_ref[...]


def router(x, nw, mod, mod_idx, rw, *, layer, tiles_per_row, row0, tm):
    t, d = x.shape
    sc_j, sh_j = mod_idx
    rw_pad = jnp.zeros((d, LANES), F32).at[:, :N_EXPERTS].set(rw)
    mspec = lambda j: pl.BlockSpec((None, None, None, 1, d), _mod_index(layer, j, tiles_per_row, row0))
    return pl.pallas_call(
        _router_kernel,
        grid=(t // tm,),
        in_specs=[pl.BlockSpec((tm, d), lambda i: (i, 0)),
                  pl.BlockSpec((1, d), lambda i: (0, 0)),
                  mspec(sc_j), mspec(sh_j),
                  pl.BlockSpec((d, LANES), lambda i: (0, 0))],
        out_specs=[pl.BlockSpec((tm, d), lambda i: (i, 0)),
                   pl.BlockSpec((tm, LANES), lambda i: (i, 0)),
                   pl.BlockSpec((tm, LANES), lambda i: (i, 0)),
                   pl.BlockSpec((8, LANES), lambda i: (0, 0))],
        out_shape=[jax.ShapeDtypeStruct((t, d), F32), jax.ShapeDtypeStruct((t, LANES), F32),
                   jax.ShapeDtypeStruct((t, LANES), jnp.int32), jax.ShapeDtypeStruct((8, LANES), F32)],
        scratch_shapes=[pltpu.VMEM((8, LANES), F32)],
        compiler_params=_cparams("arbitrary"),
        name="router",
    )(x, nw.reshape(1, d), mod, mod, rw_pad)


def _row_copy(src, src_row, dst, dst_row, sem):
    return pltpu.make_async_copy(src.at[pl.ds(src_row, 1), :], dst.at[pl.ds(dst_row, 1), :], sem)


def _dispatch_kernel(pos_ref, ends_ref, h_ref, xs_ref, zero_ref, sem, zsem):
    nt = h_ref.shape[0]
    tm = zero_ref.shape[0]

    @pl.when(pl.program_id(0) == 0)
    def _():
        zero_ref[...] = jnp.zeros_like(zero_ref)

        def last_tile_copy(e):
            row0 = pl.multiple_of((ends_ref[e] - 1) * tm, tm)
            return pltpu.make_async_copy(zero_ref, xs_ref.at[pl.ds(row0, tm), :], zsem)

        def has_rows(e):
            return ends_ref[e] > (ends_ref[e - 1] if e else 0)

        n_tiles = xs_ref.shape[0] // tm

        def spare_tile_copy(k):
            row0 = pl.multiple_of((ends_ref[N_EXPERTS - 1] + k) * tm, tm)
            return pltpu.make_async_copy(zero_ref, xs_ref.at[pl.ds(row0, tm), :], zsem)

        def is_spare(k):
            return ends_ref[N_EXPERTS - 1] + k < n_tiles

        for e in range(N_EXPERTS):
            @pl.when(has_rows(e))
            def _():
                last_tile_copy(e).start()

            @pl.when(is_spare(e))
            def _():
                spare_tile_copy(e).start()
        for e in range(N_EXPERTS):
            @pl.when(has_rows(e))
            def _():
                last_tile_copy(e).wait()

            @pl.when(is_spare(e))
            def _():
                spare_tile_copy(e).wait()

    def start(r, carry):
        for k in range(TOP_K):
            _row_copy(h_ref, r, xs_ref, pos_ref[0, k * nt + r], sem).start()
        return carry

    def wait(r, carry):
        for k in range(TOP_K):
            _row_copy(h_ref, r, xs_ref, pos_ref[0, k * nt + r], sem).wait()
        return carry

    lax.fori_loop(0, nt, start, 0, unroll=8)
    lax.fori_loop(0, nt, wait, 0, unroll=8)


def moe_dispatch(h, pos, tile_ends, rows):
    t, d = h.shape
    nt = MOE_TOKEN_TILE
    return pl.pallas_call(
        _dispatch_kernel,
        grid=(t // nt,),
        in_specs=[pl.BlockSpec((None, 1, TOP_K * nt), lambda i: (i, 0, 0), memory_space=pltpu.SMEM),
                  pl.BlockSpec(memory_space=pltpu.SMEM),
                  pl.BlockSpec((nt, d), lambda i: (i, 0))],
        out_specs=pl.BlockSpec(memory_space=pl.ANY),
        out_shape=jax.ShapeDtypeStruct((rows, d), F32),
        scratch_shapes=[pltpu.VMEM((MOE_ROW_TILE, d), F32), pltpu.SemaphoreType.DMA(()),
                        pltpu.SemaphoreType.DMA(())],
        compiler_params=_cparams("arbitrary"),
        name="moe_dispatch",
    )(pos, tile_ends, h)


def _gmm_up_kernel(te_ref, nu_ref, xs_ref, w1_ref, w3_ref, o_ref, a_ref):
    del te_ref
    used = pl.program_id(0) < nu_ref[0]

    @pl.when(used)
    def _():
        @pl.when(pl.program_id(1) == 0)
        def _():
            a_ref[...] = xs_ref[...].astype(BF16)

        a = a_ref[...]
        o_ref[...] = (_silu(_dot(a, w1_ref[...])) * _dot(a, w3_ref[...])).astype(o_ref.dtype)

    @pl.when(jnp.logical_not(used))
    def _():
        o_ref[...] = jnp.zeros_like(o_ref)


def gmm_up(xs, w1, w3, tile_expert, n_used, *, tn):
    rows, d = xs.shape
    n = w1.shape[2]
    tm = MOE_ROW_TILE
    wspec = pl.BlockSpec((None, d, tn), lambda i, j, te, nu: (te[i], 0, j))
    return pl.pallas_call(
        _gmm_up_kernel,
        grid_spec=pltpu.PrefetchScalarGridSpec(
            num_scalar_prefetch=2,
            grid=(rows // tm, n // tn),
            in_specs=[pl.BlockSpec((tm, d), lambda i, j, te, nu: (jnp.minimum(i, nu[0] - 1), 0)),
                      wspec, wspec],
            out_specs=pl.BlockSpec((tm, tn), lambda i, j, te, nu: (i, j)),
            scratch_shapes=[pltpu.VMEM((tm, d), BF16)]),
        out_shape=jax.ShapeDtypeStruct((rows, n), BF16),
        compiler_params=_cparams("parallel", "arbitrary"),
        name="gmm_up",
    )(tile_expert, n_used, xs, w1, w3)


def _gmm_down_kernel(te_ref, nu_ref, f_ref, w_ref, o_ref):
    del te_ref
    used = pl.program_id(0) < nu_ref[0]

    @pl.when(used)
    def _():
        o_ref[...] = _dot(f_ref[...], w_ref[...])

    @pl.when(jnp.logical_not(used))
    def _():
        o_ref[...] = jnp.zeros_like(o_ref)


def gmm_down(f, w2, tile_expert, n_used, *, tn):
    rows, k = f.shape
    n = w2.shape[2]
    tm = MOE_ROW_TILE
    return pl.pallas_call(
        _gmm_down_kernel,
        grid_spec=pltpu.PrefetchScalarGridSpec(
            num_scalar_prefetch=2,
            grid=(rows // tm, n // tn),
            in_specs=[pl.BlockSpec((tm, k), lambda i, j, te, nu: (jnp.minimum(i, nu[0] - 1), 0)),
                      pl.BlockSpec((None, k, tn), lambda i, j, te, nu: (te[i], 0, j))],
            out_specs=pl.BlockSpec((tm, tn), lambda i, j, te, nu: (i, j))),
        out_shape=jax.ShapeDtypeStruct((rows, n), F32),
        compiler_params=_cparams("parallel", "parallel"),
        name="gmm_down",
    )(tile_expert, n_used, f, w2)


def _combine_kernel(final, pos_ref, x_ref, tw_ref, g_ref, fw_ref, ys_ref, o_ref, y0_ref, y1_ref, sem):
    nt = x_ref.shape[0]
    bufs = (y0_ref, y1_ref)

    def start(r, carry):
        for k in range(TOP_K):
            _row_copy(ys_ref, pos_ref[0, k * nt + r], bufs[k], r, sem).start()
        return carry

    def wait(r, carry):
        for k in range(TOP_K):
            _row_copy(ys_ref, pos_ref[0, k * nt + r], bufs[k], r, sem).wait()
        return carry

    lax.fori_loop(0, nt, start, 0, unroll=8)
    lax.fori_loop(0, nt, wait, 0, unroll=8)
    tw = tw_ref[...]
    x = x_ref[...] + g_ref[...] * (tw[:, 0:1] * y0_ref[...] + tw[:, 1:2] * y1_ref[...])
    if final:
        x = x * lax.rsqrt(jnp.mean(x * x, axis=-1, keepdims=True) + EPS) * fw_ref[...]
    o_ref[...] = x


def moe_combine(x, ys, pos, tw, mod, g_j, fw, *, layer, tiles_per_row, row0, final):
    t, d = x.shape
    nt = MOE_TOKEN_TILE
    return pl.pallas_call(
        functools.partial(_combine_kernel, final),
        grid=(t // nt,),
        in_specs=[pl.BlockSpec((None, 1, TOP_K * nt), lambda i: (i, 0, 0), memory_space=pltpu.SMEM),
                  pl.BlockSpec((nt, d), lambda i: (i, 0)),
                  pl.BlockSpec((nt, LANES), lambda i: (i, 0)),
                  pl.BlockSpec((None, None, None, 1, d), _mod_index(layer, g_j, tiles_per_row, row0)),
                  pl.BlockSpec((1, d), lambda i: (0, 0)),
                  pl.BlockSpec(memory_space=pl.ANY)],
        out_specs=pl.BlockSpec((nt, d), lambda i: (i, 0)),
        out_shape=jax.ShapeDtypeStruct((t, d), F32),
        scratch_shapes=[pltpu.VMEM((nt, d), F32), pltpu.VMEM((nt, d), F32), pltpu.SemaphoreType.DMA(())],
        compiler_params=_cparams("arbitrary"),
        name="moe_combine",
    )(pos, x, tw, mod, fw.reshape(1, d), ys)


def _moe_plan(route, counts, t):
    tm = MOE_ROW_TILE
    nt = MOE_TOKEN_TILE
    n_tiles = TOP_K * t // tm + N_EXPERTS
    cnt = counts[0, :N_EXPERTS].astype(jnp.int32)
    tiles = (cnt + tm - 1) // tm
    ends = jnp.cumsum(tiles)
    offs = (ends - tiles) * tm
    pos = [jnp.take(offs, route[:, k]) + route[:, TOP_K + k] for k in range(TOP_K)]
    pos = jnp.concatenate([p.reshape(t // nt, nt) for p in pos], axis=1).reshape(t // nt, 1, TOP_K * nt)
    tile_expert = jnp.sum(jnp.arange(n_tiles, dtype=jnp.int32)[:, None] >= ends[None, :], axis=1)
    tile_expert = jnp.minimum(tile_expert, N_EXPERTS - 1).astype(jnp.int32)
    ends = ends.astype(jnp.int32)
    return pos, tile_expert, ends, ends[-1:], n_tiles * tm


def _hyena_filter_kernel(z_ref, w1_ref, b1_ref, fr_ref, w2_ref, b2_ref, w3_ref, t_ref, dec_ref, o_ref):
    fr = fr_ref[...]
    h = jnp.sin(fr * (_dot_f32(z_ref[...], w1_ref[...]) + b1_ref[...]))
    h = jnp.sin(fr * (_dot_f32(h, w2_ref[...]) + b2_ref[...]))
    o_ref[...] = _dot_f32(h, w3_ref[...]) * jnp.exp(-t_ref[...] * jnp.abs(dec_ref[...]))


def hyena_filter(seq, w1, b1, freq, w2, b2, w3, decay, tl=256):
    t = jnp.linspace(0.0, 1.0, seq, dtype=F32)[:, None]
    pos = jnp.arange(seq, dtype=F32)[:, None]
    bands = jnp.linspace(1e-4, HY_BANDS - 1.0, HY_BANDS, dtype=F32)[None, :]
    ang = (2.0 * math.pi / seq) * pos * bands
    z = jnp.concatenate([t, jnp.cos(ang), -jnp.sin(ang)], axis=-1)
    emb = z.shape[1]
    emb_pad = LANES
    z = jnp.pad(z, ((0, 0), (0, emb_pad - emb)))
    w1p = jnp.pad(w1, ((0, emb_pad - emb), (0, 0)))
    hid = w1.shape[1]
    n = w3.shape[1]
    tl = min(tl, seq)
    full = lambda shape: pl.BlockSpec(shape, lambda i: (0, 0))
    return pl.pallas_call(
        _hyena_filter_kernel,
        grid=(seq // tl,),
        in_specs=[pl.BlockSpec((tl, emb_pad), lambda i: (i, 0)),
                  full((emb_pad, hid)), full((1, hid)), full((1, hid)),
                  full((hid, hid)), full((1, hid)), full((hid, n)),
                  pl.BlockSpec((tl, 1), lambda i: (i, 0)), full((1, n))],
        out_specs=pl.BlockSpec((tl, n), lambda i: (i, 0)),
        out_shape=jax.ShapeDtypeStruct((seq, n), F32),
        compiler_params=_cparams("parallel"),
        name="hyena_filter",
    )(z, w1p, b1.reshape(1, hid), freq.reshape(1, hid), w2, b2.reshape(1, hid), w3, t,
      decay.reshape(1, n))


def _dft_tables(seq, kb):
    n = 2 * seq
    k = jnp.arange(seq, dtype=jnp.int32)[:, None]
    s = jnp.arange(seq, dtype=jnp.int32)[None, :]
    ang = ((k * s) % n).astype(F32) * (2.0 * math.pi / n)
    cos = jnp.cos(ang)
    sin = jnp.sin(ang)
    nyq = jnp.where(s % 2 == 0, 1.0, -1.0).astype(F32)
    is0 = k == 0
    f_re = cos
    f_im = jnp.where(is0, nyq, -sin)
    i_re = jnp.where(is0, 1.0 / n, (2.0 / n) * cos)
    i_im = jnp.where(is0, nyq / n, -(2.0 / n) * sin)
    nkb = seq // kb
    fwd = jnp.concatenate([f_re.reshape(nkb, kb, seq), f_im.reshape(nkb, kb, seq)], axis=1)
    inv = jnp.concatenate([i_re.reshape(nkb, kb, seq), i_im.reshape(nkb, kb, seq)], axis=1)
    return fwd, jnp.swapaxes(inv, 1, 2)


def _spectrum_kernel(kb, f_ref, hf_ref, hb_ref, bias_ref, o_ref):
    f = f_ref[...]
    row = lax.broadcasted_iota(jnp.int32, hb_ref.shape, 0)
    hb0 = jnp.where(row == 0, 0.0, hb_ref[...])
    a = _dot_f32_3pass(f, hf_ref[...])
    b = _dot_f32_3pass(f, hb0)
    orow = lax.broadcasted_iota(jnp.int32, a.shape, 0)
    nyq_slot = (orow == kb) & (pl.program_id(0) == 0)
    o_ref[...] = jnp.where((orow < kb) | nyq_slot, a + b + bias_ref[...], a - b)


def hyena_spectrum(fwd_f32, filt, bias, kb, tc=256):
    nkb, kb2, seq = fwd_f32.shape
    c = filt.shape[1] // 2
    return pl.pallas_call(
        functools.partial(_spectrum_kernel, kb),
        grid=(nkb, c // tc),
        in_specs=[pl.BlockSpec((None, kb2, seq), lambda j, i: (j, 0, 0)),
                  pl.BlockSpec((seq, tc), lambda j, i: (0, i)),
                  pl.BlockSpec((seq, tc), lambda j, i: (0, c // tc + i)),
                  pl.BlockSpec((1, tc), lambda j, i: (0, i))],
        out_specs=pl.BlockSpec((None, kb2, tc), lambda j, i: (j, 0, i)),
        out_shape=jax.ShapeDtypeStruct((nkb, kb2, c), F32),
        compiler_params=_cparams("parallel", "parallel"),
        name="hyena_spectrum",
    )(fwd_f32, filt, filt, bias.reshape(1, c))


def _hyena_conv_kernel(kb, x0_ref, x1_ref, v_ref, cw0_ref, cw1_ref, cwv_ref, cb0_ref, cb1_ref, cbv_ref,
                       f_ref, i_ref, kf_ref, o_ref, u16_ref, acc_ref):
    j = pl.program_id(2)
    seq = x0_ref.shape[0]

    def conv3(x_ref, w_ref, b_ref):
        x = x_ref[...].astype(F32)
        row = lax.broadcasted_iota(jnp.int32, x.shape, 0)
        prev = jnp.where(row == 0, 0.0, pltpu.roll(x, 1, 0))
        nxt = jnp.where(row == seq - 1, 0.0, pltpu.roll(x, seq - 1, 0))
        w = w_ref[...]
        return prev * w[0:1, :] + x * w[1:2, :] + nxt * w[2:3, :] + b_ref[...]

    @pl.when(j == 0)
    def _():
        u16_ref[...] = (conv3(v_ref, cwv_ref, cbv_ref) * conv3(x1_ref, cw1_ref, cb1_ref)).astype(BF16)
        acc_ref[...] = jnp.zeros_like(acc_ref)

    spec = _dot(f_ref[...], u16_ref[...])
    xr, xi = spec[:kb], spec[kb:]
    kf = kf_ref[...]
    kr, ki = kf[:kb], kf[kb:]
    packed = (lax.broadcasted_iota(jnp.int32, xr.shape, 0) == 0) & (j == 0)
    yr = xr * kr - jnp.where(packed, 0.0, xi * ki)
    yi = jnp.where(packed, xi * ki, xr * ki + xi * kr)
    y = jnp.concatenate([yr, yi], axis=0).astype(BF16)
    acc_ref[...] += _dot(i_ref[...], y)

    @pl.when(j == pl.num_programs(2) - 1)
    def _():
        o_ref[...] = (acc_ref[...] * conv3(x0_ref, cw0_ref, cb0_ref)).astype(o_ref.dtype)


def hyena_conv(proj, conv_w, conv_b, fwd, inv, spectrum, *, batch, seq, ct, kb):
    c = HY_WIDTH
    nkb = seq // kb
    ncb = c // ct
    col = lambda part: pl.BlockSpec((seq, ct), lambda b, i, j: (b, part * ncb + i))
    cw = lambda part: pl.BlockSpec((3, ct), lambda b, i, j: (0, part * ncb + i))
    cb = lambda part: pl.BlockSpec((1, ct), lambda b, i, j: (0, part * ncb + i))
    conv_b = conv_b.reshape(1, 3 * c)
    return pl.pallas_call(
        functools.partial(_hyena_conv_kernel, kb),
        grid=(batch, ncb, nkb),
        in_specs=[col(0), col(1), col(2), cw(0), cw(1), cw(2), cb(0), cb(1), cb(2),
                  pl.BlockSpec((None, 2 * kb, seq), lambda b, i, j: (j, 0, 0)),
                  pl.BlockSpec((None, seq, 2 * kb), lambda b, i, j: (j, 0, 0)),
                  pl.BlockSpec((None, 2 * kb, ct), lambda b, i, j: (j, 0, i))],
        out_specs=pl.BlockSpec((seq, ct), lambda b, i, j: (b, i)),
        out_shape=jax.ShapeDtypeStruct((batch * seq, c), BF16),
        scratch_shapes=[pltpu.VMEM((seq, ct), BF16), pltpu.VMEM((seq, ct), F32)],
        compiler_params=_cparams("parallel", "parallel", "arbitrary"),
        name="hyena_conv",
    )(proj, proj, proj, conv_w, conv_w, conv_w, conv_b, conv_b, conv_b, fwd, inv, spectrum)


def _rope(x, cos, sin_a, sin_b):
    return x * cos + pltpu.roll(x, HEAD_DIM - HEAD_DIM // 4, 1) * sin_a + pltpu.roll(x, HEAD_DIM // 4, 1) * sin_b


def _dot_nt(a, b):
    return lax.dot_general(a, b, (((1,), (1,)), ((), ())), preferred_element_type=F32)


def _grouped_softmax_pv(qs, keys, values, sinks, valid, o_ref):
    scores = [_dot_nt(q, keys) * ATTN_SCALE for q in qs]
    probs, dens = [], []
    for s, sink in zip(scores, sinks):
        if valid is not None:
            s = jnp.where(valid, s, NEG_BIG)
        m = jnp.maximum(jnp.max(s, axis=-1, keepdims=True), sink)
        p = jnp.exp(s - m)
        dens.append(jnp.sum(p, axis=-1, keepdims=True) + jnp.exp(sink - m))
        probs.append(p.astype(BF16))
    outs = [_dot(p, values) for p in probs]
    for g, (o, den) in enumerate(zip(outs, dens)):
        o_ref[:, g * HEAD_DIM:(g + 1) * HEAD_DIM] = (o / den).astype(o_ref.dtype)


def _ctx_attn_kernel(sink_ref, q_ref, k_ref, v_ref, o_ref):
    kvh = pl.program_id(1)
    qs = [q_ref[:, g * HEAD_DIM:(g + 1) * HEAD_DIM] for g in range(GROUP)]
    sinks = [sink_ref[kvh * GROUP + g] for g in range(GROUP)]
    _grouped_softmax_pv(qs, k_ref[...].astype(BF16), v_ref[...].astype(BF16), sinks, None, o_ref)


def context_attention(proj, proj32, sink, *, batch, seq):
    qw = GROUP * HEAD_DIM
    kv = lambda part: pl.BlockSpec((seq, HEAD_DIM),
                                   lambda b, h: (b, OFF32_KV // HEAD_DIM + part * N_KV_HEADS + h))
    return pl.pallas_call(
        _ctx_attn_kernel,
        grid=(batch, N_KV_HEADS),
        in_specs=[pl.BlockSpec(memory_space=pltpu.SMEM),
                  pl.BlockSpec((seq, qw), lambda b, h: (b, OFF_AQ // qw + h)),
                  kv(0), kv(1)],
        out_specs=pl.BlockSpec((seq, qw), lambda b, h: (b, h)),
        out_shape=jax.ShapeDtypeStruct((batch * seq, ATTN_WIDTH), BF16),
        compiler_params=_cparams("parallel", "parallel"),
        name="context_attention",
    )(sink, proj, proj32, proj32)


def _lat_attn_kernel(seq, sink_ref, q_ref, k_ref, v_ref, ck_ref, cv_ref, cos_ref, sa_ref, sb_ref,
                     o_ref, kr_ref, vb_ref):
    kvh = pl.program_id(1)
    qb = pl.program_id(2)
    blk = q_ref.shape[0]
    nwin = 3 * blk
    past = ck_ref.shape[0]

    @pl.when(qb == 0)
    def _():
        kr_ref[...] = _rope(k_ref[...], cos_ref[...], sa_ref[...], sb_ref[...]).astype(BF16)
        vb_ref[...] = v_ref[...].astype(BF16)

    start = pl.multiple_of(jnp.clip((qb - 1) * blk, 0, seq - nwin), blk)
    keys = jnp.concatenate([kr_ref[pl.ds(start, nwin), :], ck_ref[...].astype(BF16)], axis=0)
    values = jnp.concatenate([vb_ref[pl.ds(start, nwin), :], cv_ref[...].astype(BF16)], axis=0)
    rows = pl.ds(pl.multiple_of(qb * blk, blk), blk)
    cos, sa, sb = cos_ref[rows, :], sa_ref[rows, :], sb_ref[rows, :]
    qpos = qb * blk + lax.broadcasted_iota(jnp.int32, (blk, nwin + past), 0)
    col = lax.broadcasted_iota(jnp.int32, (blk, nwin + past), 1)
    valid = (col >= nwin) | (jnp.abs(qpos - (start + col)) <= WINDOW)
    qs = [_rope(q_ref[:, g * HEAD_DIM:(g + 1) * HEAD_DIM].astype(F32), cos, sa, sb).astype(BF16)
          for g in range(GROUP)]
    sinks = [sink_ref[kvh * GROUP + g] for g in range(GROUP)]
    _grouped_softmax_pv(qs, keys, values, sinks, valid, o_ref)


def _rope_tables(seq):
    rows = seq // GRID_W
    row = jnp.repeat(jnp.arange(rows, dtype=F32), GRID_W)
    col = jnp.tile(jnp.arange(GRID_W, dtype=F32), rows)
    quarter = HEAD_DIM // 4
    inv = ROPE_BASE ** (-jnp.arange(quarter, dtype=F32) / quarter)
    ar = row[:, None] * inv
    ac = col[:, None] * inv
    ang = jnp.concatenate([ar, ar, ac, ac], axis=-1)
    cos, sin = jnp.cos(ang), jnp.sin(ang)
    first = (jnp.arange(HEAD_DIM) % (2 * quarter)) < quarter
    return cos, jnp.where(first, -sin, 0.0), jnp.where(first, 0.0, sin)


def latent_attention(proj, proj32, cache_k, cache_v, sink, *, layer, batch, seq, blk=128):
    qw = GROUP * HEAD_DIM
    past = cache_k.shape[2]
    nqb = seq // blk
    cos, sa, sb = _rope_tables(seq)
    table = pl.BlockSpec((seq, HEAD_DIM), lambda b, h, i: (0, 0))
    cache = pl.BlockSpec((None, None, past, HEAD_DIM), lambda b, h, i: (b, layer, 0, h))
    kv = lambda part: pl.BlockSpec((seq, HEAD_DIM),
                                   lambda b, h, i: (b, OFF32_KV // HEAD_DIM + part * N_KV_HEADS + h))
    return pl.pallas_call(
        functools.partial(_lat_attn_kernel, seq),
        grid=(batch, N_KV_HEADS, nqb),
        in_specs=[pl.BlockSpec(memory_space=pltpu.SMEM),
                  pl.BlockSpec((blk, qw), lambda b, h, i: (b * nqb + i, OFF_AQ // qw + h)),
                  kv(0), kv(1),
                  cache, cache, table, table, table],
        out_specs=pl.BlockSpec((blk, qw), lambda b, h, i: (b * nqb + i, h)),
        out_shape=jax.ShapeDtypeStruct((batch * seq, ATTN_WIDTH), BF16),
        scratch_shapes=[pltpu.VMEM((seq, HEAD_DIM), BF16), pltpu.VMEM((seq, HEAD_DIM), BF16)],
        compiler_params=_cparams("parallel", "parallel", "arbitrary"),
        name="latent_attention",
    )(sink, proj, proj32, proj32, cache_k, cache_v, cos, sa, sb)


_HG_LEVELS = tuple(HG_CHUNK >> (i + 1) for i in range(int(math.log2(HG_CHUNK))))


def _hgrn_tables():
    c = HG_CHUNK
    t = np.arange(c)[:, None]
    u = np.arange(c)[None, :]
    blocks = [(u <= t), (u > t)]
    for m in _HG_LEVELS:
        ref = (t // (2 * m)) * (2 * m) + m - 1
        second = (t % (2 * m)) >= m
        blocks.append(np.where(second, (u > ref) & (u <= t), (u > t) & (u <= ref)))
    fwd = np.concatenate(blocks, axis=0).astype(np.float32)
    bwd = np.concatenate([b[::-1, ::-1] for b in blocks], axis=0).astype(np.float32)
    s = u
    level = np.full((c, c), len(_HG_LEVELS) + 1, np.int32)
    level[t == s] = len(_HG_LEVELS)
    for i, m in enumerate(_HG_LEVELS):
        hit = (t // (2 * m) == s // (2 * m)) & ((t % (2 * m)) >= m) & ((s % (2 * m)) < m)
        level[hit] = i
    a = np.stack([fwd, bwd])
    a = np.concatenate([a, a], axis=2)
    lv = np.stack([level, level.T])
    return jnp.asarray(a, BF16), jnp.asarray(lv, jnp.int32)


def _hgrn_kernel(layer, has_s0, seq, heads, *refs):
    if has_s0:
        (q_ref, ff_ref, fb_ref, i_ref, g_ref, lb_ref, nw_ref, a_ref, lv_ref, s0_ref,
         y_ref, sfin_ref, of_ref, ob_ref, st_ref) = refs
    else:
        (q_ref, ff_ref, fb_ref, i_ref, g_ref, lb_ref, nw_ref, a_ref, lv_ref,
         y_ref, sfin_ref, of_ref, ob_ref, st_ref) = refs
    c = HG_CHUNK
    nlev = len(_HG_LEVELS)
    nc = seq // c

    lbs = lb_ref[...]
    mx = jnp.max(lbs, axis=0, keepdims=True)
    ex = jnp.exp(lbs - mx)
    sm = ex / jnp.sum(ex, axis=0, keepdims=True)
    lb = jnp.zeros(sm.shape[1:], F32)
    for j in range(1, layer + 1):
        lb = lb + sm[j]

    for d in range(2):
        for h in range(heads):
            if has_s0:
                st_ref[d, h] = s0_ref[d, h].T
            else:
                st_ref[d, h] = jnp.zeros((HG_DV, HG_DK), F32)

    def decay_exponents(d, ci):
        rows = pl.ds(pl.multiple_of(ci * c, c), c)
        fpre = (ff_ref if d == 0 else fb_ref)[rows, :]
        lbd = lb[d:d + 1, :]
        f = jnp.maximum(lbd, LB_FLOOR) + (1.0 - lbd) * jax.nn.sigmoid(fpre)
        log2f = jnp.log(f) * (1.0 / math.log(2.0))
        args = _dot(a_ref[d], jnp.concatenate(_split3(log2f)[:2], axis=0))
        chains = []
        for h in range(heads):
            cols = slice(h * HG_DK, (h + 1) * HG_DK)
            q = _silu(q_ref[rows, cols].astype(F32)) * (HG_DK ** -0.5)
            chains.append(dict(d=d, h=h, rows=rows, cols=cols, q=q, k=1.0 - f[:, cols],
                               v=i_ref[rows, cols].astype(BF16), args=args[:, cols]))
        return chains

    def level_products(s):
        q, k = s['q'], s['k']
        e = jnp.exp2(s['args'])
        s['q_in'] = (q * e[0:c]).astype(BF16)
        s['k_out'] = (k * e[c:2 * c]).astype(BF16)
        last = (c - 1) if s['d'] == 0 else 0
        s['total'] = e[last:last + 1]
        prods = [_dot_nt((q * e[(2 + i) * c:(3 + i) * c]).astype(BF16), (k * e[(2 + i) * c:(3 + i) * c]).astype(BF16))
                 for i in range(nlev)]
        s['prods'] = prods + [_dot_nt(q.astype(BF16), k.astype(BF16))]
        return s

    def outputs_and_state(s, o_ref):
        d, h = s['d'], s['h']
        lv = lv_ref[d]
        att = jnp.zeros((c, c), F32)
        for i in range(nlev + 1):
            att = jnp.where(lv == i, s['prods'][i], att)
        st = st_ref[d, h]
        o_ref[s['rows'], s['cols']] = _dot_nt(s['q_in'], st.astype(BF16)) + _dot(att.astype(BF16), s['v'])
        upd = lax.dot_general(s['v'], s['k_out'], (((0,), (0,)), ((), ())), preferred_element_type=F32)
        st_ref[d, h] = st * s['total'] + upd

    def body(ci, carry):
        stage = decay_exponents(0, ci) + decay_exponents(1, nc - 1 - ci)
        stage = [level_products(s) for s in stage]
        for s in stage:
            outputs_and_state(s, of_ref if s['d'] == 0 else ob_ref)
        return carry

    lax.fori_loop(0, nc, body, 0)

    for h in range(heads):
        cols = slice(h * HG_DK, (h + 1) * HG_DK)
        o = of_ref[:, cols] + ob_ref[:, cols]
        o = o * lax.rsqrt(jnp.mean(o * o, axis=-1, keepdims=True) + EPS) * nw_ref[...] * _silu(g_ref[:, cols].astype(F32))
        y_ref[:, cols] = o.astype(y_ref.dtype)
        for d in range(2):
            sfin_ref[d, h] = st_ref[d, h].T


HG_HEADS_PER_STEP = 2


def hgrn2_mix(proj, proj32, hg_lb, norm_w, s0, *, layer, batch, seq):
    a_tab, lv_tab = _hgrn_tables()
    depth = hg_lb.shape[0]
    hps = HG_HEADS_PER_STEP
    w = hps * HG_DK
    col = lambda off: pl.BlockSpec((seq, w), lambda b, h: (b, off // w + h))
    state = pl.BlockSpec((None, 2, hps, HG_DK, HG_DV), lambda b, h: (b, 0, h, 0, 0))
    in_specs = [col(OFF_HQ), col(OFF32_FF), col(OFF32_FB), col(OFF_HI), col(OFF_HG),
                pl.BlockSpec((depth, 2, w), lambda b, h: (0, 0, h)),
                pl.BlockSpec((1, HG_DV), lambda b, h: (0, 0)),
                pl.BlockSpec(a_tab.shape, lambda b, h: (0, 0, 0)),
                pl.BlockSpec(lv_tab.shape, lambda b, h: (0, 0, 0))]
    args = [proj, proj32, proj32, proj, proj, hg_lb, norm_w.reshape(1, HG_DV), a_tab, lv_tab]
    if s0 is not None:
        in_specs.append(state)
        args.append(s0)
    return pl.pallas_call(
        functools.partial(_hgrn_kernel, layer, s0 is not None, seq, hps),
        grid=(batch, HG_HEADS // hps),
        in_specs=in_specs,
        out_specs=[pl.BlockSpec((seq, w), lambda b, h: (b, h)), state],
        out_shape=[jax.ShapeDtypeStruct((batch * seq, HG_WIDTH), BF16),
                   jax.ShapeDtypeStruct((batch, 2, HG_HEADS, HG_DK, HG_DV), F32)],
        scratch_shapes=[pltpu.VMEM((seq, w), F32), pltpu.VMEM((seq, w), F32),
                        pltpu.VMEM((2, hps, HG_DV, HG_DK), F32)],
        compiler_params=_cparams("parallel", "parallel"),
        name="hgrn2_mix",
    )(*args)


def _trunk_layer(x, l, p, *, batch, seq, mod, row0, ctx, final):
    t, d = x.shape
    tm = 1024
    per_row = (t if ctx is None else seq) // tm
    geo = dict(layer=l, tiles_per_row=per_row, row0=row0)

    proj, proj32 = ln_mod_matmul(x, p['norm1_w'][l], mod, (1, 0), p['w_in'][l], tm=tm, **geo)

    filt = hyena_filter(seq, p['hy_w1'][l], p['hy_b1'][l], p['hy_freq'][l], p['hy_w2'][l], p['hy_b2'][l],
                        p['hy_w3'][l], p['hy_decay'][l])
    kb = min(seq, 512)
    fwd, inv = _dft_tables(seq, kb)
    spectrum = hyena_spectrum(fwd, filt, p['hy_bias'][l], kb)
    ya = hyena_conv(proj, p['hy_conv_w'][l], p['hy_conv_b'][l], fwd.astype(BF16), inv.astype(BF16),
                    spectrum, batch=batch, seq=seq, ct=512, kb=kb)

    if ctx is None:
        yb = context_attention(proj, proj32, p['attn_sink'][l], batch=batch, seq=seq)
        s0 = None
    else:
        cache_k, cache_v, s0 = ctx
        yb = latent_attention(proj, proj32, cache_k, cache_v, p['attn_sink'][l], layer=l, batch=batch,
                              seq=seq)
        s0 = s0[:, l]

    yc, s_fin = hgrn2_mix(proj, proj32, p['hg_lb'], p['hg_norm_w'][l], s0, layer=l, batch=batch, seq=seq)

    mixed = branch_merge(ya, yb, yc, proj, p['w_branch_a'][l], p['w_branch_b'][l], p['w_branch_c'][l],
                         tm=tm, tn=512)
    x = matmul_gated_residual(mixed, p['w_out'][l], x, mod, 2, tm=tm, tn=512, **geo)

    j = l // 2
    if l % 2 == 0:
        f = ln_mod_glu(x, p['norm2_w'][l], mod, (4, 3), p['ffn_w1'][j], p['ffn_w3'][j], tm=tm, tn=512, **geo)
        x = matmul_gated_residual(f, p['ffn_w2'][j], x, mod, 5, tm=tm, tn=512, **geo)
        if final:
            x = final_norm(x, p['final_norm_w'], tm=512)
    else:
        h2, tw, route, counts = router(x, p['norm2_w'][l], mod, (4, 3), p['router_w'][j], tm=512,
                                       layer=l, tiles_per_row=per_row * 2, row0=row0)
        pos, tile_expert, tile_ends, n_used, rows = _moe_plan(route, counts, t)
        xs = moe_dispatch(h2, pos, tile_ends, rows)
        f = gmm_up(xs, p['moe_w1'][j], p['moe_w3'][j], tile_expert, n_used, tn=512)
        ys = gmm_down(f, p['moe_w2'][j], tile_expert, n_used, tn=512)
        x = moe_combine(x, ys, pos, tw, mod, 5, p['final_norm_w'], final=final, layer=l,
                        tiles_per_row=per_row * (tm // MOE_TOKEN_TILE), row0=row0)
    k = proj32[:, OFF32_KV:OFF32_KV + KV_WIDTH]
    v = proj32[:, OFF32_KV + KV_WIDTH:OFF32_KV + 2 * KV_WIDTH]
    return x, k, v, s_fin


def kernel(x_prompt, x_sample, c, cache_k, cache_v, state_hgrn, c_ctx, ada_w, ada_b, norm1_w, norm2_w, w_in,
           hy_conv_w, hy_conv_b, hy_w1, hy_b1, hy_freq, hy_w2, hy_b2, hy_w3, hy_decay, hy_bias, attn_sink,
           hg_lb, hg_norm_w, w_branch_a, w_branch_b, w_branch_c, w_out, ffn_w1, ffn_w3, ffn_w2, router_w,
           moe_w1, moe_w3, moe_w2, final_norm_w):
    batch, seq, d = x_prompt.shape
    dbatch, dseq, _ = x_sample.shape
    depth = ada_w.shape[0]
    bf = lambda a: a.astype(BF16)
    p = dict(norm1_w=norm1_w, norm2_w=norm2_w, w_in=bf(w_in), hy_conv_w=hy_conv_w, hy_conv_b=hy_conv_b,
             hy_w1=hy_w1, hy_b1=hy_b1, hy_freq=hy_freq, hy_w2=hy_w2, hy_b2=hy_b2, hy_w3=hy_w3,
             hy_decay=hy_decay, hy_bias=hy_bias, attn_sink=attn_sink, hg_lb=hg_lb, hg_norm_w=hg_norm_w,
             w_branch_a=bf(w_branch_a), w_branch_b=bf(w_branch_b), w_branch_c=bf(w_branch_c),
             w_out=bf(w_out), ffn_w1=bf(ffn_w1), ffn_w3=bf(ffn_w3), ffn_w2=bf(ffn_w2), router_w=router_w,
             moe_w1=bf(moe_w1), moe_w3=bf(moe_w3), moe_w2=bf(moe_w2), final_norm_w=final_norm_w)

    nrows = 16
    cond = jnp.zeros((nrows, d), F32).at[:dbatch].set(c).at[dbatch].set(c_ctx)
    mod = ada_modulation(cond, ada_w, ada_b).reshape(depth, nrows, 6, 1, d)

    xp = x_prompt.reshape(batch * seq, d)
    ks, vs, ss = [], [], []
    for l in range(depth):
        xp, k_l, v_l, s_l = _trunk_layer(xp, l, p, batch=batch, seq=seq, mod=mod, row0=dbatch, ctx=None,
                                         final=(l == depth - 1))
        ks.append(k_l.reshape(batch, seq, N_KV_HEADS, HEAD_DIM))
        vs.append(v_l.reshape(batch, seq, N_KV_HEADS, HEAD_DIM))
        ss.append(s_l)
    y_prompt = xp.reshape(batch, seq, d)
    new_cache_k = jnp.stack(ks, axis=1)
    new_cache_v = jnp.stack(vs, axis=1)
    new_state = jnp.stack(ss, axis=1)

    past = cache_k.shape[2]
    ck = cache_k.reshape(dbatch, depth, past, KV_WIDTH)
    cv = cache_v.reshape(dbatch, depth, past, KV_WIDTH)
    xs = x_sample.reshape(dbatch * dseq, d)
    for l in range(depth):
        xs, _, _, _ = _trunk_layer(xs, l, p, batch=dbatch, seq=dseq, mod=mod, row0=0,
                                   ctx=(ck, cv, state_hgrn), final=(l == depth - 1))
    y_sample = xs.reshape(dbatch, dseq, d)
    return (y_prompt, y_sample, new_cache_k, new_cache_v, new_state)
```

```python
import functools
import math

import numpy as np
import jax
import jax.numpy as jnp
from jax import lax
from jax.experimental import pallas as pl
from jax.experimental.pallas import tpu as pltpu

F32 = jnp.float32
BF16 = jnp.bfloat16

VMEM_LIMIT_BYTES = 56 * 1024 * 1024
LANES = 128

EPS = 1e-6
NEG_BIG = -1e30
LB_FLOOR = 1e-30
GRID_W = 64
HY_WIDTH = 1024
HY_BANDS = 16
N_HEADS = 8
N_KV_HEADS = 2
GROUP = N_HEADS // N_KV_HEADS
HEAD_DIM = 128
ATTN_WIDTH = N_HEADS * HEAD_DIM
KV_WIDTH = N_KV_HEADS * HEAD_DIM
WINDOW = 128
ROPE_BASE = 10000.0
ATTN_SCALE = HEAD_DIM ** -0.5
HG_HEADS = 8
HG_DK = 128
HG_DV = 128
HG_WIDTH = HG_HEADS * HG_DK
HG_CHUNK = 128
N_EXPERTS = 8
TOP_K = 2

OFF_HY = 0
OFF_AQ = 3 * HY_WIDTH
OFF_AK = OFF_AQ + ATTN_WIDTH
OFF_AV = OFF_AK + KV_WIDTH
OFF_HQ = OFF_AV + KV_WIDTH
OFF_FF = OFF_HQ + HG_WIDTH
OFF_FB = OFF_FF + HG_WIDTH
OFF_HI = OFF_FB + HG_WIDTH
OFF_HG = OFF_HI + HG_WIDTH
OFF_MA = OFF_HG + HG_WIDTH


def _cparams(*sem):
    return pltpu.CompilerParams(dimension_semantics=sem, vmem_limit_bytes=VMEM_LIMIT_BYTES)


def _split3(x):
    hi = x.astype(BF16)
    r1 = x - hi.astype(F32)
    mid = r1.astype(BF16)
    lo = (r1 - mid.astype(F32)).astype(BF16)
    return hi, mid, lo


def _dot(a, b):
    return jnp.dot(a, b, preferred_element_type=F32)


def _dot_f32(a, b):
    a0, a1, a2 = _split3(a)
    b0, b1, b2 = _split3(b)
    return (_dot(a0, b0) + (_dot(a0, b1) + _dot(a1, b0))
            + (_dot(a0, b2) + _dot(a1, b1) + _dot(a2, b0)))


def _dot_f32_3pass(a, b):
    a0, a1, _ = _split3(a)
    b0, b1, _ = _split3(b)
    return _dot(a0, b0) + (_dot(a0, b1) + _dot(a1, b0))


def _silu(x):
    return x * jax.nn.sigmoid(x)


def _ada_kernel(c_ref, w_ref, b_ref, o_ref):
    o_ref[...] = _dot_f32(_silu(c_ref[...]), w_ref[...]) + b_ref[...]


def ada_modulation(cond, ada_w, ada_b, tn=512):
    depth, d, n = ada_w.shape
    rows = cond.shape[0]
    return pl.pallas_call(
        _ada_kernel,
        grid=(depth, n // tn),
        in_specs=[pl.BlockSpec((rows, d), lambda l, j: (0, 0)),
                  pl.BlockSpec((None, d, tn), lambda l, j: (l, 0, j)),
                  pl.BlockSpec((None, 1, tn), lambda l, j: (l, 0, j))],
        out_specs=pl.BlockSpec((None, rows, tn), lambda l, j: (l, 0, j)),
        out_shape=jax.ShapeDtypeStruct((depth, rows, n), F32),
        compiler_params=_cparams("parallel", "parallel"),
        name="ada_modulation",
    )(cond, ada_w, ada_b.reshape(depth, 1, n))


def _norm_modulate(x, nw, sc, sh):
    ms = jnp.mean(x * x, axis=-1, keepdims=True)
    y = x * lax.rsqrt(ms + EPS) * nw
    return y * (1.0 + sc) + sh


NORM_ROWS = 32


def _fill_norm_modulate(h_ref, x_ref, nw_ref, sc_ref, sh_ref):
    def strip(r, carry):
        rows = pl.ds(pl.multiple_of(r * NORM_ROWS, NORM_ROWS), NORM_ROWS)
        h_ref[rows, :] = _norm_modulate(x_ref[rows, :], nw_ref[...], sc_ref[...], sh_ref[...]).astype(h_ref.dtype)
        return carry

    lax.fori_loop(0, x_ref.shape[0] // NORM_ROWS, strip, 0, unroll=4)


def _mod_index(layer, j, tiles_per_row, row0):
    def index(i, *_):
        return (layer, row0 + i // tiles_per_row, j, 0, 0)
    return index


PROJ_TN = 512
F32_TILES = (OFF_AK // PROJ_TN,) + tuple(range(OFF_FF // PROJ_TN, OFF_HI // PROJ_TN))
OFF32_KV = 0
OFF32_FF = PROJ_TN
OFF32_FB = OFF32_FF + HG_WIDTH
assert OFF_AK % PROJ_TN == 0 and 2 * KV_WIDTH == PROJ_TN and OFF_FF % PROJ_TN == 0 and OFF_HI % PROJ_TN == 0


def _f32_tile_slot(j):
    return jnp.maximum(sum((j >= tile).astype(jnp.int32) for tile in F32_TILES) - 1, 0)


def _ln_mm_kernel(x_ref, nw_ref, sc_ref, sh_ref, w_ref, o_ref, o32_ref, h_ref):
    j = pl.program_id(1)

    @pl.when(j == 0)
    def _():
        _fill_norm_modulate(h_ref, x_ref, nw_ref, sc_ref, sh_ref)

    acc = _dot(h_ref[...], w_ref[...])
    o_ref[...] = acc.astype(o_ref.dtype)
    keep = functools.reduce(jnp.logical_or, [j == tile for tile in F32_TILES])

    @pl.when(keep)
    def _():
        o32_ref[...] = acc


def ln_mod_matmul(x, nw, mod, mod_idx, w, *, layer, tiles_per_row, row0, tm):
    t, d = x.shape
    n = w.shape[1]
    tn = PROJ_TN
    sc_j, sh_j = mod_idx
    mspec = lambda j: pl.BlockSpec((None, None, None, 1, d), _mod_index(layer, j, tiles_per_row, row0))
    return pl.pallas_call(
        _ln_mm_kernel,
        grid=(t // tm, n // tn),
        in_specs=[pl.BlockSpec((tm, d), lambda i, j: (i, 0)),
                  pl.BlockSpec((1, d), lambda i, j: (0, 0)),
                  mspec(sc_j), mspec(sh_j),
                  pl.BlockSpec((d, tn), lambda i, j: (0, j))],
        out_specs=[pl.BlockSpec((tm, tn), lambda i, j: (i, j)),
                   pl.BlockSpec((tm, tn), lambda i, j: (i, _f32_tile_slot(j)))],
        out_shape=[jax.ShapeDtypeStruct((t, n), BF16),
                   jax.ShapeDtypeStruct((t, len(F32_TILES) * tn), F32)],
        scratch_shapes=[pltpu.VMEM((tm, d), BF16)],
        compiler_params=_cparams("parallel", "arbitrary"),
        name="ln_mod_matmul",
    )(x, nw.reshape(1, d), mod, mod, w)


def _ln_glu_kernel(x_ref, nw_ref, sc_ref, sh_ref, w1_ref, w3_ref, o_ref, h_ref):
    @pl.when(pl.program_id(1) == 0)
    def _():
        _fill_norm_modulate(h_ref, x_ref, nw_ref, sc_ref, sh_ref)

    h = h_ref[...]
    o_ref[...] = (_silu(_dot(h, w1_ref[...])) * _dot(h, w3_ref[...])).astype(o_ref.dtype)


def ln_mod_glu(x, nw, mod, mod_idx, w1, w3, *, layer, tiles_per_row, row0, tm, tn):
    t, d = x.shape
    n = w1.shape[1]
    sc_j, sh_j = mod_idx
    mspec = lambda j: pl.BlockSpec((None, None, None, 1, d), _mod_index(layer, j, tiles_per_row, row0))
    return pl.pallas_call(
        _ln_glu_kernel,
        grid=(t // tm, n // tn),
        in_specs=[pl.BlockSpec((tm, d), lambda i, j: (i, 0)),
                  pl.BlockSpec((1, d), lambda i, j: (0, 0)),
                  mspec(sc_j), mspec(sh_j),
                  pl.BlockSpec((d, tn), lambda i, j: (0, j)),
                  pl.BlockSpec((d, tn), lambda i, j: (0, j))],
        out_specs=pl.BlockSpec((tm, tn), lambda i, j: (i, j)),
        out_shape=jax.ShapeDtypeStruct((t, n), BF16),
        scratch_shapes=[pltpu.VMEM((tm, d), BF16)],
        compiler_params=_cparams("parallel", "arbitrary"),
        name="ln_mod_glu",
    )(x, nw.reshape(1, d), mod, mod, w1, w3)


def _merge_kernel(ya_ref, yb_ref, yc_ref, ma_ref, mb_ref, mc_ref, wa_ref, wb_ref, wc_ref, o_ref):
    gate = lambda m_ref: jax.nn.sigmoid(m_ref[...].astype(F32))
    acc = gate(ma_ref) * _dot(ya_ref[...], wa_ref[...])
    acc = acc + gate(mb_ref) * _dot(yb_ref[...], wb_ref[...])
    acc = acc + gate(mc_ref) * _dot(yc_ref[...], wc_ref[...])
    o_ref[...] = acc.astype(o_ref.dtype)


def branch_merge(ya, yb, yc, proj, wa, wb, wc, *, tm, tn):
    t, k = ya.shape
    n = wa.shape[1]
    gate = lambda off: pl.BlockSpec((tm, tn), lambda i, j: (i, off // tn + j))
    yspec = pl.BlockSpec((tm, k), lambda i, j: (i, 0))
    wspec = pl.BlockSpec((k, tn), lambda i, j: (0, j))
    return pl.pallas_call(
        _merge_kernel,
        grid=(t // tm, n // tn),
        in_specs=[yspec, yspec, yspec, gate(OFF_MA), gate(OFF_MA + n), gate(OFF_MA + 2 * n),
                  wspec, wspec, wspec],
        out_specs=pl.BlockSpec((tm, tn), lambda i, j: (i, j)),
        out_shape=jax.ShapeDtypeStruct((t, n), BF16),
        compiler_params=_cparams("parallel", "parallel"),
        name="branch_merge",
    )(ya, yb, yc, proj, proj, proj, wa, wb, wc)


def _mm_resid_kernel(a_ref, w_ref, x_ref, g_ref, o_ref):
    o_ref[...] = x_ref[...] + g_ref[...] * _dot(a_ref[...], w_ref[...])


def matmul_gated_residual(a, w, x, mod, g_j, *, layer, tiles_per_row, row0, tm, tn):
    t, k = a.shape
    n = w.shape[1]
    return pl.pallas_call(
        _mm_resid_kernel,
        grid=(t // tm, n // tn),
        in_specs=[pl.BlockSpec((tm, k), lambda i, j: (i, 0)),
                  pl.BlockSpec((k, tn), lambda i, j: (0, j)),
                  pl.BlockSpec((tm, tn), lambda i, j: (i, j)),
                  pl.BlockSpec((None, None, None, 1, tn),
                               lambda i, j: (layer, row0 + i // tiles_per_row, g_j, 0, j))],
        out_specs=pl.BlockSpec((tm, tn), lambda i, j: (i, j)),
        out_shape=jax.ShapeDtypeStruct((t, n), F32),
        compiler_params=_cparams("parallel", "parallel"),
        name="matmul_gated_residual",
    )(a, w, x, mod)


def _final_norm_kernel(x_ref, fw_ref, o_ref):
    x = x_ref[...]
    o_ref[...] = x * lax.rsqrt(jnp.mean(x * x, axis=-1, keepdims=True) + EPS) * fw_ref[...]


def final_norm(x, fw, *, tm):
    t, d = x.shape
    return pl.pallas_call(
        _final_norm_kernel,
        grid=(t // tm,),
        in_specs=[pl.BlockSpec((tm, d), lambda i: (i, 0)), pl.BlockSpec((1, d), lambda i: (0, 0))],
        out_specs=pl.BlockSpec((tm, d), lambda i: (i, 0)),
        out_shape=jax.ShapeDtypeStruct((t, d), F32),
        compiler_params=_cparams("parallel"),
        name="final_norm",
    )(x, fw.reshape(1, d))


MOE_ROW_TILE = 512
MOE_TOKEN_TILE = 256


_HIGH_HALF = 0xFFFF0000


def _pack_bf16_pairs(x):
    bits = lax.bitcast_convert_type(x.astype(BF16).astype(F32), jnp.uint32)
    half = x.shape[1] // 2
    return (bits[:, :half] & jnp.uint32(_HIGH_HALF)) | (bits[:, half:] >> 16)


def _unpack_bf16_pairs(w):
    hi = lax.bitcast_convert_type(w & jnp.uint32(_HIGH_HALF), F32).astype(BF16)
    lo = lax.bitcast_convert_type(w << 16, F32).astype(BF16)
    return jnp.concatenate([hi, lo], axis=1)


def _router_kernel(x_ref, nw_ref, sc_ref, sh_ref, rw_ref, h_ref, tw_ref, route_ref, cnt_ref, run_ref):
    @pl.when(pl.program_id(0) == 0)
    def _():
        run_ref[...] = jnp.zeros_like(run_ref)

    h = _norm_modulate(x_ref[...], nw_ref[...], sc_ref[...], sh_ref[...])
    h_ref[...] = _pack_bf16_pairs(h)
    tm = h.shape[0]
    lane = lax.broadcasted_iota(jnp.int32, (tm, LANES), 1)
    logits = jnp.where(lane < N_EXPERTS, _dot_f32(h, rw_ref[...]), -jnp.inf)
    m1 = jnp.max(logits, axis=-1, keepdims=True)
    i1 = jnp.min(jnp.where(logits == m1, lane, LANES), axis=-1, keepdims=True)
    rest = jnp.where(lane == i1, -jnp.inf, logits)
    m2 = jnp.max(rest, axis=-1, keepdims=True)
    i2 = jnp.min(jnp.where(rest == m2, lane, LANES), axis=-1, keepdims=True)
    e2 = jnp.exp(m2 - m1)
    inv = 1.0 / (1.0 + e2)
    tw_ref[...] = jnp.where(lane == 0, inv, jnp.where(lane == 1, e2 * inv, 0.0))

    sel = jnp.where((lane == i1) | (lane == i2), 1.0, 0.0)
    before = lax.broadcasted_iota(jnp.int32, (tm, tm), 1) < lax.broadcasted_iota(jnp.int32, (tm, tm), 0)
    rank = _dot(jnp.where(before, 1.0, 0.0).astype(BF16), sel.astype(BF16)) + run_ref[0:1, :]
    r1 = jnp.sum(jnp.where(lane == i1, rank, 0.0), axis=-1, keepdims=True).astype(jnp.int32)
    r2 = jnp.sum(jnp.where(lane == i2, rank, 0.0), axis=-1, keepdims=True).astype(jnp.int32)
    route_ref[...] = jnp.where(lane == 0, i1, jnp.where(lane == 1, i2, jnp.where(lane == 2, r1, r2)))
    run_ref[...] = run_ref[...] + jnp.sum(sel, axis=0, keepdims=True)
    cnt_ref[...] = run_ref[...]


def router(x, nw, mod, mod_idx, rw, *, layer, tiles_per_row, row0, tm):
    t, d = x.shape
    sc_j, sh_j = mod_idx
    rw_pad = jnp.zeros((d, LANES), F32).at[:, :N_EXPERTS].set(rw)
    mspec = lambda j: pl.BlockSpec((None, None, None, 1, d), _mod_index(layer, j, tiles_per_row, row0))
    return pl.pallas_call(
        _router_kernel,
        grid=(t // tm,),
        in_specs=[pl.BlockSpec((tm, d), lambda i: (i, 0)),
                  pl.BlockSpec((1, d), lambda i: (0, 0)),
                  mspec(sc_j), mspec(sh_j),
                  pl.BlockSpec((d, LANES), lambda i: (0, 0))],
        out_specs=[pl.BlockSpec((tm, d // 2), lambda i: (i, 0)),
                   pl.BlockSpec((tm, LANES), lambda i: (i, 0)),
                   pl.BlockSpec((tm, LANES), lambda i: (i, 0)),
                   pl.BlockSpec((8, LANES), lambda i: (0, 0))],
        out_shape=[jax.ShapeDtypeStruct((t, d // 2), jnp.uint32), jax.ShapeDtypeStruct((t, LANES), F32),
                   jax.ShapeDtypeStruct((t, LANES), jnp.int32), jax.ShapeDtypeStruct((8, LANES), F32)],
        scratch_shapes=[pltpu.VMEM((8, LANES), F32)],
        compiler_params=_cparams("arbitrary"),
        name="router",
    )(x, nw.reshape(1, d), mod, mod, rw_pad)


def _row_copy(src, src_row, dst, dst_row, sem):
    return pltpu.make_async_copy(src.at[pl.ds(src_row, 1), :], dst.at[pl.ds(dst_row, 1), :], sem)


def _dispatch_kernel(pos_ref, ends_ref, h_ref, xs_ref, zero_ref, sem, zsem):
    nt = h_ref.shape[0]
    tm = zero_ref.shape[0]

    @pl.when(pl.program_id(0) == 0)
    def _():
        zero_ref[...] = jnp.zeros_like(zero_ref)

        def last_tile_copy(e):
            row0 = pl.multiple_of((ends_ref[e] - 1) * tm, tm)
            return pltpu.make_async_copy(zero_ref, xs_ref.at[pl.ds(row0, tm), :], zsem)

        def has_rows(e):
            return ends_ref[e] > (ends_ref[e - 1] if e else 0)

        n_tiles = xs_ref.shape[0] // tm

        def spare_tile_copy(k):
            row0 = pl.multiple_of((ends_ref[N_EXPERTS - 1] + k) * tm, tm)
            return pltpu.make_async_copy(zero_ref, xs_ref.at[pl.ds(row0, tm), :], zsem)

        def is_spare(k):
            return ends_ref[N_EXPERTS - 1] + k < n_tiles

        for e in range(N_EXPERTS):
            @pl.when(has_rows(e))
            def _():
                last_tile_copy(e).start()

            @pl.when(is_spare(e))
            def _():
                spare_tile_copy(e).start()
        for e in range(N_EXPERTS):
            @pl.when(has_rows(e))
            def _():
                last_tile_copy(e).wait()

            @pl.when(is_spare(e))
            def _():
                spare_tile_copy(e).wait()

    def start(r, carry):
        for k in range(TOP_K):
            _row_copy(h_ref, r, xs_ref, pos_ref[0, k * nt + r], sem).start()
        return carry

    def wait(r, carry):
        for k in range(TOP_K):
            _row_copy(h_ref, r, xs_ref, pos_ref[0, k * nt + r], sem).wait()
        return carry

    lax.fori_loop(0, nt, start, 0, unroll=8)
    lax.fori_loop(0, nt, wait, 0, unroll=8)


def moe_dispatch(h, pos, tile_ends, rows):
    t, d = h.shape
    nt = MOE_TOKEN_TILE
    return pl.pallas_call(
        _dispatch_kernel,
        grid=(t // nt,),
        in_specs=[pl.BlockSpec((None, 1, TOP_K * nt), lambda i: (i, 0, 0), memory_space=pltpu.SMEM),
                  pl.BlockSpec(memory_space=pltpu.SMEM),
                  pl.BlockSpec((nt, d), lambda i: (i, 0))],
        out_specs=pl.BlockSpec(memory_space=pl.ANY),
        out_shape=jax.ShapeDtypeStruct((rows, d), h.dtype),
        scratch_shapes=[pltpu.VMEM((MOE_ROW_TILE, d), h.dtype), pltpu.SemaphoreType.DMA(()),
                        pltpu.SemaphoreType.DMA(())],
        compiler_params=_cparams("arbitrary"),
        name="moe_dispatch",
    )(pos, tile_ends, h)


def _new_weight_tile(te_ref, i):
    return jnp.logical_or(i == 0, te_ref[i] != te_ref[jnp.maximum(i - 1, 0)])


def _gmm_up_kernel(te_ref, nu_ref, xs_ref, w1_ref, w3_ref, o_ref, w1b_ref, w3b_ref):
    i = pl.program_id(1)
    used = i < nu_ref[0]

    @pl.when(_new_weight_tile(te_ref, i))
    def _():
        w1b_ref[...] = w1_ref[...].astype(BF16)
        w3b_ref[...] = w3_ref[...].astype(BF16)

    @pl.when(used)
    def _():
        a = _unpack_bf16_pairs(xs_ref[...])
        o_ref[...] = (_silu(_dot(a, w1b_ref[...])) * _dot(a, w3b_ref[...])).astype(o_ref.dtype)

    @pl.when(jnp.logical_not(used))
    def _():
        o_ref[...] = jnp.zeros_like(o_ref)


def gmm_up(xs, w1, w3, tile_expert, n_used, *, tn):
    rows = xs.shape[0]
    _, d, n = w1.shape
    tm = MOE_ROW_TILE
    wspec = pl.BlockSpec((None, d, tn), lambda j, i, te, nu: (te[i], 0, j))
    return pl.pallas_call(
        _gmm_up_kernel,
        grid_spec=pltpu.PrefetchScalarGridSpec(
            num_scalar_prefetch=2,
            grid=(n // tn, rows // tm),
            in_specs=[pl.BlockSpec((tm, d // 2), lambda j, i, te, nu: (jnp.minimum(i, nu[0] - 1), 0)),
                      wspec, wspec],
            out_specs=pl.BlockSpec((tm, tn), lambda j, i, te, nu: (i, j)),
            scratch_shapes=[pltpu.VMEM((d, tn), BF16), pltpu.VMEM((d, tn), BF16)]),
        out_shape=jax.ShapeDtypeStruct((rows, n), BF16),
        compiler_params=_cparams("arbitrary", "arbitrary"),
        name="gmm_up",
    )(tile_expert, n_used, xs, w1, w3)


def _gmm_down_kernel(te_ref, nu_ref, f_ref, w_ref, o_ref, wb_ref):
    i = pl.program_id(1)
    used = i < nu_ref[0]

    @pl.when(_new_weight_tile(te_ref, i))
    def _():
        wb_ref[...] = w_ref[...].astype(BF16)

    @pl.when(used)
    def _():
        o_ref[...] = _dot(f_ref[...], wb_ref[...])

    @pl.when(jnp.logical_not(used))
    def _():
        o_ref[...] = jnp.zeros_like(o_ref)


def gmm_down(f, w2, tile_expert, n_used, *, tn):
    rows, k = f.shape
    n = w2.shape[2]
    tm = MOE_ROW_TILE
    return pl.pallas_call(
        _gmm_down_kernel,
        grid_spec=pltpu.PrefetchScalarGridSpec(
            num_scalar_prefetch=2,
            grid=(n // tn, rows // tm),
            in_specs=[pl.BlockSpec((tm, k), lambda j, i, te, nu: (jnp.minimum(i, nu[0] - 1), 0)),
                      pl.BlockSpec((None, k, tn), lambda j, i, te, nu: (te[i], 0, j))],
            out_specs=pl.BlockSpec((tm, tn), lambda j, i, te, nu: (i, j)),
            scratch_shapes=[pltpu.VMEM((k, tn), BF16)]),
        out_shape=jax.ShapeDtypeStruct((rows, n), F32),
        compiler_params=_cparams("arbitrary", "arbitrary"),
        name="gmm_down",
    )(tile_expert, n_used, f, w2)


def _combine_kernel(final, pos_ref, x_ref, tw_ref, g_ref, fw_ref, ys_ref, o_ref, y0_ref, y1_ref, sem):
    nt = x_ref.shape[0]
    bufs = (y0_ref, y1_ref)

    def start(r, carry):
        for k in range(TOP_K):
            _row_copy(ys_ref, pos_ref[0, k * nt + r], bufs[k], r, sem).start()
        return carry

    def wait(r, carry):
        for k in range(TOP_K):
            _row_copy(ys_ref, pos_ref[0, k * nt + r], bufs[k], r, sem).wait()
        return carry

    lax.fori_loop(0, nt, start, 0, unroll=8)
    lax.fori_loop(0, nt, wait, 0, unroll=8)
    tw = tw_ref[...]
    x = x_ref[...] + g_ref[...] * (tw[:, 0:1] * y0_ref[...] + tw[:, 1:2] * y1_ref[...])
    if final:
        x = x * lax.rsqrt(jnp.mean(x * x, axis=-1, keepdims=True) + EPS) * fw_ref[...]
    o_ref[...] = x


def moe_combine(x, ys, pos, tw, mod, g_j, fw, *, layer, tiles_per_row, row0, final):
    t, d = x.shape
    nt = MOE_TOKEN_TILE
    return pl.pallas_call(
        functools.partial(_combine_kernel, final),
        grid=(t // nt,),
        in_specs=[pl.BlockSpec((None, 1, TOP_K * nt), lambda i: (i, 0, 0), memory_space=pltpu.SMEM),
                  pl.BlockSpec((nt, d), lambda i: (i, 0)),
                  pl.BlockSpec((nt, LANES), lambda i: (i, 0)),
                  pl.BlockSpec((None, None, None, 1, d), _mod_index(layer, g_j, tiles_per_row, row0)),
                  pl.BlockSpec((1, d), lambda i: (0, 0)),
                  pl.BlockSpec(memory_space=pl.ANY)],
        out_specs=pl.BlockSpec((nt, d), lambda i: (i, 0)),
        out_shape=jax.ShapeDtypeStruct((t, d), F32),
        scratch_shapes=[pltpu.VMEM((nt, d), F32), pltpu.VMEM((nt, d), F32), pltpu.SemaphoreType.DMA(())],
        compiler_params=_cparams("arbitrary"),
        name="moe_combine",
    )(pos, x, tw, mod, fw.reshape(1, d), ys)


def _moe_plan(route, counts, t):
    tm = MOE_ROW_TILE
    nt = MOE_TOKEN_TILE
    n_tiles = TOP_K * t // tm + N_EXPERTS
    cnt = counts[0, :N_EXPERTS].astype(jnp.int32)
    tiles = (cnt + tm - 1) // tm
    ends = jnp.cumsum(tiles)
    offs = (ends - tiles) * tm
    pos = [jnp.take(offs, route[:, k]) + route[:, TOP_K + k] for k in range(TOP_K)]
    pos = jnp.concatenate([p.reshape(t // nt, nt) for p in pos], axis=1).reshape(t // nt, 1, TOP_K * nt)
    tile_expert = jnp.sum(jnp.arange(n_tiles, dtype=jnp.int32)[:, None] >= ends[None, :], axis=1)
    tile_expert = jnp.minimum(tile_expert, N_EXPERTS - 1).astype(jnp.int32)
    ends = ends.astype(jnp.int32)
    return pos, tile_expert, ends, ends[-1:], n_tiles * tm


def _hyena_filter_kernel(z_ref, w1_ref, b1_ref, fr_ref, w2_ref, b2_ref, w3_ref, t_ref, dec_ref, o_ref):
    fr = fr_ref[...]
    h = jnp.sin(fr * (_dot_f32(z_ref[...], w1_ref[...]) + b1_ref[...]))
    h = jnp.sin(fr * (_dot_f32(h, w2_ref[...]) + b2_ref[...]))
    o_ref[...] = _dot_f32(h, w3_ref[...]) * jnp.exp(-t_ref[...] * jnp.abs(dec_ref[...]))


def hyena_filter(seq, w1, b1, freq, w2, b2, w3, decay, tl=256):
    t = jnp.linspace(0.0, 1.0, seq, dtype=F32)[:, None]
    pos = jnp.arange(seq, dtype=F32)[:, None]
    bands = jnp.linspace(1e-4, HY_BANDS - 1.0, HY_BANDS, dtype=F32)[None, :]
    ang = (2.0 * math.pi / seq) * pos * bands
    z = jnp.concatenate([t, jnp.cos(ang), -jnp.sin(ang)], axis=-1)
    emb = z.shape[1]
    emb_pad = LANES
    z = jnp.pad(z, ((0, 0), (0, emb_pad - emb)))
    w1p = jnp.pad(w1, ((0, emb_pad - emb), (0, 0)))
    hid = w1.shape[1]
    n = w3.shape[1]
    tl = min(tl, seq)
    full = lambda shape: pl.BlockSpec(shape, lambda i: (0, 0))
    return pl.pallas_call(
        _hyena_filter_kernel,
        grid=(seq // tl,),
        in_specs=[pl.BlockSpec((tl, emb_pad), lambda i: (i, 0)),
                  full((emb_pad, hid)), full((1, hid)), full((1, hid)),
                  full((hid, hid)), full((1, hid)), full((hid, n)),
                  pl.BlockSpec((tl, 1), lambda i: (i, 0)), full((1, n))],
        out_specs=pl.BlockSpec((tl, n), lambda i: (i, 0)),
        out_shape=jax.ShapeDtypeStruct((seq, n), F32),
        compiler_params=_cparams("parallel"),
        name="hyena_filter",
    )(z, w1p, b1.reshape(1, hid), freq.reshape(1, hid), w2, b2.reshape(1, hid), w3, t,
      decay.reshape(1, n))


def _dft_tables(seq, kb):
    n = 2 * seq
    k = jnp.arange(seq, dtype=jnp.int32)[:, None]
    s = jnp.arange(seq, dtype=jnp.int32)[None, :]
    ang = ((k * s) % n).astype(F32) * (2.0 * math.pi / n)
    cos = jnp.cos(ang)
    sin = jnp.sin(ang)
    nyq = jnp.where(s % 2 == 0, 1.0, -1.0).astype(F32)
    is0 = k == 0
    f_re = cos
    f_im = jnp.where(is0, nyq, -sin)
    i_re = jnp.where(is0, 1.0 / n, (2.0 / n) * cos)
    i_im = jnp.where(is0, nyq / n, -(2.0 / n) * sin)
    nkb = seq // kb
    fwd = jnp.concatenate([f_re.reshape(nkb, kb, seq), f_im.reshape(nkb, kb, seq)], axis=1)
    inv = jnp.concatenate([i_re.reshape(nkb, kb, seq), i_im.reshape(nkb, kb, seq)], axis=1)
    return fwd, jnp.swapaxes(inv, 1, 2)


def _spectrum_kernel(kb, f_ref, hf_ref, hb_ref, bias_ref, o_ref):
    f = f_ref[...]
    row = lax.broadcasted_iota(jnp.int32, hb_ref.shape, 0)
    hb0 = jnp.where(row == 0, 0.0, hb_ref[...])
    a = _dot_f32_3pass(f, hf_ref[...])
    b = _dot_f32_3pass(f, hb0)
    orow = lax.broadcasted_iota(jnp.int32, a.shape, 0)
    nyq_slot = (orow == kb) & (pl.program_id(0) == 0)
    o_ref[...] = jnp.where((orow < kb) | nyq_slot, a + b + bias_ref[...], a - b)


def hyena_spectrum(fwd_f32, filt, bias, kb, tc=256):
    nkb, kb2, seq = fwd_f32.shape
    c = filt.shape[1] // 2
    return pl.pallas_call(
        functools.partial(_spectrum_kernel, kb),
        grid=(nkb, c // tc),
        in_specs=[pl.BlockSpec((None, kb2, seq), lambda j, i: (j, 0, 0)),
                  pl.BlockSpec((seq, tc), lambda j, i: (0, i)),
                  pl.BlockSpec((seq, tc), lambda j, i: (0, c // tc + i)),
                  pl.BlockSpec((1, tc), lambda j, i: (0, i))],
        out_specs=pl.BlockSpec((None, kb2, tc), lambda j, i: (j, 0, i)),
        out_shape=jax.ShapeDtypeStruct((nkb, kb2, c), F32),
        compiler_params=_cparams("parallel", "parallel"),
        name="hyena_spectrum",
    )(fwd_f32, filt, filt, bias.reshape(1, c))


def _hyena_conv_kernel(kb, x0_ref, x1_ref, v_ref, cw0_ref, cw1_ref, cwv_ref, cb0_ref, cb1_ref, cbv_ref,
                       f_ref, i_ref, kf_ref, o_ref, u16_ref, acc_ref):
    j = pl.program_id(2)
    seq = x0_ref.shape[0]

    def conv3(x_ref, w_ref, b_ref):
        x = x_ref[...].astype(F32)
        row = lax.broadcasted_iota(jnp.int32, x.shape, 0)
        prev = jnp.where(row == 0, 0.0, pltpu.roll(x, 1, 0))
        nxt = jnp.where(row == seq - 1, 0.0, pltpu.roll(x, seq - 1, 0))
        w = w_ref[...]
        return prev * w[0:1, :] + x * w[1:2, :] + nxt * w[2:3, :] + b_ref[...]

    @pl.when(j == 0)
    def _():
        u16_ref[...] = (conv3(v_ref, cwv_ref, cbv_ref) * conv3(x1_ref, cw1_ref, cb1_ref)).astype(BF16)
        acc_ref[...] = jnp.zeros_like(acc_ref)

    spec = _dot(f_ref[...], u16_ref[...])
    xr, xi = spec[:kb], spec[kb:]
    kf = kf_ref[...]
    kr, ki = kf[:kb], kf[kb:]
    packed = (lax.broadcasted_iota(jnp.int32, xr.shape, 0) == 0) & (j == 0)
    yr = xr * kr - jnp.where(packed, 0.0, xi * ki)
    yi = jnp.where(packed, xi * ki, xr * ki + xi * kr)
    y = jnp.concatenate([yr, yi], axis=0).astype(BF16)
    acc_ref[...] += _dot(i_ref[...], y)

    @pl.when(j == pl.num_programs(2) - 1)
    def _():
        o_ref[...] = (acc_ref[...] * conv3(x0_ref, cw0_ref, cb0_ref)).astype(o_ref.dtype)


def hyena_conv(proj, conv_w, conv_b, fwd, inv, spectrum, *, batch, seq, ct, kb):
    c = HY_WIDTH
    nkb = seq // kb
    ncb = c // ct
    col = lambda part: pl.BlockSpec((seq, ct), lambda b, i, j: (b, part * ncb + i))
    cw = lambda part: pl.BlockSpec((3, ct), lambda b, i, j: (0, part * ncb + i))
    cb = lambda part: pl.BlockSpec((1, ct), lambda b, i, j: (0, part * ncb + i))
    conv_b = conv_b.reshape(1, 3 * c)
    return pl.pallas_call(
        functools.partial(_hyena_conv_kernel, kb),
        grid=(batch, ncb, nkb),
        in_specs=[col(0), col(1), col(2), cw(0), cw(1), cw(2), cb(0), cb(1), cb(2),
                  pl.BlockSpec((None, 2 * kb, seq), lambda b, i, j: (j, 0, 0)),
                  pl.BlockSpec((None, seq, 2 * kb), lambda b, i, j: (j, 0, 0)),
                  pl.BlockSpec((None, 2 * kb, ct), lambda b, i, j: (j, 0, i))],
        out_specs=pl.BlockSpec((seq, ct), lambda b, i, j: (b, i)),
        out_shape=jax.ShapeDtypeStruct((batch * seq, c), BF16),
        scratch_shapes=[pltpu.VMEM((seq, ct), BF16), pltpu.VMEM((seq, ct), F32)],
        compiler_params=_cparams("parallel", "parallel", "arbitrary"),
        name="hyena_conv",
    )(proj, proj, proj, conv_w, conv_w, conv_w, conv_b, conv_b, conv_b, fwd, inv, spectrum)


def _rope(x, cos, sin_a, sin_b):
    return x * cos + pltpu.roll(x, HEAD_DIM - HEAD_DIM // 4, 1) * sin_a + pltpu.roll(x, HEAD_DIM // 4, 1) * sin_b


def _dot_nt(a, b):
    return lax.dot_general(a, b, (((1,), (1,)), ((), ())), preferred_element_type=F32)


def _grouped_softmax_pv(qs, keys, values, sinks, valid, o_ref):
    scores = [_dot_nt(q, keys) * ATTN_SCALE for q in qs]
    probs, dens = [], []
    for s, sink in zip(scores, sinks):
        if valid is not None:
            s = jnp.where(valid, s, NEG_BIG)
        m = jnp.maximum(jnp.max(s, axis=-1, keepdims=True), sink)
        p = jnp.exp(s - m)
        dens.append(jnp.sum(p, axis=-1, keepdims=True) + jnp.exp(sink - m))
        probs.append(p.astype(BF16))
    outs = [_dot(p, values) for p in probs]
    for g, (o, den) in enumerate(zip(outs, dens)):
        o_ref[:, g * HEAD_DIM:(g + 1) * HEAD_DIM] = (o / den).astype(o_ref.dtype)


def _ctx_attn_kernel(sink_ref, q_ref, k_ref, v_ref, o_ref):
    kvh = pl.program_id(1)
    qs = [q_ref[:, g * HEAD_DIM:(g + 1) * HEAD_DIM] for g in range(GROUP)]
    sinks = [sink_ref[kvh * GROUP + g] for g in range(GROUP)]
    _grouped_softmax_pv(qs, k_ref[...].astype(BF16), v_ref[...].astype(BF16), sinks, None, o_ref)


def context_attention(proj, proj32, sink, *, batch, seq):
    qw = GROUP * HEAD_DIM
    kv = lambda part: pl.BlockSpec((seq, HEAD_DIM),
                                   lambda b, h: (b, OFF32_KV // HEAD_DIM + part * N_KV_HEADS + h))
    return pl.pallas_call(
        _ctx_attn_kernel,
        grid=(batch, N_KV_HEADS),
        in_specs=[pl.BlockSpec(memory_space=pltpu.SMEM),
                  pl.BlockSpec((seq, qw), lambda b, h: (b, OFF_AQ // qw + h)),
                  kv(0), kv(1)],
        out_specs=pl.BlockSpec((seq, qw), lambda b, h: (b, h)),
        out_shape=jax.ShapeDtypeStruct((batch * seq, ATTN_WIDTH), BF16),
        compiler_params=_cparams("parallel", "parallel"),
        name="context_attention",
    )(sink, proj, proj32, proj32)


def _lat_attn_kernel(seq, sink_ref, q_ref, k_ref, v_ref, ck_ref, cv_ref, cos_ref, sa_ref, sb_ref,
                     o_ref, kr_ref, vb_ref):
    kvh = pl.program_id(1)
    qb = pl.program_id(2)
    blk = q_ref.shape[0]
    nwin = 3 * blk
    past = ck_ref.shape[0]

    @pl.when(qb == 0)
    def _():
        kr_ref[...] = _rope(k_ref[...], cos_ref[...], sa_ref[...], sb_ref[...]).astype(BF16)
        vb_ref[...] = v_ref[...].astype(BF16)

    start = pl.multiple_of(jnp.clip((qb - 1) * blk, 0, seq - nwin), blk)
    keys = jnp.concatenate([kr_ref[pl.ds(start, nwin), :], ck_ref[...].astype(BF16)], axis=0)
    values = jnp.concatenate([vb_ref[pl.ds(start, nwin), :], cv_ref[...].astype(BF16)], axis=0)
    rows = pl.ds(pl.multiple_of(qb * blk, blk), blk)
    cos, sa, sb = cos_ref[rows, :], sa_ref[rows, :], sb_ref[rows, :]
    qpos = qb * blk + lax.broadcasted_iota(jnp.int32, (blk, nwin + past), 0)
    col = lax.broadcasted_iota(jnp.int32, (blk, nwin + past), 1)
    valid = (col >= nwin) | (jnp.abs(qpos - (start + col)) <= WINDOW)
    qs = [_rope(q_ref[:, g * HEAD_DIM:(g + 1) * HEAD_DIM].astype(F32), cos, sa, sb).astype(BF16)
          for g in range(GROUP)]
    sinks = [sink_ref[kvh * GROUP + g] for g in range(GROUP)]
    _grouped_softmax_pv(qs, keys, values, sinks, valid, o_ref)


def _rope_tables(seq):
    rows = seq // GRID_W
    row = jnp.repeat(jnp.arange(rows, dtype=F32), GRID_W)
    col = jnp.tile(jnp.arange(GRID_W, dtype=F32), rows)
    quarter = HEAD_DIM // 4
    inv = ROPE_BASE ** (-jnp.arange(quarter, dtype=F32) / quarter)
    ar = row[:, None] * inv
    ac = col[:, None] * inv
    ang = jnp.concatenate([ar, ar, ac, ac], axis=-1)
    cos, sin = jnp.cos(ang), jnp.sin(ang)
    first = (jnp.arange(HEAD_DIM) % (2 * quarter)) < quarter
    return cos, jnp.where(first, -sin, 0.0), jnp.where(first, 0.0, sin)


def latent_attention(proj, proj32, cache_k, cache_v, sink, *, layer, batch, seq, blk=128):
    qw = GROUP * HEAD_DIM
    past = cache_k.shape[2]
    nqb = seq // blk
    cos, sa, sb = _rope_tables(seq)
    table = pl.BlockSpec((seq, HEAD_DIM), lambda b, h, i: (0, 0))
    cache = pl.BlockSpec((None, None, past, HEAD_DIM), lambda b, h, i: (b, layer, 0, h))
    kv = lambda part: pl.BlockSpec((seq, HEAD_DIM),
                                   lambda b, h, i: (b, OFF32_KV // HEAD_DIM + part * N_KV_HEADS + h))
    return pl.pallas_call(
        functools.partial(_lat_attn_kernel, seq),
        grid=(batch, N_KV_HEADS, nqb),
        in_specs=[pl.BlockSpec(memory_space=pltpu.SMEM),
                  pl.BlockSpec((blk, qw), lambda b, h, i: (b * nqb + i, OFF_AQ // qw + h)),
                  kv(0), kv(1),
                  cache, cache, table, table, table],
        out_specs=pl.BlockSpec((blk, qw), lambda b, h, i: (b * nqb + i, h)),
        out_shape=jax.ShapeDtypeStruct((batch * seq, ATTN_WIDTH), BF16),
        scratch_shapes=[pltpu.VMEM((seq, HEAD_DIM), BF16), pltpu.VMEM((seq, HEAD_DIM), BF16)],
        compiler_params=_cparams("parallel", "parallel", "arbitrary"),
        name="latent_attention",
    )(sink, proj, proj32, proj32, cache_k, cache_v, cos, sa, sb)


_HG_LEVELS = tuple(HG_CHUNK >> (i + 1) for i in range(int(math.log2(HG_CHUNK))))


def _hgrn_tables():
    c = HG_CHUNK
    t = np.arange(c)[:, None]
    u = np.arange(c)[None, :]
    blocks = [(u <= t), (u > t)]
    for m in _HG_LEVELS:
        ref = (t // (2 * m)) * (2 * m) + m - 1
        second = (t % (2 * m)) >= m
        blocks.append(np.where(second, (u > ref) & (u <= t), (u > t) & (u <= ref)))
    fwd = np.concatenate(blocks, axis=0).astype(np.float32)
    bwd = np.concatenate([b[::-1, ::-1] for b in blocks], axis=0).astype(np.float32)
    s = u
    level = np.full((c, c), len(_HG_LEVELS) + 1, np.int32)
    level[t == s] = len(_HG_LEVELS)
    for i, m in enumerate(_HG_LEVELS):
        hit = (t // (2 * m) == s // (2 * m)) & ((t % (2 * m)) >= m) & ((s % (2 * m)) < m)
        level[hit] = i
    a = np.stack([fwd, bwd])
    a = np.concatenate([a, a], axis=2)
    lv = np.stack([level, level.T])
    return jnp.asarray(a, BF16), jnp.asarray(lv, jnp.int32)


def _hgrn_kernel(layer, has_s0, seq, heads, *refs):
    if has_s0:
        (q_ref, ff_ref, fb_ref, i_ref, g_ref, lb_ref, nw_ref, a_ref, lv_ref, s0_ref,
         y_ref, sfin_ref, of_ref, ob_ref, st_ref) = refs
    else:
        (q_ref, ff_ref, fb_ref, i_ref, g_ref, lb_ref, nw_ref, a_ref, lv_ref,
         y_ref, sfin_ref, of_ref, ob_ref, st_ref) = refs
    c = HG_CHUNK
    nlev = len(_HG_LEVELS)
    nc = seq // c

    lbs = lb_ref[...]
    mx = jnp.max(lbs, axis=0, keepdims=True)
    ex = jnp.exp(lbs - mx)
    sm = ex / jnp.sum(ex, axis=0, keepdims=True)
    lb = jnp.zeros(sm.shape[1:], F32)
    for j in range(1, layer + 1):
        lb = lb + sm[j]

    for d in range(2):
        for h in range(heads):
            if has_s0:
                st_ref[d, h] = s0_ref[d, h].T
            else:
                st_ref[d, h] = jnp.zeros((HG_DV, HG_DK), F32)

    def decay_exponents(d, ci):
        rows = pl.ds(pl.multiple_of(ci * c, c), c)
        fpre = (ff_ref if d == 0 else fb_ref)[rows, :]
        lbd = lb[d:d + 1, :]
        f = jnp.maximum(lbd, LB_FLOOR) + (1.0 - lbd) * jax.nn.sigmoid(fpre)
        log2f = jnp.log(f) * (1.0 / math.log(2.0))
        args = _dot(a_ref[d], jnp.concatenate(_split3(log2f)[:2], axis=0))
        chains = []
        for h in range(heads):
            cols = slice(h * HG_DK, (h + 1) * HG_DK)
            q = _silu(q_ref[rows, cols].astype(F32)) * (HG_DK ** -0.5)
            chains.append(dict(d=d, h=h, rows=rows, cols=cols, q=q, k=1.0 - f[:, cols],
                               v=i_ref[rows, cols].astype(BF16), args=args[:, cols]))
        return chains

    def level_products(s):
        q, k = s['q'], s['k']
        e = jnp.exp2(s['args'])
        s['q_in'] = (q * e[0:c]).astype(BF16)
        s['k_out'] = (k * e[c:2 * c]).astype(BF16)
        last = (c - 1) if s['d'] == 0 else 0
        s['total'] = e[last:last + 1]
        prods = [_dot_nt((q * e[(2 + i) * c:(3 + i) * c]).astype(BF16), (k * e[(2 + i) * c:(3 + i) * c]).astype(BF16))
                 for i in range(nlev)]
        s['prods'] = prods + [_dot_nt(q.astype(BF16), k.astype(BF16))]
        return s

    def outputs_and_state(s, o_ref):
        d, h = s['d'], s['h']
        lv = lv_ref[d]
        att = jnp.zeros((c, c), F32)
        for i in range(nlev + 1):
            att = jnp.where(lv == i, s['prods'][i], att)
        st = st_ref[d, h]
        o_ref[s['rows'], s['cols']] = _dot_nt(s['q_in'], st.astype(BF16)) + _dot(att.astype(BF16), s['v'])
        upd = lax.dot_general(s['v'], s['k_out'], (((0,), (0,)), ((), ())), preferred_element_type=F32)
        st_ref[d, h] = st * s['total'] + upd

    def body(ci, carry):
        stage = decay_exponents(0, ci) + decay_exponents(1, nc - 1 - ci)
        stage = [level_products(s) for s in stage]
        for s in stage:
            outputs_and_state(s, of_ref if s['d'] == 0 else ob_ref)
        return carry

    lax.fori_loop(0, nc, body, 0)

    for h in range(heads):
        cols = slice(h * HG_DK, (h + 1) * HG_DK)
        o = of_ref[:, cols] + ob_ref[:, cols]
        o = o * lax.rsqrt(jnp.mean(o * o, axis=-1, keepdims=True) + EPS) * nw_ref[...] * _silu(g_ref[:, cols].astype(F32))
        y_ref[:, cols] = o.astype(y_ref.dtype)
        for d in range(2):
            sfin_ref[d, h] = st_ref[d, h].T


HG_HEADS_PER_STEP = 2


def hgrn2_mix(proj, proj32, hg_lb, norm_w, s0, *, layer, batch, seq):
    a_tab, lv_tab = _hgrn_tables()
    depth = hg_lb.shape[0]
    hps = HG_HEADS_PER_STEP
    w = hps * HG_DK
    col = lambda off: pl.BlockSpec((seq, w), lambda b, h: (b, off // w + h))
    state = pl.BlockSpec((None, 2, hps, HG_DK, HG_DV), lambda b, h: (b, 0, h, 0, 0))
    in_specs = [col(OFF_HQ), col(OFF32_FF), col(OFF32_FB), col(OFF_HI), col(OFF_HG),
                pl.BlockSpec((depth, 2, w), lambda b, h: (0, 0, h)),
                pl.BlockSpec((1, HG_DV), lambda b, h: (0, 0)),
                pl.BlockSpec(a_tab.shape, lambda b, h: (0, 0, 0)),
                pl.BlockSpec(lv_tab.shape, lambda b, h: (0, 0, 0))]
    args = [proj, proj32, proj32, proj, proj, hg_lb, norm_w.reshape(1, HG_DV), a_tab, lv_tab]
    if s0 is not None:
        in_specs.append(state)
        args.append(s0)
    return pl.pallas_call(
        functools.partial(_hgrn_kernel, layer, s0 is not None, seq, hps),
        grid=(batch, HG_HEADS // hps),
        in_specs=in_specs,
        out_specs=[pl.BlockSpec((seq, w), lambda b, h: (b, h)), state],
        out_shape=[jax.ShapeDtypeStruct((batch * seq, HG_WIDTH), BF16),
                   jax.ShapeDtypeStruct((batch, 2, HG_HEADS, HG_DK, HG_DV), F32)],
        scratch_shapes=[pltpu.VMEM((seq, w), F32), pltpu.VMEM((seq, w), F32),
                        pltpu.VMEM((2, hps, HG_DV, HG_DK), F32)],
        compiler_params=_cparams("parallel", "parallel"),
        name="hgrn2_mix",
    )(*args)


def _trunk_layer(x, l, p, *, batch, seq, mod, row0, ctx, final):
    t, d = x.shape
    tm = 1024
    per_row = (t if ctx is None else seq) // tm
    geo = dict(layer=l, tiles_per_row=per_row, row0=row0)

    proj, proj32 = ln_mod_matmul(x, p['norm1_w'][l], mod, (1, 0), p['w_in'][l], tm=tm, **geo)

    filt = hyena_filter(seq, p['hy_w1'][l], p['hy_b1'][l], p['hy_freq'][l], p['hy_w2'][l], p['hy_b2'][l],
                        p['hy_w3'][l], p['hy_decay'][l])
    kb = min(seq, 512)
    fwd, inv = _dft_tables(seq, kb)
    spectrum = hyena_spectrum(fwd, filt, p['hy_bias'][l], kb)
    ya = hyena_conv(proj, p['hy_conv_w'][l], p['hy_conv_b'][l], fwd.astype(BF16), inv.astype(BF16),
                    spectrum, batch=batch, seq=seq, ct=512, kb=kb)

    if ctx is None:
        yb = context_attention(proj, proj32, p['attn_sink'][l], batch=batch, seq=seq)
        s0 = None
    else:
        cache_k, cache_v, s0 = ctx
        yb = latent_attention(proj, proj32, cache_k, cache_v, p['attn_sink'][l], layer=l, batch=batch,
                              seq=seq)
        s0 = s0[:, l]

    yc, s_fin = hgrn2_mix(proj, proj32, p['hg_lb'], p['hg_norm_w'][l], s0, layer=l, batch=batch, seq=seq)

    mixed = branch_merge(ya, yb, yc, proj, p['w_branch_a'][l], p['w_branch_b'][l], p['w_branch_c'][l],
                         tm=tm, tn=512)
    x = matmul_gated_residual(mixed, p['w_out'][l], x, mod, 2, tm=tm, tn=512, **geo)

    j = l // 2
    if l % 2 == 0:
        f = ln_mod_glu(x, p['norm2_w'][l], mod, (4, 3), p['ffn_w1'][j], p['ffn_w3'][j], tm=tm, tn=512, **geo)
        x = matmul_gated_residual(f, p['ffn_w2'][j], x, mod, 5, tm=tm, tn=512, **geo)
        if final:
            x = final_norm(x, p['final_norm_w'], tm=512)
    else:
        h2, tw, route, counts = router(x, p['norm2_w'][l], mod, (4, 3), p['router_w'][j], tm=512,
                                       layer=l, tiles_per_row=per_row * 2, row0=row0)
        pos, tile_expert, tile_ends, n_used, rows = _moe_plan(route, counts, t)
        xs = moe_dispatch(h2, pos, tile_ends, rows)
        f = gmm_up(xs, p['moe_w1'][j], p['moe_w3'][j], tile_expert, n_used, tn=512)
        ys = gmm_down(f, p['moe_w2'][j], tile_expert, n_used, tn=512)
        x = moe_combine(x, ys, pos, tw, mod, 5, p['final_norm_w'], final=final, layer=l,
                        tiles_per_row=per_row * (tm // MOE_TOKEN_TILE), row0=row0)
    k = proj32[:, OFF32_KV:OFF32_KV + KV_WIDTH]
    v = proj32[:, OFF32_KV + KV_WIDTH:OFF32_KV + 2 * KV_WIDTH]
    return x, k, v, s_fin


def kernel(x_prompt, x_sample, c, cache_k, cache_v, state_hgrn, c_ctx, ada_w, ada_b, norm1_w, norm2_w, w_in,
           hy_conv_w, hy_conv_b, hy_w1, hy_b1, hy_freq, hy_w2, hy_b2, hy_w3, hy_decay, hy_bias, attn_sink,
           hg_lb, hg_norm_w, w_branch_a, w_branch_b, w_branch_c, w_out, ffn_w1, ffn_w3, ffn_w2, router_w,
           moe_w1, moe_w3, moe_w2, final_norm_w):
    batch, seq, d = x_prompt.shape
    dbatch, dseq, _ = x_sample.shape
    depth = ada_w.shape[0]
    bf = lambda a: a.astype(BF16)
    p = dict(norm1_w=norm1_w, norm2_w=norm2_w, w_in=bf(w_in), hy_conv_w=hy_conv_w, hy_conv_b=hy_conv_b,
             hy_w1=hy_w1, hy_b1=hy_b1, hy_freq=hy_freq, hy_w2=hy_w2, hy_b2=hy_b2, hy_w3=hy_w3,
             hy_decay=hy_decay, hy_bias=hy_bias, attn_sink=attn_sink, hg_lb=hg_lb, hg_norm_w=hg_norm_w,
             w_branch_a=bf(w_branch_a), w_branch_b=bf(w_branch_b), w_branch_c=bf(w_branch_c),
             w_out=bf(w_out), ffn_w1=bf(ffn_w1), ffn_w3=bf(ffn_w3), ffn_w2=bf(ffn_w2), router_w=router_w,
             moe_w1=moe_w1, moe_w3=moe_w3, moe_w2=moe_w2, final_norm_w=final_norm_w)

    nrows = 16
    cond = jnp.zeros((nrows, d), F32).at[:dbatch].set(c).at[dbatch].set(c_ctx)
    mod = ada_modulation(cond, ada_w, ada_b).reshape(depth, nrows, 6, 1, d)

    xp = x_prompt.reshape(batch * seq, d)
    ks, vs, ss = [], [], []
    for l in range(depth):
        xp, k_l, v_l, s_l = _trunk_layer(xp, l, p, batch=batch, seq=seq, mod=mod, row0=dbatch, ctx=None,
                                         final=(l == depth - 1))
        ks.append(k_l.reshape(batch, seq, N_KV_HEADS, HEAD_DIM))
        vs.append(v_l.reshape(batch, seq, N_KV_HEADS, HEAD_DIM))
        ss.append(s_l)
    y_prompt = xp.reshape(batch, seq, d)
    new_cache_k = jnp.stack(ks, axis=1)
    new_cache_v = jnp.stack(vs, axis=1)
    new_state = jnp.stack(ss, axis=1)

    past = cache_k.shape[2]
    ck = cache_k.reshape(dbatch, depth, past, KV_WIDTH)
    cv = cache_v.reshape(dbatch, depth, past, KV_WIDTH)
    xs = x_sample.reshape(dbatch * dseq, d)
    for l in range(depth):
        xs, _, _, _ = _trunk_layer(xs, l, p, batch=dbatch, seq=dseq, mod=mod, row0=0,
                                   ctx=(ck, cv, state_hgrn), final=(l == depth - 1))
    y_sample = xs.reshape(dbatch, dseq, d)
    return (y_prompt, y_sample, new_cache_k, new_cache_v, new_state)
```

```python
import functools
import math

import numpy as np
import jax
import jax.numpy as jnp
from jax import lax
from jax.experimental import pallas as pl
from jax.experimental.pallas import tpu as pltpu

F32 = jnp.float32
BF16 = jnp.bfloat16

VMEM_LIMIT_BYTES = 56 * 1024 * 1024
LANES = 128

EPS = 1e-6
NEG_BIG = -1e30
LB_FLOOR = 1e-30
GRID_W = 64
HY_WIDTH = 1024
HY_BANDS = 16
N_HEADS = 8
N_KV_HEADS = 2
GROUP = N_HEADS // N_KV_HEADS
HEAD_DIM = 128
ATTN_WIDTH = N_HEADS * HEAD_DIM
KV_WIDTH = N_KV_HEADS * HEAD_DIM
WINDOW = 128
ROPE_BASE = 10000.0
ATTN_SCALE = HEAD_DIM ** -0.5
HG_HEADS = 8
HG_DK = 128
HG_DV = 128
HG_WIDTH = HG_HEADS * HG_DK
HG_CHUNK = 128
N_EXPERTS = 8
TOP_K = 2

OFF_HY = 0
OFF_AQ = 3 * HY_WIDTH
OFF_AK = OFF_AQ + ATTN_WIDTH
OFF_AV = OFF_AK + KV_WIDTH
OFF_HQ = OFF_AV + KV_WIDTH
OFF_FF = OFF_HQ + HG_WIDTH
OFF_FB = OFF_FF + HG_WIDTH
OFF_HI = OFF_FB + HG_WIDTH
OFF_HG = OFF_HI + HG_WIDTH
OFF_MA = OFF_HG + HG_WIDTH


def _cparams(*sem):
    return pltpu.CompilerParams(dimension_semantics=sem, vmem_limit_bytes=VMEM_LIMIT_BYTES)


def _split3(x):
    hi = x.astype(BF16)
    r1 = x - hi.astype(F32)
    mid = r1.astype(BF16)
    lo = (r1 - mid.astype(F32)).astype(BF16)
    return hi, mid, lo


def _dot(a, b):
    return jnp.dot(a, b, preferred_element_type=F32)


def _dot_f32(a, b):
    a0, a1, a2 = _split3(a)
    b0, b1, b2 = _split3(b)
    return (_dot(a0, b0) + (_dot(a0, b1) + _dot(a1, b0))
            + (_dot(a0, b2) + _dot(a1, b1) + _dot(a2, b0)))


def _dot_f32_3pass(a, b):
    a0, a1, _ = _split3(a)
    b0, b1, _ = _split3(b)
    return _dot(a0, b0) + (_dot(a0, b1) + _dot(a1, b0))


def _silu(x):
    return x * jax.nn.sigmoid(x)


def _ada_kernel(c_ref, w_ref, b_ref, o_ref):
    o_ref[...] = _dot_f32(_silu(c_ref[...]), w_ref[...]) + b_ref[...]


def ada_modulation(cond, ada_w, ada_b, tn=512):
    depth, d, n = ada_w.shape
    rows = cond.shape[0]
    return pl.pallas_call(
        _ada_kernel,
        grid=(depth, n // tn),
        in_specs=[pl.BlockSpec((rows, d), lambda l, j: (0, 0)),
                  pl.BlockSpec((None, d, tn), lambda l, j: (l, 0, j)),
                  pl.BlockSpec((None, 1, tn), lambda l, j: (l, 0, j))],
        out_specs=pl.BlockSpec((None, rows, tn), lambda l, j: (l, 0, j)),
        out_shape=jax.ShapeDtypeStruct((depth, rows, n), F32),
        compiler_params=_cparams("parallel", "parallel"),
        name="ada_modulation",
    )(cond, ada_w, ada_b.reshape(depth, 1, n))


def _norm_modulate(x, nw, sc, sh):
    ms = jnp.mean(x * x, axis=-1, keepdims=True)
    y = x * lax.rsqrt(ms + EPS) * nw
    return y * (1.0 + sc) + sh


NORM_ROWS = 32


def _fill_norm_modulate(h_ref, x_ref, nw_ref, sc_ref, sh_ref):
    def strip(r, carry):
        rows = pl.ds(pl.multiple_of(r * NORM_ROWS, NORM_ROWS), NORM_ROWS)
        h_ref[rows, :] = _norm_modulate(x_ref[rows, :], nw_ref[...], sc_ref[...], sh_ref[...]).astype(h_ref.dtype)
        return carry

    lax.fori_loop(0, x_ref.shape[0] // NORM_ROWS, strip, 0, unroll=4)


def _mod_index(layer, j, tiles_per_row, row0):
    def index(i, *_):
        return (layer, row0 + i // tiles_per_row, j, 0, 0)
    return index


PROJ_TN = 512
F32_TILES = (OFF_AK // PROJ_TN,) + tuple(range(OFF_FF // PROJ_TN, OFF_HI // PROJ_TN))
OFF32_KV = 0
OFF32_FF = PROJ_TN
OFF32_FB = OFF32_FF + HG_WIDTH
assert OFF_AK % PROJ_TN == 0 and 2 * KV_WIDTH == PROJ_TN and OFF_FF % PROJ_TN == 0 and OFF_HI % PROJ_TN == 0


def _f32_tile_slot(j):
    return jnp.maximum(sum((j >= tile).astype(jnp.int32) for tile in F32_TILES) - 1, 0)


def _ln_mm_kernel(x_ref, nw_ref, sc_ref, sh_ref, w_ref, o_ref, o32_ref, h_ref):
    j = pl.program_id(1)

    @pl.when(j == 0)
    def _():
        _fill_norm_modulate(h_ref, x_ref, nw_ref, sc_ref, sh_ref)

    acc = _dot(h_ref[...], w_ref[...])
    o_ref[...] = acc.astype(o_ref.dtype)
    keep = functools.reduce(jnp.logical_or, [j == tile for tile in F32_TILES])

    @pl.when(keep)
    def _():
        o32_ref[...] = acc


def ln_mod_matmul(x, nw, mod, mod_idx, w, *, layer, tiles_per_row, row0, tm):
    t, d = x.shape
    n = w.shape[1]
    tn = PROJ_TN
    sc_j, sh_j = mod_idx
    mspec = lambda j: pl.BlockSpec((None, None, None, 1, d), _mod_index(layer, j, tiles_per_row, row0))
    return pl.pallas_call(
        _ln_mm_kernel,
        grid=(t // tm, n // tn),
        in_specs=[pl.BlockSpec((tm, d), lambda i, j: (i, 0)),
                  pl.BlockSpec((1, d), lambda i, j: (0, 0)),
                  mspec(sc_j), mspec(sh_j),
                  pl.BlockSpec((d, tn), lambda i, j: (0, j))],
        out_specs=[pl.BlockSpec((tm, tn), lambda i, j: (i, j)),
                   pl.BlockSpec((tm, tn), lambda i, j: (i, _f32_tile_slot(j)))],
        out_shape=[jax.ShapeDtypeStruct((t, n), BF16),
                   jax.ShapeDtypeStruct((t, len(F32_TILES) * tn), F32)],
        scratch_shapes=[pltpu.VMEM((tm, d), BF16)],
        compiler_params=_cparams("parallel", "arbitrary"),
        name="ln_mod_matmul",
    )(x, nw.reshape(1, d), mod, mod, w)


def _ln_glu_kernel(x_ref, nw_ref, sc_ref, sh_ref, w1_ref, w3_ref, o_ref, h_ref):
    @pl.when(pl.program_id(1) == 0)
    def _():
        _fill_norm_modulate(h_ref, x_ref, nw_ref, sc_ref, sh_ref)

    h = h_ref[...]
    o_ref[...] = (_silu(_dot(h, w1_ref[...])) * _dot(h, w3_ref[...])).astype(o_ref.dtype)


def ln_mod_glu(x, nw, mod, mod_idx, w1, w3, *, layer, tiles_per_row, row0, tm, tn):
    t, d = x.shape
    n = w1.shape[1]
    sc_j, sh_j = mod_idx
    mspec = lambda j: pl.BlockSpec((None, None, None, 1, d), _mod_index(layer, j, tiles_per_row, row0))
    return pl.pallas_call(
        _ln_glu_kernel,
        grid=(t // tm, n // tn),
        in_specs=[pl.BlockSpec((tm, d), lambda i, j: (i, 0)),
                  pl.BlockSpec((1, d), lambda i, j: (0, 0)),
                  mspec(sc_j), mspec(sh_j),
                  pl.BlockSpec((d, tn), lambda i, j: (0, j)),
                  pl.BlockSpec((d, tn), lambda i, j: (0, j))],
        out_specs=pl.BlockSpec((tm, tn), lambda i, j: (i, j)),
        out_shape=jax.ShapeDtypeStruct((t, n), BF16),
        scratch_shapes=[pltpu.VMEM((tm, d), BF16)],
        compiler_params=_cparams("parallel", "arbitrary"),
        name="ln_mod_glu",
    )(x, nw.reshape(1, d), mod, mod, w1, w3)


def _merge_kernel(ya_ref, yb_ref, yc_ref, ma_ref, mb_ref, mc_ref, wa_ref, wb_ref, wc_ref, o_ref):
    gate = lambda m_ref: jax.nn.sigmoid(m_ref[...].astype(F32))
    acc = gate(ma_ref) * _dot(ya_ref[...], wa_ref[...])
    acc = acc + gate(mb_ref) * _dot(yb_ref[...], wb_ref[...])
    acc = acc + gate(mc_ref) * _dot(yc_ref[...], wc_ref[...])
    o_ref[...] = acc.astype(o_ref.dtype)


def branch_merge(ya, yb, yc, proj, wa, wb, wc, *, tm, tn):
    t, k = ya.shape
    n = wa.shape[1]
    gate = lambda off: pl.BlockSpec((tm, tn), lambda i, j: (i, off // tn + j))
    yspec = pl.BlockSpec((tm, k), lambda i, j: (i, 0))
    wspec = pl.BlockSpec((k, tn), lambda i, j: (0, j))
    return pl.pallas_call(
        _merge_kernel,
        grid=(t // tm, n // tn),
        in_specs=[yspec, yspec, yspec, gate(OFF_MA), gate(OFF_MA + n), gate(OFF_MA + 2 * n),
                  wspec, wspec, wspec],
        out_specs=pl.BlockSpec((tm, tn), lambda i, j: (i, j)),
        out_shape=jax.ShapeDtypeStruct((t, n), BF16),
        compiler_params=_cparams("parallel", "parallel"),
        name="branch_merge",
    )(ya, yb, yc, proj, proj, proj, wa, wb, wc)


def _mm_resid_kernel(a_ref, w_ref, x_ref, g_ref, o_ref):
    o_ref[...] = x_ref[...] + g_ref[...] * _dot(a_ref[...], w_ref[...])


def matmul_gated_residual(a, w, x, mod, g_j, *, layer, tiles_per_row, row0, tm, tn):
    t, k = a.shape
    n = w.shape[1]
    return pl.pallas_call(
        _mm_resid_kernel,
        grid=(t // tm, n // tn),
        in_specs=[pl.BlockSpec((tm, k), lambda i, j: (i, 0)),
                  pl.BlockSpec((k, tn), lambda i, j: (0, j)),
                  pl.BlockSpec((tm, tn), lambda i, j: (i, j)),
                  pl.BlockSpec((None, None, None, 1, tn),
                               lambda i, j: (layer, row0 + i // tiles_per_row, g_j, 0, j))],
        out_specs=pl.BlockSpec((tm, tn), lambda i, j: (i, j)),
        out_shape=jax.ShapeDtypeStruct((t, n), F32),
        compiler_params=_cparams("parallel", "parallel"),
        name="matmul_gated_residual",
    )(a, w, x, mod)


def _final_norm_kernel(x_ref, fw_ref, o_ref):
    x = x_ref[...]
    o_ref[...] = x * lax.rsqrt(jnp.mean(x * x, axis=-1, keepdims=True) + EPS) * fw_ref[...]


def final_norm(x, fw, *, tm):
    t, d = x.shape
    return pl.pallas_call(
        _final_norm_kernel,
        grid=(t // tm,),
        in_specs=[pl.BlockSpec((tm, d), lambda i: (i, 0)), pl.BlockSpec((1, d), lambda i: (0, 0))],
        out_specs=pl.BlockSpec((tm, d), lambda i: (i, 0)),
        out_shape=jax.ShapeDtypeStruct((t, d), F32),
        compiler_params=_cparams("parallel"),
        name="final_norm",
    )(x, fw.reshape(1, d))


MOE_ROW_TILE = 512
MOE_TOKEN_TILE = 256


_HIGH_HALF = 0xFFFF0000


def _pack_bf16_pairs(x):
    bits = lax.bitcast_convert_type(x.astype(BF16).astype(F32), jnp.uint32)
    half = x.shape[1] // 2
    return (bits[:, :half] & jnp.uint32(_HIGH_HALF)) | (bits[:, half:] >> 16)


def _unpack_bf16_pairs(w):
    hi = lax.bitcast_convert_type(w & jnp.uint32(_HIGH_HALF), F32).astype(BF16)
    lo = lax.bitcast_convert_type(w << 16, F32).astype(BF16)
    return jnp.concatenate([hi, lo], axis=1)


def _router_kernel(x_ref, nw_ref, sc_ref, sh_ref, rw_ref, h_ref, tw_ref, route_ref, cnt_ref, run_ref):
    @pl.when(pl.program_id(0) == 0)
    def _():
        run_ref[...] = jnp.zeros_like(run_ref)

    h = _norm_modulate(x_ref[...], nw_ref[...], sc_ref[...], sh_ref[...])
    h_ref[...] = _pack_bf16_pairs(h)
    tm = h.shape[0]
    lane = lax.broadcasted_iota(jnp.int32, (tm, LANES), 1)
    logits = jnp.where(lane < N_EXPERTS, _dot_f32_3pass(h, rw_ref[...]), -jnp.inf)
    m1 = jnp.max(logits, axis=-1, keepdims=True)
    i1 = jnp.min(jnp.where(logits == m1, lane, LANES), axis=-1, keepdims=True)
    rest = jnp.where(lane == i1, -jnp.inf, logits)
    m2 = jnp.max(rest, axis=-1, keepdims=True)
    i2 = jnp.min(jnp.where(rest == m2, lane, LANES), axis=-1, keepdims=True)
    e2 = jnp.exp(m2 - m1)
    inv = 1.0 / (1.0 + e2)
    tw_ref[...] = jnp.where(lane == 0, inv, jnp.where(lane == 1, e2 * inv, 0.0))

    sel = jnp.where((lane == i1) | (lane == i2), 1.0, 0.0)
    before = lax.broadcasted_iota(jnp.int32, (tm, tm), 1) < lax.broadcasted_iota(jnp.int32, (tm, tm), 0)
    rank = _dot(jnp.where(before, 1.0, 0.0).astype(BF16), sel.astype(BF16)) + run_ref[0:1, :]
    r1 = jnp.sum(jnp.where(lane == i1, rank, 0.0), axis=-1, keepdims=True).astype(jnp.int32)
    r2 = jnp.sum(jnp.where(lane == i2, rank, 0.0), axis=-1, keepdims=True).astype(jnp.int32)
    route_ref[...] = jnp.where(lane == 0, i1, jnp.where(lane == 1, i2, jnp.where(lane == 2, r1, r2)))
    run_ref[...] = run_ref[...] + jnp.sum(sel, axis=0, keepdims=True)
    cnt_ref[...] = run_ref[...]


def router(x, nw, mod, mod_idx, rw, *, layer, tiles_per_row, row0, tm):
    t, d = x.shape
    sc_j, sh_j = mod_idx
    rw_pad = jnp.zeros((d, LANES), F32).at[:, :N_EXPERTS].set(rw)
    mspec = lambda j: pl.BlockSpec((None, None, None, 1, d), _mod_index(layer, j, tiles_per_row, row0))
    return pl.pallas_call(
        _router_kernel,
        grid=(t // tm,),
        in_specs=[pl.BlockSpec((tm, d), lambda i: (i, 0)),
                  pl.BlockSpec((1, d), lambda i: (0, 0)),
                  mspec(sc_j), mspec(sh_j),
                  pl.BlockSpec((d, LANES), lambda i: (0, 0))],
        out_specs=[pl.BlockSpec((tm, d // 2), lambda i: (i, 0)),
                   pl.BlockSpec((tm, LANES), lambda i: (i, 0)),
                   pl.BlockSpec((tm, LANES), lambda i: (i, 0)),
                   pl.BlockSpec((8, LANES), lambda i: (0, 0))],
        out_shape=[jax.ShapeDtypeStruct((t, d // 2), jnp.uint32), jax.ShapeDtypeStruct((t, LANES), F32),
                   jax.ShapeDtypeStruct((t, LANES), jnp.int32), jax.ShapeDtypeStruct((8, LANES), F32)],
        scratch_shapes=[pltpu.VMEM((8, LANES), F32)],
        compiler_params=_cparams("arbitrary"),
        name="router",
    )(x, nw.reshape(1, d), mod, mod, rw_pad)


def _row_copy(src, src_row, dst, dst_row, sem):
    return pltpu.make_async_copy(src.at[pl.ds(src_row, 1), :], dst.at[pl.ds(dst_row, 1), :], sem)


def _dispatch_kernel(pos_ref, ends_ref, h_ref, xs_ref, zero_ref, sem, zsem):
    nt = h_ref.shape[0]
    tm = zero_ref.shape[0]

    @pl.when(pl.program_id(0) == 0)
    def _():
        zero_ref[...] = jnp.zeros_like(zero_ref)

        def last_tile_copy(e):
            row0 = pl.multiple_of((ends_ref[e] - 1) * tm, tm)
            return pltpu.make_async_copy(zero_ref, xs_ref.at[pl.ds(row0, tm), :], zsem)

        def has_rows(e):
            return ends_ref[e] > (ends_ref[e - 1] if e else 0)

        n_tiles = xs_ref.shape[0] // tm

        def spare_tile_copy(k):
            row0 = pl.multiple_of((ends_ref[N_EXPERTS - 1] + k) * tm, tm)
            return pltpu.make_async_copy(zero_ref, xs_ref.at[pl.ds(row0, tm), :], zsem)

        def is_spare(k):
            return ends_ref[N_EXPERTS - 1] + k < n_tiles

        for e in range(N_EXPERTS):
            @pl.when(has_rows(e))
            def _():
                last_tile_copy(e).start()

            @pl.when(is_spare(e))
            def _():
                spare_tile_copy(e).start()
        for e in range(N_EXPERTS):
            @pl.when(has_rows(e))
            def _():
                last_tile_copy(e).wait()

            @pl.when(is_spare(e))
            def _():
                spare_tile_copy(e).wait()

    def start(r, carry):
        for k in range(TOP_K):
            _row_copy(h_ref, r, xs_ref, pos_ref[0, k * nt + r], sem).start()
        return carry

    def wait(r, carry):
        for k in range(TOP_K):
            _row_copy(h_ref, r, xs_ref, pos_ref[0, k * nt + r], sem).wait()
        return carry

    lax.fori_loop(0, nt, start, 0, unroll=8)
    lax.fori_loop(0, nt, wait, 0, unroll=8)


def moe_dispatch(h, pos, tile_ends, rows):
    t, d = h.shape
    nt = MOE_TOKEN_TILE
    return pl.pallas_call(
        _dispatch_kernel,
        grid=(t // nt,),
        in_specs=[pl.BlockSpec((None, 1, TOP_K * nt), lambda i: (i, 0, 0), memory_space=pltpu.SMEM),
                  pl.BlockSpec(memory_space=pltpu.SMEM),
                  pl.BlockSpec((nt, d), lambda i: (i, 0))],
        out_specs=pl.BlockSpec(memory_space=pl.ANY),
        out_shape=jax.ShapeDtypeStruct((rows, d), h.dtype),
        scratch_shapes=[pltpu.VMEM((MOE_ROW_TILE, d), h.dtype), pltpu.SemaphoreType.DMA(()),
                        pltpu.SemaphoreType.DMA(())],
        compiler_params=_cparams("arbitrary"),
        name="moe_dispatch",
    )(pos, tile_ends, h)


def _new_weight_tile(te_ref, i):
    return jnp.logical_or(i == 0, te_ref[i] != te_ref[jnp.maximum(i - 1, 0)])


def _gmm_up_kernel(te_ref, nu_ref, xs_ref, w1_ref, w3_ref, o_ref, w1b_ref, w3b_ref):
    i = pl.program_id(1)
    used = i < nu_ref[0]

    @pl.when(_new_weight_tile(te_ref, i))
    def _():
        w1b_ref[...] = w1_ref[...].astype(BF16)
        w3b_ref[...] = w3_ref[...].astype(BF16)

    @pl.when(used)
    def _():
        a = _unpack_bf16_pairs(xs_ref[...])
        o_ref[...] = (_silu(_dot(a, w1b_ref[...])) * _dot(a, w3b_ref[...])).astype(o_ref.dtype)

    @pl.when(jnp.logical_not(used))
    def _():
        o_ref[...] = jnp.zeros_like(o_ref)


def gmm_up(xs, w1, w3, tile_expert, n_used, *, tn):
    rows = xs.shape[0]
    _, d, n = w1.shape
    tm = MOE_ROW_TILE
    wspec = pl.BlockSpec((None, d, tn), lambda j, i, te, nu: (te[i], 0, j))
    return pl.pallas_call(
        _gmm_up_kernel,
        grid_spec=pltpu.PrefetchScalarGridSpec(
            num_scalar_prefetch=2,
            grid=(n // tn, rows // tm),
            in_specs=[pl.BlockSpec((tm, d // 2), lambda j, i, te, nu: (jnp.minimum(i, nu[0] - 1), 0)),
                      wspec, wspec],
            out_specs=pl.BlockSpec((tm, tn), lambda j, i, te, nu: (i, j)),
            scratch_shapes=[pltpu.VMEM((d, tn), BF16), pltpu.VMEM((d, tn), BF16)]),
        out_shape=jax.ShapeDtypeStruct((rows, n), BF16),
        compiler_params=_cparams("arbitrary", "arbitrary"),
        name="gmm_up",
    )(tile_expert, n_used, xs, w1, w3)


def _gmm_down_kernel(te_ref, nu_ref, f_ref, w_ref, o_ref, wb_ref):
    i = pl.program_id(1)
    used = i < nu_ref[0]

    @pl.when(_new_weight_tile(te_ref, i))
    def _():
        wb_ref[...] = w_ref[...].astype(BF16)

    @pl.when(used)
    def _():
        o_ref[...] = _dot(f_ref[...], wb_ref[...])

    @pl.when(jnp.logical_not(used))
    def _():
        o_ref[...] = jnp.zeros_like(o_ref)


def gmm_down(f, w2, tile_expert, n_used, *, tn):
    rows, k = f.shape
    n = w2.shape[2]
    tm = MOE_ROW_TILE
    return pl.pallas_call(
        _gmm_down_kernel,
        grid_spec=pltpu.PrefetchScalarGridSpec(
            num_scalar_prefetch=2,
            grid=(n // tn, rows // tm),
            in_specs=[pl.BlockSpec((tm, k), lambda j, i, te, nu: (jnp.minimum(i, nu[0] - 1), 0)),
                      pl.BlockSpec((None, k, tn), lambda j, i, te, nu: (te[i], 0, j))],
            out_specs=pl.BlockSpec((tm, tn), lambda j, i, te, nu: (i, j)),
            scratch_shapes=[pltpu.VMEM((k, tn), BF16)]),
        out_shape=jax.ShapeDtypeStruct((rows, n), F32),
        compiler_params=_cparams("arbitrary", "arbitrary"),
        name="gmm_down",
    )(tile_expert, n_used, f, w2)


def _combine_kernel(final, pos_ref, nxt_ref, x_ref, tw_ref, g_ref, fw_ref, ys_ref, o_ref, y_ref, sems):
    nt = x_ref.shape[0]
    i = pl.program_id(0)
    slot = lax.rem(i, 2)

    def gather(p_ref, s):
        def copies(r):
            return [_row_copy(ys_ref, p_ref[0, k * nt + r], y_ref.at[s, k], r, sems.at[s])
                    for k in range(TOP_K)]
        return copies

    def start_all(copies):
        def body(r, carry):
            for c in copies(r):
                c.start()
            return carry
        lax.fori_loop(0, nt, body, 0, unroll=8)

    def wait_all(copies):
        def body(r, carry):
            for c in copies(r):
                c.wait()
            return carry
        lax.fori_loop(0, nt, body, 0, unroll=8)

    @pl.when(i == 0)
    def _():
        start_all(gather(pos_ref, 0))

    @pl.when(i + 1 < pl.num_programs(0))
    def _():
        start_all(gather(nxt_ref, 1 - slot))

    wait_all(gather(pos_ref, slot))
    tw = tw_ref[...]
    x = x_ref[...] + g_ref[...] * (tw[:, 0:1] * y_ref[slot, 0] + tw[:, 1:2] * y_ref[slot, 1])
    if final:
        x = x * lax.rsqrt(jnp.mean(x * x, axis=-1, keepdims=True) + EPS) * fw_ref[...]
    o_ref[...] = x


def moe_combine(x, ys, pos, tw, mod, g_j, fw, *, layer, tiles_per_row, row0, final):
    t, d = x.shape
    nt = MOE_TOKEN_TILE
    n = t // nt
    pos_spec = lambda index: pl.BlockSpec((None, 1, TOP_K * nt), index, memory_space=pltpu.SMEM)
    return pl.pallas_call(
        functools.partial(_combine_kernel, final),
        grid=(n,),
        in_specs=[pos_spec(lambda i: (i, 0, 0)),
                  pos_spec(lambda i: (jnp.minimum(i + 1, n - 1), 0, 0)),
                  pl.BlockSpec((nt, d), lambda i: (i, 0)),
                  pl.BlockSpec((nt, LANES), lambda i: (i, 0)),
                  pl.BlockSpec((None, None, None, 1, d), _mod_index(layer, g_j, tiles_per_row, row0)),
                  pl.BlockSpec((1, d), lambda i: (0, 0)),
                  pl.BlockSpec(memory_space=pl.ANY)],
        out_specs=pl.BlockSpec((nt, d), lambda i: (i, 0)),
        out_shape=jax.ShapeDtypeStruct((t, d), F32),
        scratch_shapes=[pltpu.VMEM((2, TOP_K, nt, d), F32), pltpu.SemaphoreType.DMA((2,))],
        compiler_params=_cparams("arbitrary"),
        name="moe_combine",
    )(pos, pos, x, tw, mod, fw.reshape(1, d), ys)


def _moe_plan(route, counts, t):
    tm = MOE_ROW_TILE
    nt = MOE_TOKEN_TILE
    n_tiles = TOP_K * t // tm + N_EXPERTS
    cnt = counts[0, :N_EXPERTS].astype(jnp.int32)
    tiles = (cnt + tm - 1) // tm
    ends = jnp.cumsum(tiles)
    offs = (ends - tiles) * tm
    pos = [jnp.take(offs, route[:, k]) + route[:, TOP_K + k] for k in range(TOP_K)]
    pos = jnp.concatenate([p.reshape(t // nt, nt) for p in pos], axis=1).reshape(t // nt, 1, TOP_K * nt)
    tile_expert = jnp.sum(jnp.arange(n_tiles, dtype=jnp.int32)[:, None] >= ends[None, :], axis=1)
    tile_expert = jnp.minimum(tile_expert, N_EXPERTS - 1).astype(jnp.int32)
    ends = ends.astype(jnp.int32)
    return pos, tile_expert, ends, ends[-1:], n_tiles * tm


def _hyena_filter_kernel(z_ref, w1_ref, b1_ref, fr_ref, w2_ref, b2_ref, w3_ref, t_ref, dec_ref, o_ref):
    fr = fr_ref[...]
    h = jnp.sin(fr * (_dot_f32(z_ref[...], w1_ref[...]) + b1_ref[...]))
    h = jnp.sin(fr * (_dot_f32(h, w2_ref[...]) + b2_ref[...]))
    o_ref[...] = _dot_f32(h, w3_ref[...]) * jnp.exp(-t_ref[...] * jnp.abs(dec_ref[...]))


def hyena_filter(seq, w1, b1, freq, w2, b2, w3, decay, tl=256):
    t = jnp.linspace(0.0, 1.0, seq, dtype=F32)[:, None]
    pos = jnp.arange(seq, dtype=F32)[:, None]
    bands = jnp.linspace(1e-4, HY_BANDS - 1.0, HY_BANDS, dtype=F32)[None, :]
    ang = (2.0 * math.pi / seq) * pos * bands
    z = jnp.concatenate([t, jnp.cos(ang), -jnp.sin(ang)], axis=-1)
    emb = z.shape[1]
    emb_pad = LANES
    z = jnp.pad(z, ((0, 0), (0, emb_pad - emb)))
    w1p = jnp.pad(w1, ((0, emb_pad - emb), (0, 0)))
    hid = w1.shape[1]
    n = w3.shape[1]
    tl = min(tl, seq)
    full = lambda shape: pl.BlockSpec(shape, lambda i: (0, 0))
    return pl.pallas_call(
        _hyena_filter_kernel,
        grid=(seq // tl,),
        in_specs=[pl.BlockSpec((tl, emb_pad), lambda i: (i, 0)),
                  full((emb_pad, hid)), full((1, hid)), full((1, hid)),
                  full((hid, hid)), full((1, hid)), full((hid, n)),
                  pl.BlockSpec((tl, 1), lambda i: (i, 0)), full((1, n))],
        out_specs=pl.BlockSpec((tl, n), lambda i: (i, 0)),
        out_shape=jax.ShapeDtypeStruct((seq, n), F32),
        compiler_params=_cparams("parallel"),
        name="hyena_filter",
    )(z, w1p, b1.reshape(1, hid), freq.reshape(1, hid), w2, b2.reshape(1, hid), w3, t,
      decay.reshape(1, n))


def _dft_tables(seq, kb):
    n = 2 * seq
    k = jnp.arange(seq, dtype=jnp.int32)[:, None]
    s = jnp.arange(seq, dtype=jnp.int32)[None, :]
    ang = ((k * s) % n).astype(F32) * (2.0 * math.pi / n)
    cos = jnp.cos(ang)
    sin = jnp.sin(ang)
    nyq = jnp.where(s % 2 == 0, 1.0, -1.0).astype(F32)
    is0 = k == 0
    f_re = cos
    f_im = jnp.where(is0, nyq, -sin)
    i_re = jnp.where(is0, 1.0 / n, (2.0 / n) * cos)
    i_im = jnp.where(is0, nyq / n, -(2.0 / n) * sin)
    nkb = seq // kb
    fwd = jnp.concatenate([f_re.reshape(nkb, kb, seq), f_im.reshape(nkb, kb, seq)], axis=1)
    inv = jnp.concatenate([i_re.reshape(nkb, kb, seq), i_im.reshape(nkb, kb, seq)], axis=1)
    return fwd, jnp.swapaxes(inv, 1, 2)


def _spectrum_kernel(kb, f_ref, hf_ref, hb_ref, bias_ref, o_ref):
    f = f_ref[...]
    row = lax.broadcasted_iota(jnp.int32, hb_ref.shape, 0)
    hb0 = jnp.where(row == 0, 0.0, hb_ref[...])
    a = _dot_f32_3pass(f, hf_ref[...])
    b = _dot_f32_3pass(f, hb0)
    orow = lax.broadcasted_iota(jnp.int32, a.shape, 0)
    nyq_slot = (orow == kb) & (pl.program_id(0) == 0)
    o_ref[...] = jnp.where((orow < kb) | nyq_slot, a + b + bias_ref[...], a - b)


def hyena_spectrum(fwd_f32, filt, bias, kb, tc=256):
    nkb, kb2, seq = fwd_f32.shape
    c = filt.shape[1] // 2
    return pl.pallas_call(
        functools.partial(_spectrum_kernel, kb),
        grid=(nkb, c // tc),
        in_specs=[pl.BlockSpec((None, kb2, seq), lambda j, i: (j, 0, 0)),
                  pl.BlockSpec((seq, tc), lambda j, i: (0, i)),
                  pl.BlockSpec((seq, tc), lambda j, i: (0, c // tc + i)),
                  pl.BlockSpec((1, tc), lambda j, i: (0, i))],
        out_specs=pl.BlockSpec((None, kb2, tc), lambda j, i: (j, 0, i)),
        out_shape=jax.ShapeDtypeStruct((nkb, kb2, c), F32),
        compiler_params=_cparams("parallel", "parallel"),
        name="hyena_spectrum",
    )(fwd_f32, filt, filt, bias.reshape(1, c))


def _hyena_conv_kernel(kb, x0_ref, x1_ref, v_ref, cw0_ref, cw1_ref, cwv_ref, cb0_ref, cb1_ref, cbv_ref,
                       f_ref, i_ref, kf_ref, o_ref, u16_ref, acc_ref):
    j = pl.program_id(2)
    seq = x0_ref.shape[0]

    def conv3(x_ref, w_ref, b_ref):
        x = x_ref[...].astype(F32)
        row = lax.broadcasted_iota(jnp.int32, x.shape, 0)
        prev = jnp.where(row == 0, 0.0, pltpu.roll(x, 1, 0))
        nxt = jnp.where(row == seq - 1, 0.0, pltpu.roll(x, seq - 1, 0))
        w = w_ref[...]
        return prev * w[0:1, :] + x * w[1:2, :] + nxt * w[2:3, :] + b_ref[...]

    @pl.when(j == 0)
    def _():
        u16_ref[...] = (conv3(v_ref, cwv_ref, cbv_ref) * conv3(x1_ref, cw1_ref, cb1_ref)).astype(BF16)
        acc_ref[...] = jnp.zeros_like(acc_ref)

    spec = _dot(f_ref[...], u16_ref[...])
    xr, xi = spec[:kb], spec[kb:]
    kf = kf_ref[...]
    kr, ki = kf[:kb], kf[kb:]
    packed = (lax.broadcasted_iota(jnp.int32, xr.shape, 0) == 0) & (j == 0)
    yr = xr * kr - jnp.where(packed, 0.0, xi * ki)
    yi = jnp.where(packed, xi * ki, xr * ki + xi * kr)
    y = jnp.concatenate([yr, yi], axis=0).astype(BF16)
    acc_ref[...] += _dot(i_ref[...], y)

    @pl.when(j == pl.num_programs(2) - 1)
    def _():
        o_ref[...] = (acc_ref[...] * conv3(x0_ref, cw0_ref, cb0_ref)).astype(o_ref.dtype)


def hyena_conv(proj, conv_w, conv_b, fwd, inv, spectrum, *, batch, seq, ct, kb):
    c = HY_WIDTH
    nkb = seq // kb
    ncb = c // ct
    col = lambda part: pl.BlockSpec((seq, ct), lambda b, i, j: (b, part * ncb + i))
    cw = lambda part: pl.BlockSpec((3, ct), lambda b, i, j: (0, part * ncb + i))
    cb = lambda part: pl.BlockSpec((1, ct), lambda b, i, j: (0, part * ncb + i))
    conv_b = conv_b.reshape(1, 3 * c)
    return pl.pallas_call(
        functools.partial(_hyena_conv_kernel, kb),
        grid=(batch, ncb, nkb),
        in_specs=[col(0), col(1), col(2), cw(0), cw(1), cw(2), cb(0), cb(1), cb(2),
                  pl.BlockSpec((None, 2 * kb, seq), lambda b, i, j: (j, 0, 0)),
                  pl.BlockSpec((None, seq, 2 * kb), lambda b, i, j: (j, 0, 0)),
                  pl.BlockSpec((None, 2 * kb, ct), lambda b, i, j: (j, 0, i))],
        out_specs=pl.BlockSpec((seq, ct), lambda b, i, j: (b, i)),
        out_shape=jax.ShapeDtypeStruct((batch * seq, c), BF16),
        scratch_shapes=[pltpu.VMEM((seq, ct), BF16), pltpu.VMEM((seq, ct), F32)],
        compiler_params=_cparams("parallel", "parallel", "arbitrary"),
        name="hyena_conv",
    )(proj, proj, proj, conv_w, conv_w, conv_w, conv_b, conv_b, conv_b, fwd, inv, spectrum)


def _rope(x, cos, sin_a, sin_b):
    return x * cos + pltpu.roll(x, HEAD_DIM - HEAD_DIM // 4, 1) * sin_a + pltpu.roll(x, HEAD_DIM // 4, 1) * sin_b


def _dot_nt(a, b):
    return lax.dot_general(a, b, (((1,), (1,)), ((), ())), preferred_element_type=F32)


LOG2E = 1.0 / math.log(2.0)
LOGIT_SCALE = ATTN_SCALE * LOG2E


def _grouped_softmax_pv(qs, keys, values, sinks, valid, o_ref, score_scale):
    scores = [_dot_nt(q, keys) for q in qs]
    probs, dens = [], []
    for s, sink in zip(scores, sinks):
        if score_scale is not None:
            s = s * score_scale
        if valid is not None:
            s = jnp.where(valid, s, NEG_BIG)
        sink2 = sink * LOG2E
        m = jnp.maximum(jnp.max(s, axis=-1, keepdims=True), sink2)
        p = jnp.exp2(s - m)
        dens.append(jnp.sum(p, axis=-1, keepdims=True) + jnp.exp2(sink2 - m))
        probs.append(p.astype(BF16))
    outs = [_dot(p, values) for p in probs]
    for g, (o, den) in enumerate(zip(outs, dens)):
        o_ref[:, g * HEAD_DIM:(g + 1) * HEAD_DIM] = (o / den).astype(o_ref.dtype)


def _ctx_attn_kernel(sink_ref, q_ref, k_ref, v_ref, o_ref):
    kvh = pl.program_id(1)
    qs = [q_ref[:, g * HEAD_DIM:(g + 1) * HEAD_DIM] for g in range(GROUP)]
    sinks = [sink_ref[kvh * GROUP + g] for g in range(GROUP)]
    _grouped_softmax_pv(qs, k_ref[...].astype(BF16), v_ref[...].astype(BF16), sinks, None, o_ref,
                        LOGIT_SCALE)


def context_attention(proj, proj32, sink, *, batch, seq):
    qw = GROUP * HEAD_DIM
    kv = lambda part: pl.BlockSpec((seq, HEAD_DIM),
                                   lambda b, h: (b, OFF32_KV // HEAD_DIM + part * N_KV_HEADS + h))
    return pl.pallas_call(
        _ctx_attn_kernel,
        grid=(batch, N_KV_HEADS),
        in_specs=[pl.BlockSpec(memory_space=pltpu.SMEM),
                  pl.BlockSpec((seq, qw), lambda b, h: (b, OFF_AQ // qw + h)),
                  kv(0), kv(1)],
        out_specs=pl.BlockSpec((seq, qw), lambda b, h: (b, h)),
        out_shape=jax.ShapeDtypeStruct((batch * seq, ATTN_WIDTH), BF16),
        compiler_params=_cparams("parallel", "parallel"),
        name="context_attention",
    )(sink, proj, proj32, proj32)


def _lat_attn_kernel(seq, sink_ref, q_ref, k_ref, v_ref, ck_ref, cv_ref, cos_ref, sa_ref, sb_ref,
                     o_ref, kr_ref, vb_ref):
    kvh = pl.program_id(1)
    qb = pl.program_id(2)
    blk = q_ref.shape[0]
    nwin = 3 * blk
    past = ck_ref.shape[0]

    @pl.when(qb == 0)
    def _():
        kr_ref[...] = _rope(k_ref[...], cos_ref[...], sa_ref[...], sb_ref[...]).astype(BF16)
        vb_ref[...] = v_ref[...].astype(BF16)

    start = pl.multiple_of(jnp.clip((qb - 1) * blk, 0, seq - nwin), blk)
    keys = jnp.concatenate([kr_ref[pl.ds(start, nwin), :], ck_ref[...].astype(BF16)], axis=0)
    values = jnp.concatenate([vb_ref[pl.ds(start, nwin), :], cv_ref[...].astype(BF16)], axis=0)
    rows = pl.ds(pl.multiple_of(qb * blk, blk), blk)
    cos, sa, sb = cos_ref[rows, :], sa_ref[rows, :], sb_ref[rows, :]
    qpos = qb * blk + lax.broadcasted_iota(jnp.int32, (blk, nwin + past), 0)
    col = lax.broadcasted_iota(jnp.int32, (blk, nwin + past), 1)
    valid = (col >= nwin) | (jnp.abs(qpos - (start + col)) <= WINDOW)
    qs = [(_rope(q_ref[:, g * HEAD_DIM:(g + 1) * HEAD_DIM].astype(F32), cos, sa, sb) * LOGIT_SCALE).astype(BF16)
          for g in range(GROUP)]
    sinks = [sink_ref[kvh * GROUP + g] for g in range(GROUP)]
    _grouped_softmax_pv(qs, keys, values, sinks, valid, o_ref, None)


def _rope_tables(seq):
    rows = seq // GRID_W
    row = jnp.repeat(jnp.arange(rows, dtype=F32), GRID_W)
    col = jnp.tile(jnp.arange(GRID_W, dtype=F32), rows)
    quarter = HEAD_DIM // 4
    inv = ROPE_BASE ** (-jnp.arange(quarter, dtype=F32) / quarter)
    ar = row[:, None] * inv
    ac = col[:, None] * inv
    ang = jnp.concatenate([ar, ar, ac, ac], axis=-1)
    cos, sin = jnp.cos(ang), jnp.sin(ang)
    first = (jnp.arange(HEAD_DIM) % (2 * quarter)) < quarter
    return cos, jnp.where(first, -sin, 0.0), jnp.where(first, 0.0, sin)


def latent_attention(proj, proj32, cache_k, cache_v, sink, *, layer, batch, seq, blk=128):
    qw = GROUP * HEAD_DIM
    past = cache_k.shape[2]
    nqb = seq // blk
    cos, sa, sb = _rope_tables(seq)
    table = pl.BlockSpec((seq, HEAD_DIM), lambda b, h, i: (0, 0))
    cache = pl.BlockSpec((None, None, past, HEAD_DIM), lambda b, h, i: (b, layer, 0, h))
    kv = lambda part: pl.BlockSpec((seq, HEAD_DIM),
                                   lambda b, h, i: (b, OFF32_KV // HEAD_DIM + part * N_KV_HEADS + h))
    return pl.pallas_call(
        functools.partial(_lat_attn_kernel, seq),
        grid=(batch, N_KV_HEADS, nqb),
        in_specs=[pl.BlockSpec(memory_space=pltpu.SMEM),
                  pl.BlockSpec((blk, qw), lambda b, h, i: (b * nqb + i, OFF_AQ // qw + h)),
                  kv(0), kv(1),
                  cache, cache, table, table, table],
        out_specs=pl.BlockSpec((blk, qw), lambda b, h, i: (b * nqb + i, h)),
        out_shape=jax.ShapeDtypeStruct((batch * seq, ATTN_WIDTH), BF16),
        scratch_shapes=[pltpu.VMEM((seq, HEAD_DIM), BF16), pltpu.VMEM((seq, HEAD_DIM), BF16)],
        compiler_params=_cparams("parallel", "parallel", "arbitrary"),
        name="latent_attention",
    )(sink, proj, proj32, proj32, cache_k, cache_v, cos, sa, sb)


_HG_LEVELS = tuple(HG_CHUNK >> (i + 1) for i in range(int(math.log2(HG_CHUNK))))


def _hgrn_tables():
    c = HG_CHUNK
    t = np.arange(c)[:, None]
    u = np.arange(c)[None, :]
    blocks = [(u <= t), (u > t)]
    for m in _HG_LEVELS:
        ref = (t // (2 * m)) * (2 * m) + m - 1
        second = (t % (2 * m)) >= m
        blocks.append(np.where(second, (u > ref) & (u <= t), (u > t) & (u <= ref)))
    fwd = np.concatenate(blocks, axis=0).astype(np.float32)
    bwd = np.concatenate([b[::-1, ::-1] for b in blocks], axis=0).astype(np.float32)
    s = u
    level = np.full((c, c), len(_HG_LEVELS) + 1, np.int32)
    level[t == s] = len(_HG_LEVELS)
    for i, m in enumerate(_HG_LEVELS):
        hit = (t // (2 * m) == s // (2 * m)) & ((t % (2 * m)) >= m) & ((s % (2 * m)) < m)
        level[hit] = i
    a = np.stack([fwd, bwd])
    a = np.concatenate([a, a], axis=2)
    lv = np.stack([level, level.T])
    return jnp.asarray(a, BF16), jnp.asarray(lv, jnp.int32)


def _hgrn_kernel(layer, has_s0, seq, heads, *refs):
    if has_s0:
        (q_ref, ff_ref, fb_ref, i_ref, g_ref, lb_ref, nw_ref, a_ref, lv_ref, s0_ref,
         y_ref, sfin_ref, of_ref, ob_ref, st_ref) = refs
    else:
        (q_ref, ff_ref, fb_ref, i_ref, g_ref, lb_ref, nw_ref, a_ref, lv_ref,
         y_ref, sfin_ref, of_ref, ob_ref, st_ref) = refs
    c = HG_CHUNK
    nlev = len(_HG_LEVELS)
    nc = seq // c

    lbs = lb_ref[...]
    mx = jnp.max(lbs, axis=0, keepdims=True)
    ex = jnp.exp(lbs - mx)
    sm = ex / jnp.sum(ex, axis=0, keepdims=True)
    lb = jnp.zeros(sm.shape[1:], F32)
    for j in range(1, layer + 1):
        lb = lb + sm[j]

    for d in range(2):
        for h in range(heads):
            if has_s0:
                st_ref[d, h] = s0_ref[d, h].T
            else:
                st_ref[d, h] = jnp.zeros((HG_DV, HG_DK), F32)

    def decay_exponents(d, ci):
        rows = pl.ds(pl.multiple_of(ci * c, c), c)
        fpre = (ff_ref if d == 0 else fb_ref)[rows, :]
        lbd = lb[d:d + 1, :]
        f = jnp.maximum(lbd, LB_FLOOR) + (1.0 - lbd) * jax.nn.sigmoid(fpre)
        log2f = jnp.log(f) * (1.0 / math.log(2.0))
        args = _dot(a_ref[d], jnp.concatenate(_split3(log2f)[:2], axis=0))
        chains = []
        for h in range(heads):
            cols = slice(h * HG_DK, (h + 1) * HG_DK)
            q = _silu(q_ref[rows, cols].astype(F32)) * (HG_DK ** -0.5)
            chains.append(dict(d=d, h=h, rows=rows, cols=cols, q=q, k=1.0 - f[:, cols],
                               v=i_ref[rows, cols].astype(BF16), args=args[:, cols]))
        return chains

    def level_products(s):
        q, k = s['q'], s['k']
        e = jnp.exp2(s['args'])
        s['q_in'] = (q * e[0:c]).astype(BF16)
        s['k_out'] = (k * e[c:2 * c]).astype(BF16)
        last = (c - 1) if s['d'] == 0 else 0
        s['total'] = e[last:last + 1]
        prods = [_dot_nt((q * e[(2 + i) * c:(3 + i) * c]).astype(BF16), (k * e[(2 + i) * c:(3 + i) * c]).astype(BF16))
                 for i in range(nlev)]
        s['prods'] = prods + [_dot_nt(q.astype(BF16), k.astype(BF16))]
        return s

    def outputs_and_state(s, o_ref):
        d, h = s['d'], s['h']
        lv = lv_ref[d]
        att = jnp.zeros((c, c), F32)
        for i in range(nlev + 1):
            att = jnp.where(lv == i, s['prods'][i], att)
        st = st_ref[d, h]
        o_ref[s['rows'], s['cols']] = _dot_nt(s['q_in'], st.astype(BF16)) + _dot(att.astype(BF16), s['v'])
        upd = lax.dot_general(s['v'], s['k_out'], (((0,), (0,)), ((), ())), preferred_element_type=F32)
        st_ref[d, h] = st * s['total'] + upd

    def body(ci, carry):
        stage = decay_exponents(0, ci) + decay_exponents(1, nc - 1 - ci)
        stage = [level_products(s) for s in stage]
        for s in stage:
            outputs_and_state(s, of_ref if s['d'] == 0 else ob_ref)
        return carry

    lax.fori_loop(0, nc, body, 0)

    for h in range(heads):
        cols = slice(h * HG_DK, (h + 1) * HG_DK)
        o = of_ref[:, cols] + ob_ref[:, cols]
        o = o * lax.rsqrt(jnp.mean(o * o, axis=-1, keepdims=True) + EPS) * nw_ref[...] * _silu(g_ref[:, cols].astype(F32))
        y_ref[:, cols] = o.astype(y_ref.dtype)
        for d in range(2):
            sfin_ref[d, h] = st_ref[d, h].T


HG_HEADS_PER_STEP = 2


def hgrn2_mix(proj, proj32, hg_lb, norm_w, s0, *, layer, batch, seq):
    a_tab, lv_tab = _hgrn_tables()
    depth = hg_lb.shape[0]
    hps = HG_HEADS_PER_STEP
    w = hps * HG_DK
    col = lambda off: pl.BlockSpec((seq, w), lambda b, h: (b, off // w + h))
    state = pl.BlockSpec((None, 2, hps, HG_DK, HG_DV), lambda b, h: (b, 0, h, 0, 0))
    in_specs = [col(OFF_HQ), col(OFF32_FF), col(OFF32_FB), col(OFF_HI), col(OFF_HG),
                pl.BlockSpec((depth, 2, w), lambda b, h: (0, 0, h)),
                pl.BlockSpec((1, HG_DV), lambda b, h: (0, 0)),
                pl.BlockSpec(a_tab.shape, lambda b, h: (0, 0, 0)),
                pl.BlockSpec(lv_tab.shape, lambda b, h: (0, 0, 0))]
    args = [proj, proj32, proj32, proj, proj, hg_lb, norm_w.reshape(1, HG_DV), a_tab, lv_tab]
    if s0 is not None:
        in_specs.append(state)
        args.append(s0)
    return pl.pallas_call(
        functools.partial(_hgrn_kernel, layer, s0 is not None, seq, hps),
        grid=(batch, HG_HEADS // hps),
        in_specs=in_specs,
        out_specs=[pl.BlockSpec((seq, w), lambda b, h: (b, h)), state],
        out_shape=[jax.ShapeDtypeStruct((batch * seq, HG_WIDTH), BF16),
                   jax.ShapeDtypeStruct((batch, 2, HG_HEADS, HG_DK, HG_DV), F32)],
        scratch_shapes=[pltpu.VMEM((seq, w), F32), pltpu.VMEM((seq, w), F32),
                        pltpu.VMEM((2, hps, HG_DV, HG_DK), F32)],
        compiler_params=_cparams("parallel", "parallel"),
        name="hgrn2_mix",
    )(*args)


def _trunk_layer(x, l, p, *, batch, seq, mod, row0, ctx, final):
    t, d = x.shape
    tm = 1024
    per_row = (t if ctx is None else seq) // tm
    geo = dict(layer=l, tiles_per_row=per_row, row0=row0)

    proj, proj32 = ln_mod_matmul(x, p['norm1_w'][l], mod, (1, 0), p['w_in'][l], tm=tm, **geo)

    filt = hyena_filter(seq, p['hy_w1'][l], p['hy_b1'][l], p['hy_freq'][l], p['hy_w2'][l], p['hy_b2'][l],
                        p['hy_w3'][l], p['hy_decay'][l])
    kb = min(seq, 512)
    fwd, inv = _dft_tables(seq, kb)
    spectrum = hyena_spectrum(fwd, filt, p['hy_bias'][l], kb)
    ya = hyena_conv(proj, p['hy_conv_w'][l], p['hy_conv_b'][l], fwd.astype(BF16), inv.astype(BF16),
                    spectrum, batch=batch, seq=seq, ct=512, kb=kb)

    if ctx is None:
        yb = context_attention(proj, proj32, p['attn_sink'][l], batch=batch, seq=seq)
        s0 = None
    else:
        cache_k, cache_v, s0 = ctx
        yb = latent_attention(proj, proj32, cache_k, cache_v, p['attn_sink'][l], layer=l, batch=batch,
                              seq=seq)
        s0 = s0[:, l]

    yc, s_fin = hgrn2_mix(proj, proj32, p['hg_lb'], p['hg_norm_w'][l], s0, layer=l, batch=batch, seq=seq)

    mixed = branch_merge(ya, yb, yc, proj, p['w_branch_a'][l], p['w_branch_b'][l], p['w_branch_c'][l],
                         tm=tm, tn=512)
    x = matmul_gated_residual(mixed, p['w_out'][l], x, mod, 2, tm=tm, tn=512, **geo)

    j = l // 2
    if l % 2 == 0:
        f = ln_mod_glu(x, p['norm2_w'][l], mod, (4, 3), p['ffn_w1'][j], p['ffn_w3'][j], tm=tm, tn=512, **geo)
        x = matmul_gated_residual(f, p['ffn_w2'][j], x, mod, 5, tm=tm, tn=512, **geo)
        if final:
            x = final_norm(x, p['final_norm_w'], tm=512)
    else:
        h2, tw, route, counts = router(x, p['norm2_w'][l], mod, (4, 3), p['router_w'][j], tm=512,
                                       layer=l, tiles_per_row=per_row * 2, row0=row0)
        pos, tile_expert, tile_ends, n_used, rows = _moe_plan(route, counts, t)
        xs = moe_dispatch(h2, pos, tile_ends, rows)
        f = gmm_up(xs, p['moe_w1'][j], p['moe_w3'][j], tile_expert, n_used, tn=512)
        ys = gmm_down(f, p['moe_w2'][j], tile_expert, n_used, tn=512)
        x = moe_combine(x, ys, pos, tw, mod, 5, p['final_norm_w'], final=final, layer=l,
                        tiles_per_row=per_row * (tm // MOE_TOKEN_TILE), row0=row0)
    k = proj32[:, OFF32_KV:OFF32_KV + KV_WIDTH]
    v = proj32[:, OFF32_KV + KV_WIDTH:OFF32_KV + 2 * KV_WIDTH]
    return x, k, v, s_fin


def kernel(x_prompt, x_sample, c, cache_k, cache_v, state_hgrn, c_ctx, ada_w, ada_b, norm1_w, norm2_w, w_in,
           hy_conv_w, hy_conv_b, hy_w1, hy_b1, hy_freq, hy_w2, hy_b2, hy_w3, hy_decay, hy_bias, attn_sink,
           hg_lb, hg_norm_w, w_branch_a, w_branch_b, w_branch_c, w_out, ffn_w1, ffn_w3, ffn_w2, router_w,
           moe_w1, moe_w3, moe_w2, final_norm_w):
    batch, seq, d = x_prompt.shape
    dbatch, dseq, _ = x_sample.shape
    depth = ada_w.shape[0]
    bf = lambda a: a.astype(BF16)
    p = dict(norm1_w=norm1_w, norm2_w=norm2_w, w_in=bf(w_in), hy_conv_w=hy_conv_w, hy_conv_b=hy_conv_b,
             hy_w1=hy_w1, hy_b1=hy_b1, hy_freq=hy_freq, hy_w2=hy_w2, hy_b2=hy_b2, hy_w3=hy_w3,
             hy_decay=hy_decay, hy_bias=hy_bias, attn_sink=attn_sink, hg_lb=hg_lb, hg_norm_w=hg_norm_w,
             w_branch_a=bf(w_branch_a), w_branch_b=bf(w_branch_b), w_branch_c=bf(w_branch_c),
             w_out=bf(w_out), ffn_w1=bf(ffn_w1), ffn_w3=bf(ffn_w3), ffn_w2=bf(ffn_w2), router_w=router_w,
             moe_w1=moe_w1, moe_w3=moe_w3, moe_w2=moe_w2, final_norm_w=final_norm_w)

    nrows = 16
    cond = jnp.zeros((nrows, d), F32).at[:dbatch].set(c).at[dbatch].set(c_ctx)
    mod = ada_modulation(cond, ada_w, ada_b).reshape(depth, nrows, 6, 1, d)

    xp = x_prompt.reshape(batch * seq, d)
    ks, vs, ss = [], [], []
    for l in range(depth):
        xp, k_l, v_l, s_l = _trunk_layer(xp, l, p, batch=batch, seq=seq, mod=mod, row0=dbatch, ctx=None,
                                         final=(l == depth - 1))
        ks.append(k_l.reshape(batch, seq, N_KV_HEADS, HEAD_DIM))
        vs.append(v_l.reshape(batch, seq, N_KV_HEADS, HEAD_DIM))
        ss.append(s_l)
    y_prompt = xp.reshape(batch, seq, d)
    new_cache_k = jnp.stack(ks, axis=1)
    new_cache_v = jnp.stack(vs, axis=1)
    new_state = jnp.stack(ss, axis=1)

    past = cache_k.shape[2]
    ck = cache_k.reshape(dbatch, depth, past, KV_WIDTH)
    cv = cache_v.reshape(dbatch, depth, past, KV_WIDTH)
    xs = x_sample.reshape(dbatch * dseq, d)
    for l in range(depth):
        xs, _, _, _ = _trunk_layer(xs, l, p, batch=dbatch, seq=dseq, mod=mod, row0=0,
                                   ctx=(ck, cv, state_hgrn), final=(l == depth - 1))
    y_sample = xs.reshape(dbatch, dseq, d)
    return (y_prompt, y_sample, new_cache_k, new_cache_v, new_state)
```

```python
import functools
import math

import numpy as np
import jax
import jax.numpy as jnp
from jax import lax
from jax.experimental import pallas as pl
from jax.experimental.pallas import tpu as pltpu

F32 = jnp.float32
BF16 = jnp.bfloat16

VMEM_LIMIT_BYTES = 56 * 1024 * 1024
LANES = 128

EPS = 1e-6
NEG_BIG = -1e30
LB_FLOOR = 1e-30
GRID_W = 64
HY_WIDTH = 1024
HY_BANDS = 16
N_HEADS = 8
N_KV_HEADS = 2
GROUP = N_HEADS // N_KV_HEADS
HEAD_DIM = 128
ATTN_WIDTH = N_HEADS * HEAD_DIM
KV_WIDTH = N_KV_HEADS * HEAD_DIM
WINDOW = 128
ROPE_BASE = 10000.0
ATTN_SCALE = HEAD_DIM ** -0.5
HG_HEADS = 8
HG_DK = 128
HG_DV = 128
HG_WIDTH = HG_HEADS * HG_DK
HG_CHUNK = 128
N_EXPERTS = 8
TOP_K = 2

OFF_HY = 0
OFF_AQ = 3 * HY_WIDTH
OFF_AK = OFF_AQ + ATTN_WIDTH
OFF_AV = OFF_AK + KV_WIDTH
OFF_HQ = OFF_AV + KV_WIDTH
OFF_FF = OFF_HQ + HG_WIDTH
OFF_FB = OFF_FF + HG_WIDTH
OFF_HI = OFF_FB + HG_WIDTH
OFF_HG = OFF_HI + HG_WIDTH
OFF_MA = OFF_HG + HG_WIDTH


def _cparams(*sem):
    return pltpu.CompilerParams(dimension_semantics=sem, vmem_limit_bytes=VMEM_LIMIT_BYTES)


def _split3(x):
    hi = x.astype(BF16)
    r1 = x - hi.astype(F32)
    mid = r1.astype(BF16)
    lo = (r1 - mid.astype(F32)).astype(BF16)
    return hi, mid, lo


def _dot(a, b):
    return jnp.dot(a, b, preferred_element_type=F32)


def _dot_f32(a, b):
    a0, a1, a2 = _split3(a)
    b0, b1, b2 = _split3(b)
    return (_dot(a0, b0) + (_dot(a0, b1) + _dot(a1, b0))
            + (_dot(a0, b2) + _dot(a1, b1) + _dot(a2, b0)))


def _dot_f32_3pass(a, b):
    a0, a1, _ = _split3(a)
    b0, b1, _ = _split3(b)
    return _dot(a0, b0) + (_dot(a0, b1) + _dot(a1, b0))


def _silu(x):
    return x * jax.nn.sigmoid(x)


def _ada_kernel(c_ref, w_ref, b_ref, o_ref):
    o_ref[...] = _dot_f32(_silu(c_ref[...]), w_ref[...]) + b_ref[...]


def ada_modulation(cond, ada_w, ada_b, tn=512):
    depth, d, n = ada_w.shape
    rows = cond.shape[0]
    return pl.pallas_call(
        _ada_kernel,
        grid=(depth, n // tn),
        in_specs=[pl.BlockSpec((rows, d), lambda l, j: (0, 0)),
                  pl.BlockSpec((None, d, tn), lambda l, j: (l, 0, j)),
                  pl.BlockSpec((None, 1, tn), lambda l, j: (l, 0, j))],
        out_specs=pl.BlockSpec((None, rows, tn), lambda l, j: (l, 0, j)),
        out_shape=jax.ShapeDtypeStruct((depth, rows, n), F32),
        compiler_params=_cparams("parallel", "parallel"),
        name="ada_modulation",
    )(cond, ada_w, ada_b.reshape(depth, 1, n))


def _norm_modulate(x, nw, sc, sh):
    ms = jnp.mean(x * x, axis=-1, keepdims=True)
    y = x * lax.rsqrt(ms + EPS) * nw
    return y * (1.0 + sc) + sh


NORM_ROWS = 32


def _fill_norm_modulate(h_ref, x_ref, nw_ref, sc_ref, sh_ref):
    def strip(r, carry):
        rows = pl.ds(pl.multiple_of(r * NORM_ROWS, NORM_ROWS), NORM_ROWS)
        h_ref[rows, :] = _norm_modulate(x_ref[rows, :], nw_ref[...], sc_ref[...], sh_ref[...]).astype(h_ref.dtype)
        return carry

    lax.fori_loop(0, x_ref.shape[0] // NORM_ROWS, strip, 0, unroll=4)


def _mod_index(layer, j, tiles_per_row, row0):
    def index(i, *_):
        return (layer, row0 + i // tiles_per_row, j, 0, 0)
    return index


PROJ_TN = 512
F32_TILES = (OFF_AK // PROJ_TN,) + tuple(range(OFF_FF // PROJ_TN, OFF_HI // PROJ_TN))
OFF32_KV = 0
OFF32_FF = PROJ_TN
OFF32_FB = OFF32_FF + HG_WIDTH
assert OFF_AK % PROJ_TN == 0 and 2 * KV_WIDTH == PROJ_TN and OFF_FF % PROJ_TN == 0 and OFF_HI % PROJ_TN == 0


def _f32_tile_slot(j):
    return jnp.maximum(sum((j >= tile).astype(jnp.int32) for tile in F32_TILES) - 1, 0)


def _ln_mm_kernel(x_ref, nw_ref, sc_ref, sh_ref, w_ref, o_ref, o32_ref, h_ref):
    j = pl.program_id(1)

    @pl.when(j == 0)
    def _():
        _fill_norm_modulate(h_ref, x_ref, nw_ref, sc_ref, sh_ref)

    acc = _dot(h_ref[...], w_ref[...])
    o_ref[...] = acc.astype(o_ref.dtype)
    keep = functools.reduce(jnp.logical_or, [j == tile for tile in F32_TILES])

    @pl.when(keep)
    def _():
        o32_ref[...] = acc


def ln_mod_matmul(x, nw, mod, mod_idx, w, *, layer, tiles_per_row, row0, tm):
    t, d = x.shape
    n = w.shape[1]
    tn = PROJ_TN
    sc_j, sh_j = mod_idx
    mspec = lambda j: pl.BlockSpec((None, None, None, 1, d), _mod_index(layer, j, tiles_per_row, row0))
    return pl.pallas_call(
        _ln_mm_kernel,
        grid=(t // tm, n // tn),
        in_specs=[pl.BlockSpec((tm, d), lambda i, j: (i, 0)),
                  pl.BlockSpec((1, d), lambda i, j: (0, 0)),
                  mspec(sc_j), mspec(sh_j),
                  pl.BlockSpec((d, tn), lambda i, j: (0, j))],
        out_specs=[pl.BlockSpec((tm, tn), lambda i, j: (i, j)),
                   pl.BlockSpec((tm, tn), lambda i, j: (i, _f32_tile_slot(j)))],
        out_shape=[jax.ShapeDtypeStruct((t, n), BF16),
                   jax.ShapeDtypeStruct((t, len(F32_TILES) * tn), F32)],
        scratch_shapes=[pltpu.VMEM((tm, d), BF16)],
        compiler_params=_cparams("parallel", "arbitrary"),
        name="ln_mod_matmul",
    )(x, nw.reshape(1, d), mod, mod, w)


def _ln_glu_kernel(x_ref, nw_ref, sc_ref, sh_ref, w1_ref, w3_ref, o_ref, h_ref):
    @pl.when(pl.program_id(1) == 0)
    def _():
        _fill_norm_modulate(h_ref, x_ref, nw_ref, sc_ref, sh_ref)

    h = h_ref[...]
    o_ref[...] = (_silu(_dot(h, w1_ref[...])) * _dot(h, w3_ref[...])).astype(o_ref.dtype)


def ln_mod_glu(x, nw, mod, mod_idx, w1, w3, *, layer, tiles_per_row, row0, tm, tn):
    t, d = x.shape
    n = w1.shape[1]
    sc_j, sh_j = mod_idx
    mspec = lambda j: pl.BlockSpec((None, None, None, 1, d), _mod_index(layer, j, tiles_per_row, row0))
    return pl.pallas_call(
        _ln_glu_kernel,
        grid=(t // tm, n // tn),
        in_specs=[pl.BlockSpec((tm, d), lambda i, j: (i, 0)),
                  pl.BlockSpec((1, d), lambda i, j: (0, 0)),
                  mspec(sc_j), mspec(sh_j),
                  pl.BlockSpec((d, tn), lambda i, j: (0, j)),
                  pl.BlockSpec((d, tn), lambda i, j: (0, j))],
        out_specs=pl.BlockSpec((tm, tn), lambda i, j: (i, j)),
        out_shape=jax.ShapeDtypeStruct((t, n), BF16),
        scratch_shapes=[pltpu.VMEM((tm, d), BF16)],
        compiler_params=_cparams("parallel", "arbitrary"),
        name="ln_mod_glu",
    )(x, nw.reshape(1, d), mod, mod, w1, w3)


def _merge_kernel(ya_ref, yb_ref, yc_ref, ma_ref, mb_ref, mc_ref, wa_ref, wb_ref, wc_ref, o_ref):
    gate = lambda m_ref: jax.nn.sigmoid(m_ref[...].astype(F32))
    acc = gate(ma_ref) * _dot(ya_ref[...], wa_ref[...])
    acc = acc + gate(mb_ref) * _dot(yb_ref[...], wb_ref[...])
    acc = acc + gate(mc_ref) * _dot(yc_ref[...], wc_ref[...])
    o_ref[...] = acc.astype(o_ref.dtype)


def branch_merge(ya, yb, yc, proj, wa, wb, wc, *, tm, tn):
    t, k = ya.shape
    n = wa.shape[1]
    gate = lambda off: pl.BlockSpec((tm, tn), lambda i, j: (i, off // tn + j))
    yspec = pl.BlockSpec((tm, k), lambda i, j: (i, 0))
    wspec = pl.BlockSpec((k, tn), lambda i, j: (0, j))
    return pl.pallas_call(
        _merge_kernel,
        grid=(t // tm, n // tn),
        in_specs=[yspec, yspec, yspec, gate(OFF_MA), gate(OFF_MA + n), gate(OFF_MA + 2 * n),
                  wspec, wspec, wspec],
        out_specs=pl.BlockSpec((tm, tn), lambda i, j: (i, j)),
        out_shape=jax.ShapeDtypeStruct((t, n), BF16),
        compiler_params=_cparams("parallel", "parallel"),
        name="branch_merge",
    )(ya, yb, yc, proj, proj, proj, wa, wb, wc)


def _mm_resid_kernel(a_ref, w_ref, x_ref, g_ref, o_ref):
    o_ref[...] = x_ref[...] + g_ref[...] * _dot(a_ref[...], w_ref[...])


def matmul_gated_residual(a, w, x, mod, g_j, *, layer, tiles_per_row, row0, tm, tn):
    t, k = a.shape
    n = w.shape[1]
    tn = min(tn, n)
    return pl.pallas_call(
        _mm_resid_kernel,
        grid=(t // tm, n // tn),
        in_specs=[pl.BlockSpec((tm, k), lambda i, j: (i, 0)),
                  pl.BlockSpec((k, tn), lambda i, j: (0, j)),
                  pl.BlockSpec((tm, tn), lambda i, j: (i, j)),
                  pl.BlockSpec((None, None, None, 1, tn),
                               lambda i, j: (layer, row0 + i // tiles_per_row, g_j, 0, j))],
        out_specs=pl.BlockSpec((tm, tn), lambda i, j: (i, j)),
        out_shape=jax.ShapeDtypeStruct((t, n), F32),
        compiler_params=_cparams("parallel", "parallel"),
        name="matmul_gated_residual",
    )(a, w, x, mod)


def _final_norm_kernel(x_ref, fw_ref, o_ref):
    x = x_ref[...]
    o_ref[...] = x * lax.rsqrt(jnp.mean(x * x, axis=-1, keepdims=True) + EPS) * fw_ref[...]


def final_norm(x, fw, *, tm):
    t, d = x.shape
    return pl.pallas_call(
        _final_norm_kernel,
        grid=(t // tm,),
        in_specs=[pl.BlockSpec((tm, d), lambda i: (i, 0)), pl.BlockSpec((1, d), lambda i: (0, 0))],
        out_specs=pl.BlockSpec((tm, d), lambda i: (i, 0)),
        out_shape=jax.ShapeDtypeStruct((t, d), F32),
        compiler_params=_cparams("parallel"),
        name="final_norm",
    )(x, fw.reshape(1, d))


MOE_ROW_TILE = 512
MOE_TOKEN_TILE = 256


_HIGH_HALF = 0xFFFF0000


def _pack_bf16_pairs(x):
    bits = lax.bitcast_convert_type(x.astype(BF16).astype(F32), jnp.uint32)
    half = x.shape[1] // 2
    return (bits[:, :half] & jnp.uint32(_HIGH_HALF)) | (bits[:, half:] >> 16)


def _unpack_bf16_pairs(w):
    hi = lax.bitcast_convert_type(w & jnp.uint32(_HIGH_HALF), F32).astype(BF16)
    lo = lax.bitcast_convert_type(w << 16, F32).astype(BF16)
    return jnp.concatenate([hi, lo], axis=1)


def _router_kernel(x_ref, nw_ref, sc_ref, sh_ref, rw_ref, h_ref, tw_ref, route_ref, cnt_ref, run_ref):
    @pl.when(pl.program_id(0) == 0)
    def _():
        run_ref[...] = jnp.zeros_like(run_ref)

    h = _norm_modulate(x_ref[...], nw_ref[...], sc_ref[...], sh_ref[...])
    h_ref[...] = _pack_bf16_pairs(h)
    tm = h.shape[0]
    lane = lax.broadcasted_iota(jnp.int32, (tm, LANES), 1)
    logits = jnp.where(lane < N_EXPERTS, _dot_f32_3pass(h, rw_ref[...]), -jnp.inf)
    m1 = jnp.max(logits, axis=-1, keepdims=True)
    i1 = jnp.min(jnp.where(logits == m1, lane, LANES), axis=-1, keepdims=True)
    rest = jnp.where(lane == i1, -jnp.inf, logits)
    m2 = jnp.max(rest, axis=-1, keepdims=True)
    i2 = jnp.min(jnp.where(rest == m2, lane, LANES), axis=-1, keepdims=True)
    e2 = jnp.exp(m2 - m1)
    inv = 1.0 / (1.0 + e2)
    tw_ref[...] = jnp.where(lane == 0, inv, jnp.where(lane == 1, e2 * inv, 0.0))

    sel = jnp.where((lane == i1) | (lane == i2), 1.0, 0.0)
    before = lax.broadcasted_iota(jnp.int32, (tm, tm), 1) < lax.broadcasted_iota(jnp.int32, (tm, tm), 0)
    rank = _dot(jnp.where(before, 1.0, 0.0).astype(BF16), sel.astype(BF16)) + run_ref[0:1, :]
    r1 = jnp.sum(jnp.where(lane == i1, rank, 0.0), axis=-1, keepdims=True).astype(jnp.int32)
    r2 = jnp.sum(jnp.where(lane == i2, rank, 0.0), axis=-1, keepdims=True).astype(jnp.int32)
    route_ref[...] = jnp.where(lane == 0, i1, jnp.where(lane == 1, i2, jnp.where(lane == 2, r1, r2)))
    run_ref[...] = run_ref[...] + jnp.sum(sel, axis=0, keepdims=True)
    cnt_ref[...] = run_ref[...]


def router(x, nw, mod, mod_idx, rw, *, layer, tiles_per_row, row0, tm):
    t, d = x.shape
    sc_j, sh_j = mod_idx
    rw_pad = jnp.zeros((d, LANES), F32).at[:, :N_EXPERTS].set(rw)
    mspec = lambda j: pl.BlockSpec((None, None, None, 1, d), _mod_index(layer, j, tiles_per_row, row0))
    return pl.pallas_call(
        _router_kernel,
        grid=(t // tm,),
        in_specs=[pl.BlockSpec((tm, d), lambda i: (i, 0)),
                  pl.BlockSpec((1, d), lambda i: (0, 0)),
                  mspec(sc_j), mspec(sh_j),
                  pl.BlockSpec((d, LANES), lambda i: (0, 0))],
        out_specs=[pl.BlockSpec((tm, d // 2), lambda i: (i, 0)),
                   pl.BlockSpec((tm, LANES), lambda i: (i, 0)),
                   pl.BlockSpec((tm, LANES), lambda i: (i, 0)),
                   pl.BlockSpec((8, LANES), lambda i: (0, 0))],
        out_shape=[jax.ShapeDtypeStruct((t, d // 2), jnp.uint32), jax.ShapeDtypeStruct((t, LANES), F32),
                   jax.ShapeDtypeStruct((t, LANES), jnp.int32), jax.ShapeDtypeStruct((8, LANES), F32)],
        scratch_shapes=[pltpu.VMEM((8, LANES), F32)],
        compiler_params=_cparams("arbitrary"),
        name="router",
    )(x, nw.reshape(1, d), mod, mod, rw_pad)


def _row_copy(src, src_row, dst, dst_row, sem):
    return pltpu.make_async_copy(src.at[pl.ds(src_row, 1), :], dst.at[pl.ds(dst_row, 1), :], sem)


def _dispatch_kernel(pos_ref, ends_ref, h_ref, xs_ref, zero_ref, sem, zsem):
    nt = h_ref.shape[0]
    tm = zero_ref.shape[0]

    @pl.when(pl.program_id(0) == 0)
    def _():
        zero_ref[...] = jnp.zeros_like(zero_ref)

        def last_tile_copy(e):
            row0 = pl.multiple_of((ends_ref[e] - 1) * tm, tm)
            return pltpu.make_async_copy(zero_ref, xs_ref.at[pl.ds(row0, tm), :], zsem)

        def has_rows(e):
            return ends_ref[e] > (ends_ref[e - 1] if e else 0)

        n_tiles = xs_ref.shape[0] // tm

        def spare_tile_copy(k):
            row0 = pl.multiple_of((ends_ref[N_EXPERTS - 1] + k) * tm, tm)
            return pltpu.make_async_copy(zero_ref, xs_ref.at[pl.ds(row0, tm), :], zsem)

        def is_spare(k):
            return ends_ref[N_EXPERTS - 1] + k < n_tiles

        for e in range(N_EXPERTS):
            @pl.when(has_rows(e))
            def _():
                last_tile_copy(e).start()

            @pl.when(is_spare(e))
            def _():
                spare_tile_copy(e).start()
        for e in range(N_EXPERTS):
            @pl.when(has_rows(e))
            def _():
                last_tile_copy(e).wait()

            @pl.when(is_spare(e))
            def _():
                spare_tile_copy(e).wait()

    def start(r, carry):
        for k in range(TOP_K):
            _row_copy(h_ref, r, xs_ref, pos_ref[0, k * nt + r], sem).start()
        return carry

    def wait(r, carry):
        for k in range(TOP_K):
            _row_copy(h_ref, r, xs_ref, pos_ref[0, k * nt + r], sem).wait()
        return carry

    lax.fori_loop(0, nt, start, 0, unroll=8)
    lax.fori_loop(0, nt, wait, 0, unroll=8)


def moe_dispatch(h, pos, tile_ends, rows):
    t, d = h.shape
    nt = MOE_TOKEN_TILE
    return pl.pallas_call(
        _dispatch_kernel,
        grid=(t // nt,),
        in_specs=[pl.BlockSpec((None, 1, TOP_K * nt), lambda i: (i, 0, 0), memory_space=pltpu.SMEM),
                  pl.BlockSpec(memory_space=pltpu.SMEM),
                  pl.BlockSpec((nt, d), lambda i: (i, 0))],
        out_specs=pl.BlockSpec(memory_space=pl.ANY),
        out_shape=jax.ShapeDtypeStruct((rows, d), h.dtype),
        scratch_shapes=[pltpu.VMEM((MOE_ROW_TILE, d), h.dtype), pltpu.SemaphoreType.DMA(()),
                        pltpu.SemaphoreType.DMA(())],
        compiler_params=_cparams("arbitrary"),
        name="moe_dispatch",
    )(pos, tile_ends, h)


def _new_weight_tile(te_ref, i):
    return jnp.logical_or(i == 0, te_ref[i] != te_ref[jnp.maximum(i - 1, 0)])


def _gmm_up_kernel(te_ref, nu_ref, xs_ref, w1_ref, w3_ref, o_ref, w1b_ref, w3b_ref):
    i = pl.program_id(1)
    used = i < nu_ref[0]

    @pl.when(_new_weight_tile(te_ref, i))
    def _():
        w1b_ref[...] = w1_ref[...].astype(BF16)
        w3b_ref[...] = w3_ref[...].astype(BF16)

    @pl.when(used)
    def _():
        a = _unpack_bf16_pairs(xs_ref[...])
        o_ref[...] = (_silu(_dot(a, w1b_ref[...])) * _dot(a, w3b_ref[...])).astype(o_ref.dtype)

    @pl.when(jnp.logical_not(used))
    def _():
        o_ref[...] = jnp.zeros_like(o_ref)


def gmm_up(xs, w1, w3, tile_expert, n_used, *, tn):
    rows = xs.shape[0]
    _, d, n = w1.shape
    tm = MOE_ROW_TILE
    wspec = pl.BlockSpec((None, d, tn), lambda j, i, te, nu: (te[i], 0, j))
    return pl.pallas_call(
        _gmm_up_kernel,
        grid_spec=pltpu.PrefetchScalarGridSpec(
            num_scalar_prefetch=2,
            grid=(n // tn, rows // tm),
            in_specs=[pl.BlockSpec((tm, d // 2), lambda j, i, te, nu: (jnp.minimum(i, nu[0] - 1), 0)),
                      wspec, wspec],
            out_specs=pl.BlockSpec((tm, tn), lambda j, i, te, nu: (i, j)),
            scratch_shapes=[pltpu.VMEM((d, tn), BF16), pltpu.VMEM((d, tn), BF16)]),
        out_shape=jax.ShapeDtypeStruct((rows, n), BF16),
        compiler_params=_cparams("arbitrary", "arbitrary"),
        name="gmm_up",
    )(tile_expert, n_used, xs, w1, w3)


def _gmm_down_kernel(te_ref, nu_ref, f_ref, w_ref, o_ref, wb_ref):
    i = pl.program_id(1)
    used = i < nu_ref[0]

    @pl.when(_new_weight_tile(te_ref, i))
    def _():
        wb_ref[...] = w_ref[...].astype(BF16)

    @pl.when(used)
    def _():
        o_ref[...] = _dot(f_ref[...], wb_ref[...])

    @pl.when(jnp.logical_not(used))
    def _():
        o_ref[...] = jnp.zeros_like(o_ref)


def gmm_down(f, w2, tile_expert, n_used, *, tn):
    rows, k = f.shape
    n = w2.shape[2]
    tm = MOE_ROW_TILE
    return pl.pallas_call(
        _gmm_down_kernel,
        grid_spec=pltpu.PrefetchScalarGridSpec(
            num_scalar_prefetch=2,
            grid=(n // tn, rows // tm),
            in_specs=[pl.BlockSpec((tm, k), lambda j, i, te, nu: (jnp.minimum(i, nu[0] - 1), 0)),
                      pl.BlockSpec((None, k, tn), lambda j, i, te, nu: (te[i], 0, j))],
            out_specs=pl.BlockSpec((tm, tn), lambda j, i, te, nu: (i, j)),
            scratch_shapes=[pltpu.VMEM((k, tn), BF16)]),
        out_shape=jax.ShapeDtypeStruct((rows, n), F32),
        compiler_params=_cparams("arbitrary", "arbitrary"),
        name="gmm_down",
    )(tile_expert, n_used, f, w2)


def _combine_kernel(final, pos_ref, nxt_ref, x_ref, tw_ref, g_ref, fw_ref, ys_ref, o_ref, y_ref, sems):
    nt = x_ref.shape[0]
    i = pl.program_id(0)
    slot = lax.rem(i, 2)

    def gather(p_ref, s):
        def copies(r):
            return [_row_copy(ys_ref, p_ref[0, k * nt + r], y_ref.at[s, k], r, sems.at[s])
                    for k in range(TOP_K)]
        return copies

    def start_all(copies):
        def body(r, carry):
            for c in copies(r):
                c.start()
            return carry
        lax.fori_loop(0, nt, body, 0, unroll=8)

    def wait_all(copies):
        def body(r, carry):
            for c in copies(r):
                c.wait()
            return carry
        lax.fori_loop(0, nt, body, 0, unroll=8)

    @pl.when(i == 0)
    def _():
        start_all(gather(pos_ref, 0))

    @pl.when(i + 1 < pl.num_programs(0))
    def _():
        start_all(gather(nxt_ref, 1 - slot))

    wait_all(gather(pos_ref, slot))
    tw = tw_ref[...]
    x = x_ref[...] + g_ref[...] * (tw[:, 0:1] * y_ref[slot, 0] + tw[:, 1:2] * y_ref[slot, 1])
    if final:
        x = x * lax.rsqrt(jnp.mean(x * x, axis=-1, keepdims=True) + EPS) * fw_ref[...]
    o_ref[...] = x


def moe_combine(x, ys, pos, tw, mod, g_j, fw, *, layer, tiles_per_row, row0, final):
    t, d = x.shape
    nt = MOE_TOKEN_TILE
    n = t // nt
    pos_spec = lambda index: pl.BlockSpec((None, 1, TOP_K * nt), index, memory_space=pltpu.SMEM)
    return pl.pallas_call(
        functools.partial(_combine_kernel, final),
        grid=(n,),
        in_specs=[pos_spec(lambda i: (i, 0, 0)),
                  pos_spec(lambda i: (jnp.minimum(i + 1, n - 1), 0, 0)),
                  pl.BlockSpec((nt, d), lambda i: (i, 0)),
                  pl.BlockSpec((nt, LANES), lambda i: (i, 0)),
                  pl.BlockSpec((None, None, None, 1, d), _mod_index(layer, g_j, tiles_per_row, row0)),
                  pl.BlockSpec((1, d), lambda i: (0, 0)),
                  pl.BlockSpec(memory_space=pl.ANY)],
        out_specs=pl.BlockSpec((nt, d), lambda i: (i, 0)),
        out_shape=jax.ShapeDtypeStruct((t, d), F32),
        scratch_shapes=[pltpu.VMEM((2, TOP_K, nt, d), F32), pltpu.SemaphoreType.DMA((2,))],
        compiler_params=_cparams("arbitrary"),
        name="moe_combine",
    )(pos, pos, x, tw, mod, fw.reshape(1, d), ys)


def _moe_plan(route, counts, t):
    tm = MOE_ROW_TILE
    nt = MOE_TOKEN_TILE
    n_tiles = TOP_K * t // tm + N_EXPERTS
    cnt = counts[0, :N_EXPERTS].astype(jnp.int32)
    tiles = (cnt + tm - 1) // tm
    ends = jnp.cumsum(tiles)
    offs = (ends - tiles) * tm
    pos = [jnp.take(offs, route[:, k]) + route[:, TOP_K + k] for k in range(TOP_K)]
    pos = jnp.concatenate([p.reshape(t // nt, nt) for p in pos], axis=1).reshape(t // nt, 1, TOP_K * nt)
    tile_expert = jnp.sum(jnp.arange(n_tiles, dtype=jnp.int32)[:, None] >= ends[None, :], axis=1)
    tile_expert = jnp.minimum(tile_expert, N_EXPERTS - 1).astype(jnp.int32)
    ends = ends.astype(jnp.int32)
    return pos, tile_expert, ends, ends[-1:], n_tiles * tm


def _hyena_filter_kernel(z_ref, w1_ref, b1_ref, fr_ref, w2_ref, b2_ref, w3_ref, t_ref, dec_ref, o_ref):
    fr = fr_ref[...]
    h = jnp.sin(fr * (_dot_f32(z_ref[...], w1_ref[...]) + b1_ref[...]))
    h = jnp.sin(fr * (_dot_f32(h, w2_ref[...]) + b2_ref[...]))
    o_ref[...] = _dot_f32(h, w3_ref[...]) * jnp.exp(-t_ref[...] * jnp.abs(dec_ref[...]))


def hyena_filter(seq, w1, b1, freq, w2, b2, w3, decay, tl=256):
    t = jnp.linspace(0.0, 1.0, seq, dtype=F32)[:, None]
    pos = jnp.arange(seq, dtype=F32)[:, None]
    bands = jnp.linspace(1e-4, HY_BANDS - 1.0, HY_BANDS, dtype=F32)[None, :]
    ang = (2.0 * math.pi / seq) * pos * bands
    z = jnp.concatenate([t, jnp.cos(ang), -jnp.sin(ang)], axis=-1)
    emb = z.shape[1]
    emb_pad = LANES
    z = jnp.pad(z, ((0, 0), (0, emb_pad - emb)))
    w1p = jnp.pad(w1, ((0, emb_pad - emb), (0, 0)))
    hid = w1.shape[1]
    n = w3.shape[1]
    tl = min(tl, seq)
    full = lambda shape: pl.BlockSpec(shape, lambda i: (0, 0))
    return pl.pallas_call(
        _hyena_filter_kernel,
        grid=(seq // tl,),
        in_specs=[pl.BlockSpec((tl, emb_pad), lambda i: (i, 0)),
                  full((emb_pad, hid)), full((1, hid)), full((1, hid)),
                  full((hid, hid)), full((1, hid)), full((hid, n)),
                  pl.BlockSpec((tl, 1), lambda i: (i, 0)), full((1, n))],
        out_specs=pl.BlockSpec((tl, n), lambda i: (i, 0)),
        out_shape=jax.ShapeDtypeStruct((seq, n), F32),
        compiler_params=_cparams("parallel"),
        name="hyena_filter",
    )(z, w1p, b1.reshape(1, hid), freq.reshape(1, hid), w2, b2.reshape(1, hid), w3, t,
      decay.reshape(1, n))


def _dft_tables(seq, kb):
    n = 2 * seq
    k = jnp.arange(seq, dtype=jnp.int32)[:, None]
    s = jnp.arange(seq, dtype=jnp.int32)[None, :]
    ang = ((k * s) % n).astype(F32) * (2.0 * math.pi / n)
    cos = jnp.cos(ang)
    sin = jnp.sin(ang)
    nyq = jnp.where(s % 2 == 0, 1.0, -1.0).astype(F32)
    is0 = k == 0
    f_re = cos
    f_im = jnp.where(is0, nyq, -sin)
    i_re = jnp.where(is0, 1.0 / n, (2.0 / n) * cos)
    i_im = jnp.where(is0, nyq / n, -(2.0 / n) * sin)
    nkb = seq // kb
    fwd = jnp.concatenate([f_re.reshape(nkb, kb, seq), f_im.reshape(nkb, kb, seq)], axis=1)
    inv = jnp.concatenate([i_re.reshape(nkb, kb, seq), i_im.reshape(nkb, kb, seq)], axis=1)
    return fwd, jnp.swapaxes(inv, 1, 2)


def _spectrum_kernel(kb, f_ref, hf_ref, hb_ref, bias_ref, o_ref):
    f = f_ref[...]
    row = lax.broadcasted_iota(jnp.int32, hb_ref.shape, 0)
    hb0 = jnp.where(row == 0, 0.0, hb_ref[...])
    a = _dot_f32_3pass(f, hf_ref[...])
    b = _dot_f32_3pass(f, hb0)
    orow = lax.broadcasted_iota(jnp.int32, a.shape, 0)
    nyq_slot = (orow == kb) & (pl.program_id(0) == 0)
    o_ref[...] = jnp.where((orow < kb) | nyq_slot, a + b + bias_ref[...], a - b)


def hyena_spectrum(fwd_f32, filt, bias, kb, tc=256):
    nkb, kb2, seq = fwd_f32.shape
    c = filt.shape[1] // 2
    return pl.pallas_call(
        functools.partial(_spectrum_kernel, kb),
        grid=(nkb, c // tc),
        in_specs=[pl.BlockSpec((None, kb2, seq), lambda j, i: (j, 0, 0)),
                  pl.BlockSpec((seq, tc), lambda j, i: (0, i)),
                  pl.BlockSpec((seq, tc), lambda j, i: (0, c // tc + i)),
                  pl.BlockSpec((1, tc), lambda j, i: (0, i))],
        out_specs=pl.BlockSpec((None, kb2, tc), lambda j, i: (j, 0, i)),
        out_shape=jax.ShapeDtypeStruct((nkb, kb2, c), F32),
        compiler_params=_cparams("parallel", "parallel"),
        name="hyena_spectrum",
    )(fwd_f32, filt, filt, bias.reshape(1, c))


def _hyena_conv_kernel(kb, x0_ref, x1_ref, v_ref, cw0_ref, cw1_ref, cwv_ref, cb0_ref, cb1_ref, cbv_ref,
                       f_ref, i_ref, kf_ref, o_ref, u16_ref, acc_ref):
    j = pl.program_id(2)
    seq = x0_ref.shape[0]

    def conv3(x_ref, w_ref, b_ref):
        x = x_ref[...].astype(F32)
        row = lax.broadcasted_iota(jnp.int32, x.shape, 0)
        prev = jnp.where(row == 0, 0.0, pltpu.roll(x, 1, 0))
        nxt = jnp.where(row == seq - 1, 0.0, pltpu.roll(x, seq - 1, 0))
        w = w_ref[...]
        return prev * w[0:1, :] + x * w[1:2, :] + nxt * w[2:3, :] + b_ref[...]

    @pl.when(j == 0)
    def _():
        u16_ref[...] = (conv3(v_ref, cwv_ref, cbv_ref) * conv3(x1_ref, cw1_ref, cb1_ref)).astype(BF16)
        acc_ref[...] = jnp.zeros_like(acc_ref)

    spec = _dot(f_ref[...], u16_ref[...])
    xr, xi = spec[:kb], spec[kb:]
    kf = kf_ref[...]
    kr, ki = kf[:kb], kf[kb:]
    packed = (lax.broadcasted_iota(jnp.int32, xr.shape, 0) == 0) & (j == 0)
    yr = xr * kr - jnp.where(packed, 0.0, xi * ki)
    yi = jnp.where(packed, xi * ki, xr * ki + xi * kr)
    y = jnp.concatenate([yr, yi], axis=0).astype(BF16)
    acc_ref[...] += _dot(i_ref[...], y)

    @pl.when(j == pl.num_programs(2) - 1)
    def _():
        o_ref[...] = (acc_ref[...] * conv3(x0_ref, cw0_ref, cb0_ref)).astype(o_ref.dtype)


def hyena_conv(proj, conv_w, conv_b, fwd, inv, spectrum, *, batch, seq, ct, kb):
    c = HY_WIDTH
    nkb = seq // kb
    ncb = c // ct
    col = lambda part: pl.BlockSpec((seq, ct), lambda b, i, j: (b, part * ncb + i))
    cw = lambda part: pl.BlockSpec((3, ct), lambda b, i, j: (0, part * ncb + i))
    cb = lambda part: pl.BlockSpec((1, ct), lambda b, i, j: (0, part * ncb + i))
    conv_b = conv_b.reshape(1, 3 * c)
    return pl.pallas_call(
        functools.partial(_hyena_conv_kernel, kb),
        grid=(batch, ncb, nkb),
        in_specs=[col(0), col(1), col(2), cw(0), cw(1), cw(2), cb(0), cb(1), cb(2),
                  pl.BlockSpec((None, 2 * kb, seq), lambda b, i, j: (j, 0, 0)),
                  pl.BlockSpec((None, seq, 2 * kb), lambda b, i, j: (j, 0, 0)),
                  pl.BlockSpec((None, 2 * kb, ct), lambda b, i, j: (j, 0, i))],
        out_specs=pl.BlockSpec((seq, ct), lambda b, i, j: (b, i)),
        out_shape=jax.ShapeDtypeStruct((batch * seq, c), BF16),
        scratch_shapes=[pltpu.VMEM((seq, ct), BF16), pltpu.VMEM((seq, ct), F32)],
        compiler_params=_cparams("parallel", "parallel", "arbitrary"),
        name="hyena_conv",
    )(proj, proj, proj, conv_w, conv_w, conv_w, conv_b, conv_b, conv_b, fwd, inv, spectrum)


def _rope(x, cos, sin_a, sin_b):
    return x * cos + pltpu.roll(x, HEAD_DIM - HEAD_DIM // 4, 1) * sin_a + pltpu.roll(x, HEAD_DIM // 4, 1) * sin_b


def _dot_nt(a, b):
    return lax.dot_general(a, b, (((1,), (1,)), ((), ())), preferred_element_type=F32)


LOG2E = 1.0 / math.log(2.0)
LOGIT_SCALE = ATTN_SCALE * LOG2E


def _grouped_softmax_pv(qs, keys, values, sinks, valid, o_ref, score_scale):
    scores = [_dot_nt(q, keys) for q in qs]
    probs, dens = [], []
    for s, sink in zip(scores, sinks):
        if score_scale is not None:
            s = s * score_scale
        if valid is not None:
            s = jnp.where(valid, s, NEG_BIG)
        sink2 = sink * LOG2E
        m = jnp.maximum(jnp.max(s, axis=-1, keepdims=True), sink2)
        p = jnp.exp2(s - m)
        dens.append(jnp.sum(p, axis=-1, keepdims=True) + jnp.exp2(sink2 - m))
        probs.append(p.astype(BF16))
    outs = [_dot(p, values) for p in probs]
    for g, (o, den) in enumerate(zip(outs, dens)):
        o_ref[:, g * HEAD_DIM:(g + 1) * HEAD_DIM] = (o / den).astype(o_ref.dtype)


def _ctx_attn_kernel(sink_ref, q_ref, k_ref, v_ref, o_ref):
    kvh = pl.program_id(1)
    qs = [q_ref[:, g * HEAD_DIM:(g + 1) * HEAD_DIM] for g in range(GROUP)]
    sinks = [sink_ref[kvh * GROUP + g] for g in range(GROUP)]
    _grouped_softmax_pv(qs, k_ref[...].astype(BF16), v_ref[...].astype(BF16), sinks, None, o_ref,
                        LOGIT_SCALE)


def context_attention(proj, proj32, sink, *, batch, seq):
    qw = GROUP * HEAD_DIM
    kv = lambda part: pl.BlockSpec((seq, HEAD_DIM),
                                   lambda b, h: (b, OFF32_KV // HEAD_DIM + part * N_KV_HEADS + h))
    return pl.pallas_call(
        _ctx_attn_kernel,
        grid=(batch, N_KV_HEADS),
        in_specs=[pl.BlockSpec(memory_space=pltpu.SMEM),
                  pl.BlockSpec((seq, qw), lambda b, h: (b, OFF_AQ // qw + h)),
                  kv(0), kv(1)],
        out_specs=pl.BlockSpec((seq, qw), lambda b, h: (b, h)),
        out_shape=jax.ShapeDtypeStruct((batch * seq, ATTN_WIDTH), BF16),
        compiler_params=_cparams("parallel", "parallel"),
        name="context_attention",
    )(sink, proj, proj32, proj32)


def _lat_attn_kernel(seq, sink_ref, q_ref, k_ref, v_ref, ck_ref, cv_ref, cos_ref, sa_ref, sb_ref,
                     o_ref, kr_ref, vb_ref):
    kvh = pl.program_id(1)
    qb = pl.program_id(2)
    blk = q_ref.shape[0]
    nwin = 3 * blk
    past = ck_ref.shape[0]

    @pl.when(qb == 0)
    def _():
        kr_ref[...] = _rope(k_ref[...], cos_ref[...], sa_ref[...], sb_ref[...]).astype(BF16)
        vb_ref[...] = v_ref[...].astype(BF16)

    start = pl.multiple_of(jnp.clip((qb - 1) * blk, 0, seq - nwin), blk)
    keys = jnp.concatenate([kr_ref[pl.ds(start, nwin), :], ck_ref[...].astype(BF16)], axis=0)
    values = jnp.concatenate([vb_ref[pl.ds(start, nwin), :], cv_ref[...].astype(BF16)], axis=0)
    rows = pl.ds(pl.multiple_of(qb * blk, blk), blk)
    cos, sa, sb = cos_ref[rows, :], sa_ref[rows, :], sb_ref[rows, :]
    qpos = qb * blk + lax.broadcasted_iota(jnp.int32, (blk, nwin + past), 0)
    col = lax.broadcasted_iota(jnp.int32, (blk, nwin + past), 1)
    valid = (col >= nwin) | (jnp.abs(qpos - (start + col)) <= WINDOW)
    qs = [(_rope(q_ref[:, g * HEAD_DIM:(g + 1) * HEAD_DIM].astype(F32), cos, sa, sb) * LOGIT_SCALE).astype(BF16)
          for g in range(GROUP)]
    sinks = [sink_ref[kvh * GROUP + g] for g in range(GROUP)]
    _grouped_softmax_pv(qs, keys, values, sinks, valid, o_ref, None)


def _rope_tables(seq):
    rows = seq // GRID_W
    row = jnp.repeat(jnp.arange(rows, dtype=F32), GRID_W)
    col = jnp.tile(jnp.arange(GRID_W, dtype=F32), rows)
    quarter = HEAD_DIM // 4
    inv = ROPE_BASE ** (-jnp.arange(quarter, dtype=F32) / quarter)
    ar = row[:, None] * inv
    ac = col[:, None] * inv
    ang = jnp.concatenate([ar, ar, ac, ac], axis=-1)
    cos, sin = jnp.cos(ang), jnp.sin(ang)
    first = (jnp.arange(HEAD_DIM) % (2 * quarter)) < quarter
    return cos, jnp.where(first, -sin, 0.0), jnp.where(first, 0.0, sin)


def latent_attention(proj, proj32, cache_k, cache_v, sink, *, layer, batch, seq, blk=128):
    qw = GROUP * HEAD_DIM
    past = cache_k.shape[2]
    nqb = seq // blk
    cos, sa, sb = _rope_tables(seq)
    table = pl.BlockSpec((seq, HEAD_DIM), lambda b, h, i: (0, 0))
    cache = pl.BlockSpec((None, None, past, HEAD_DIM), lambda b, h, i: (b, layer, 0, h))
    kv = lambda part: pl.BlockSpec((seq, HEAD_DIM),
                                   lambda b, h, i: (b, OFF32_KV // HEAD_DIM + part * N_KV_HEADS + h))
    return pl.pallas_call(
        functools.partial(_lat_attn_kernel, seq),
        grid=(batch, N_KV_HEADS, nqb),
        in_specs=[pl.BlockSpec(memory_space=pltpu.SMEM),
                  pl.BlockSpec((blk, qw), lambda b, h, i: (b * nqb + i, OFF_AQ // qw + h)),
                  kv(0), kv(1),
                  cache, cache, table, table, table],
        out_specs=pl.BlockSpec((blk, qw), lambda b, h, i: (b * nqb + i, h)),
        out_shape=jax.ShapeDtypeStruct((batch * seq, ATTN_WIDTH), BF16),
        scratch_shapes=[pltpu.VMEM((seq, HEAD_DIM), BF16), pltpu.VMEM((seq, HEAD_DIM), BF16)],
        compiler_params=_cparams("parallel", "parallel", "arbitrary"),
        name="latent_attention",
    )(sink, proj, proj32, proj32, cache_k, cache_v, cos, sa, sb)


_HG_LEVELS = tuple(HG_CHUNK >> (i + 1) for i in range(int(math.log2(HG_CHUNK))))


def _hgrn_tables():
    c = HG_CHUNK
    t = np.arange(c)[:, None]
    u = np.arange(c)[None, :]
    blocks = [(u <= t), (u > t)]
    for m in _HG_LEVELS:
        ref = (t // (2 * m)) * (2 * m) + m - 1
        second = (t % (2 * m)) >= m
        blocks.append(np.where(second, (u > ref) & (u <= t), (u > t) & (u <= ref)))
    fwd = np.concatenate(blocks, axis=0).astype(np.float32)
    bwd = np.concatenate([b[::-1, ::-1] for b in blocks], axis=0).astype(np.float32)
    s = u
    level = np.full((c, c), len(_HG_LEVELS) + 1, np.int32)
    level[t == s] = len(_HG_LEVELS)
    for i, m in enumerate(_HG_LEVELS):
        hit = (t // (2 * m) == s // (2 * m)) & ((t % (2 * m)) >= m) & ((s % (2 * m)) < m)
        level[hit] = i
    a = np.stack([fwd, bwd])
    a = np.concatenate([a, a], axis=2)
    lv = np.stack([level, level.T])
    return jnp.asarray(a, BF16), jnp.asarray(lv, jnp.int32)


def _hgrn_kernel(layer, has_s0, seq, heads, *refs):
    if has_s0:
        (q_ref, ff_ref, fb_ref, i_ref, g_ref, lb_ref, nw_ref, a_ref, lv_ref, s0_ref,
         y_ref, sfin_ref, of_ref, ob_ref, st_ref) = refs
    else:
        (q_ref, ff_ref, fb_ref, i_ref, g_ref, lb_ref, nw_ref, a_ref, lv_ref,
         y_ref, sfin_ref, of_ref, ob_ref, st_ref) = refs
    c = HG_CHUNK
    nlev = len(_HG_LEVELS)
    nc = seq // c

    lbs = lb_ref[...]
    mx = jnp.max(lbs, axis=0, keepdims=True)
    ex = jnp.exp(lbs - mx)
    sm = ex / jnp.sum(ex, axis=0, keepdims=True)
    lb = jnp.zeros(sm.shape[1:], F32)
    for j in range(1, layer + 1):
        lb = lb + sm[j]

    for d in range(2):
        for h in range(heads):
            if has_s0:
                st_ref[d, h] = s0_ref[d, h].T
            else:
                st_ref[d, h] = jnp.zeros((HG_DV, HG_DK), F32)

    def decay_exponents(d, ci):
        rows = pl.ds(pl.multiple_of(ci * c, c), c)
        fpre = (ff_ref if d == 0 else fb_ref)[rows, :]
        lbd = lb[d:d + 1, :]
        f = jnp.maximum(lbd, LB_FLOOR) + (1.0 - lbd) * jax.nn.sigmoid(fpre)
        log2f = jnp.log(f) * (1.0 / math.log(2.0))
        args = _dot(a_ref[d], jnp.concatenate(_split3(log2f)[:2], axis=0))
        chains = []
        for h in range(heads):
            cols = slice(h * HG_DK, (h + 1) * HG_DK)
            q = _silu(q_ref[rows, cols].astype(F32)) * (HG_DK ** -0.5)
            chains.append(dict(d=d, h=h, rows=rows, cols=cols, q=q, k=1.0 - f[:, cols],
                               v=i_ref[rows, cols].astype(BF16), args=args[:, cols]))
        return chains

    def level_products(s):
        q, k = s['q'], s['k']
        e = jnp.exp2(s['args'])
        s['q_in'] = (q * e[0:c]).astype(BF16)
        s['k_out'] = (k * e[c:2 * c]).astype(BF16)
        last = (c - 1) if s['d'] == 0 else 0
        s['total'] = e[last:last + 1]
        prods = [_dot_nt((q * e[(2 + i) * c:(3 + i) * c]).astype(BF16), (k * e[(2 + i) * c:(3 + i) * c]).astype(BF16))
                 for i in range(nlev)]
        s['prods'] = prods + [_dot_nt(q.astype(BF16), k.astype(BF16))]
        return s

    def outputs_and_state(s, o_ref):
        d, h = s['d'], s['h']
        lv = lv_ref[d]
        att = jnp.zeros((c, c), F32)
        for i in range(nlev + 1):
            att = jnp.where(lv == i, s['prods'][i], att)
        st = st_ref[d, h]
        o_ref[s['rows'], s['cols']] = _dot_nt(s['q_in'], st.astype(BF16)) + _dot(att.astype(BF16), s['v'])
        upd = lax.dot_general(s['v'], s['k_out'], (((0,), (0,)), ((), ())), preferred_element_type=F32)
        st_ref[d, h] = st * s['total'] + upd

    def body(ci, carry):
        stage = decay_exponents(0, ci) + decay_exponents(1, nc - 1 - ci)
        stage = [level_products(s) for s in stage]
        for s in stage:
            outputs_and_state(s, of_ref if s['d'] == 0 else ob_ref)
        return carry

    lax.fori_loop(0, nc, body, 0, unroll=2)

    for h in range(heads):
        cols = slice(h * HG_DK, (h + 1) * HG_DK)
        o = of_ref[:, cols] + ob_ref[:, cols]
        o = o * lax.rsqrt(jnp.mean(o * o, axis=-1, keepdims=True) + EPS) * nw_ref[...] * _silu(g_ref[:, cols].astype(F32))
        y_ref[:, cols] = o.astype(y_ref.dtype)
        for d in range(2):
            sfin_ref[d, h] = st_ref[d, h].T


HG_HEADS_PER_STEP = 2


def hgrn2_mix(proj, proj32, hg_lb, norm_w, s0, *, layer, batch, seq):
    a_tab, lv_tab = _hgrn_tables()
    depth = hg_lb.shape[0]
    hps = HG_HEADS_PER_STEP
    w = hps * HG_DK
    col = lambda off: pl.BlockSpec((seq, w), lambda b, h: (b, off // w + h))
    state = pl.BlockSpec((None, 2, hps, HG_DK, HG_DV), lambda b, h: (b, 0, h, 0, 0))
    in_specs = [col(OFF_HQ), col(OFF32_FF), col(OFF32_FB), col(OFF_HI), col(OFF_HG),
                pl.BlockSpec((depth, 2, w), lambda b, h: (0, 0, h)),
                pl.BlockSpec((1, HG_DV), lambda b, h: (0, 0)),
                pl.BlockSpec(a_tab.shape, lambda b, h: (0, 0, 0)),
                pl.BlockSpec(lv_tab.shape, lambda b, h: (0, 0, 0))]
    args = [proj, proj32, proj32, proj, proj, hg_lb, norm_w.reshape(1, HG_DV), a_tab, lv_tab]
    if s0 is not None:
        in_specs.append(state)
        args.append(s0)
    return pl.pallas_call(
        functools.partial(_hgrn_kernel, layer, s0 is not None, seq, hps),
        grid=(batch, HG_HEADS // hps),
        in_specs=in_specs,
        out_specs=[pl.BlockSpec((seq, w), lambda b, h: (b, h)), state],
        out_shape=[jax.ShapeDtypeStruct((batch * seq, HG_WIDTH), BF16),
                   jax.ShapeDtypeStruct((batch, 2, HG_HEADS, HG_DK, HG_DV), F32)],
        scratch_shapes=[pltpu.VMEM((seq, w), F32), pltpu.VMEM((seq, w), F32),
                        pltpu.VMEM((2, hps, HG_DV, HG_DK), F32)],
        compiler_params=_cparams("parallel", "parallel"),
        name="hgrn2_mix",
    )(*args)


def _trunk_layer(x, l, p, *, batch, seq, mod, row0, ctx, final):
    t, d = x.shape
    tm = 1024
    per_row = (t if ctx is None else seq) // tm
    geo = dict(layer=l, tiles_per_row=per_row, row0=row0)

    proj, proj32 = ln_mod_matmul(x, p['norm1_w'][l], mod, (1, 0), p['w_in'][l], tm=tm, **geo)

    filt = hyena_filter(seq, p['hy_w1'][l], p['hy_b1'][l], p['hy_freq'][l], p['hy_w2'][l], p['hy_b2'][l],
                        p['hy_w3'][l], p['hy_decay'][l])
    kb = min(seq, 512)
    fwd, inv = _dft_tables(seq, kb)
    spectrum = hyena_spectrum(fwd, filt, p['hy_bias'][l], kb)
    ya = hyena_conv(proj, p['hy_conv_w'][l], p['hy_conv_b'][l], fwd.astype(BF16), inv.astype(BF16),
                    spectrum, batch=batch, seq=seq, ct=512, kb=kb)

    if ctx is None:
        yb = context_attention(proj, proj32, p['attn_sink'][l], batch=batch, seq=seq)
        s0 = None
    else:
        cache_k, cache_v, s0 = ctx
        yb = latent_attention(proj, proj32, cache_k, cache_v, p['attn_sink'][l], layer=l, batch=batch,
                              seq=seq)
        s0 = s0[:, l]

    yc, s_fin = hgrn2_mix(proj, proj32, p['hg_lb'], p['hg_norm_w'][l], s0, layer=l, batch=batch, seq=seq)

    mixed = branch_merge(ya, yb, yc, proj, p['w_branch_a'][l], p['w_branch_b'][l], p['w_branch_c'][l],
                         tm=tm, tn=512)
    x = matmul_gated_residual(mixed, p['w_out'][l], x, mod, 2, tm=tm, tn=1024, **geo)

    j = l // 2
    if l % 2 == 0:
        f = ln_mod_glu(x, p['norm2_w'][l], mod, (4, 3), p['ffn_w1'][j], p['ffn_w3'][j], tm=tm, tn=512, **geo)
        x = matmul_gated_residual(f, p['ffn_w2'][j], x, mod, 5, tm=tm, tn=512, **geo)
        if final:
            x = final_norm(x, p['final_norm_w'], tm=512)
    else:
        h2, tw, route, counts = router(x, p['norm2_w'][l], mod, (4, 3), p['router_w'][j], tm=512,
                                       layer=l, tiles_per_row=per_row * 2, row0=row0)
        pos, tile_expert, tile_ends, n_used, rows = _moe_plan(route, counts, t)
        xs = moe_dispatch(h2, pos, tile_ends, rows)
        f = gmm_up(xs, p['moe_w1'][j], p['moe_w3'][j], tile_expert, n_used, tn=512)
        ys = gmm_down(f, p['moe_w2'][j], tile_expert, n_used, tn=512)
        x = moe_combine(x, ys, pos, tw, mod, 5, p['final_norm_w'], final=final, layer=l,
                        tiles_per_row=per_row * (tm // MOE_TOKEN_TILE), row0=row0)
    k = proj32[:, OFF32_KV:OFF32_KV + KV_WIDTH]
    v = proj32[:, OFF32_KV + KV_WIDTH:OFF32_KV + 2 * KV_WIDTH]
    return x, k, v, s_fin


def kernel(x_prompt, x_sample, c, cache_k, cache_v, state_hgrn, c_ctx, ada_w, ada_b, norm1_w, norm2_w, w_in,
           hy_conv_w, hy_conv_b, hy_w1, hy_b1, hy_freq, hy_w2, hy_b2, hy_w3, hy_decay, hy_bias, attn_sink,
           hg_lb, hg_norm_w, w_branch_a, w_branch_b, w_branch_c, w_out, ffn_w1, ffn_w3, ffn_w2, router_w,
           moe_w1, moe_w3, moe_w2, final_norm_w):
    batch, seq, d = x_prompt.shape
    dbatch, dseq, _ = x_sample.shape
    depth = ada_w.shape[0]
    bf = lambda a: a.astype(BF16)
    p = dict(norm1_w=norm1_w, norm2_w=norm2_w, w_in=bf(w_in), hy_conv_w=hy_conv_w, hy_conv_b=hy_conv_b,
             hy_w1=hy_w1, hy_b1=hy_b1, hy_freq=hy_freq, hy_w2=hy_w2, hy_b2=hy_b2, hy_w3=hy_w3,
             hy_decay=hy_decay, hy_bias=hy_bias, attn_sink=attn_sink, hg_lb=hg_lb, hg_norm_w=hg_norm_w,
             w_branch_a=bf(w_branch_a), w_branch_b=bf(w_branch_b), w_branch_c=bf(w_branch_c),
             w_out=bf(w_out), ffn_w1=bf(ffn_w1), ffn_w3=bf(ffn_w3), ffn_w2=bf(ffn_w2), router_w=router_w,
             moe_w1=moe_w1, moe_w3=moe_w3, moe_w2=moe_w2, final_norm_w=final_norm_w)

    nrows = 16
    cond = jnp.zeros((nrows, d), F32).at[:dbatch].set(c).at[dbatch].set(c_ctx)
    mod = ada_modulation(cond, ada_w, ada_b).reshape(depth, nrows, 6, 1, d)

    xp = x_prompt.reshape(batch * seq, d)
    ks, vs, ss = [], [], []
    for l in range(depth):
        xp, k_l, v_l, s_l = _trunk_layer(xp, l, p, batch=batch, seq=seq, mod=mod, row0=dbatch, ctx=None,
                                         final=(l == depth - 1))
        ks.append(k_l.reshape(batch, seq, N_KV_HEADS, HEAD_DIM))
        vs.append(v_l.reshape(batch, seq, N_KV_HEADS, HEAD_DIM))
        ss.append(s_l)
    y_prompt = xp.reshape(batch, seq, d)
    new_cache_k = jnp.stack(ks, axis=1)
    new_cache_v = jnp.stack(vs, axis=1)
    new_state = jnp.stack(ss, axis=1)

    past = cache_k.shape[2]
    ck = cache_k.reshape(dbatch, depth, past, KV_WIDTH)
    cv = cache_v.reshape(dbatch, depth, past, KV_WIDTH)
    xs = x_sample.reshape(dbatch * dseq, d)
    for l in range(depth):
        xs, _, _, _ = _trunk_layer(xs, l, p, batch=dbatch, seq=dseq, mod=mod, row0=0,
                                   ctx=(ck, cv, state_hgrn), final=(l == depth - 1))
    y_sample = xs.reshape(dbatch, dseq, d)
    return (y_prompt, y_sample, new_cache_k, new_cache_v, new_state)
```

```python
import functools
import math

import numpy as np
import jax
import jax.numpy as jnp
from jax import lax
from jax.experimental import pallas as pl
from jax.experimental.pallas import tpu as pltpu

F32 = jnp.float32
BF16 = jnp.bfloat16

VMEM_LIMIT_BYTES = 56 * 1024 * 1024
LANES = 128

EPS = 1e-6
NEG_BIG = -1e30
LB_FLOOR = 1e-30
GRID_W = 64
HY_WIDTH = 1024
HY_BANDS = 16
N_HEADS = 8
N_KV_HEADS = 2
GROUP = N_HEADS // N_KV_HEADS
HEAD_DIM = 128
ATTN_WIDTH = N_HEADS * HEAD_DIM
KV_WIDTH = N_KV_HEADS * HEAD_DIM
WINDOW = 128
ROPE_BASE = 10000.0
ATTN_SCALE = HEAD_DIM ** -0.5
HG_HEADS = 8
HG_DK = 128
HG_DV = 128
HG_WIDTH = HG_HEADS * HG_DK
HG_CHUNK = 128
N_EXPERTS = 8
TOP_K = 2

OFF_HY = 0
OFF_AQ = 3 * HY_WIDTH
OFF_AK = OFF_AQ + ATTN_WIDTH
OFF_AV = OFF_AK + KV_WIDTH
OFF_HQ = OFF_AV + KV_WIDTH
OFF_FF = OFF_HQ + HG_WIDTH
OFF_FB = OFF_FF + HG_WIDTH
OFF_HI = OFF_FB + HG_WIDTH
OFF_HG = OFF_HI + HG_WIDTH
OFF_MA = OFF_HG + HG_WIDTH


def _cparams(*sem):
    return pltpu.CompilerParams(dimension_semantics=sem, vmem_limit_bytes=VMEM_LIMIT_BYTES)


def _split3(x):
    hi = x.astype(BF16)
    r1 = x - hi.astype(F32)
    mid = r1.astype(BF16)
    lo = (r1 - mid.astype(F32)).astype(BF16)
    return hi, mid, lo


def _dot(a, b):
    return jnp.dot(a, b, preferred_element_type=F32)


def _dot_f32(a, b):
    a0, a1, a2 = _split3(a)
    b0, b1, b2 = _split3(b)
    return (_dot(a0, b0) + (_dot(a0, b1) + _dot(a1, b0))
            + (_dot(a0, b2) + _dot(a1, b1) + _dot(a2, b0)))


def _dot_f32_3pass(a, b):
    a0, a1, _ = _split3(a)
    b0, b1, _ = _split3(b)
    return _dot(a0, b0) + (_dot(a0, b1) + _dot(a1, b0))


def _silu(x):
    return x * jax.nn.sigmoid(x)


def _ada_kernel(c_ref, w_ref, b_ref, o_ref):
    o_ref[...] = _dot_f32(_silu(c_ref[...]), w_ref[...]) + b_ref[...]


def ada_modulation(cond, ada_w, ada_b, tn=512):
    depth, d, n = ada_w.shape
    rows = cond.shape[0]
    return pl.pallas_call(
        _ada_kernel,
        grid=(depth, n // tn),
        in_specs=[pl.BlockSpec((rows, d), lambda l, j: (0, 0)),
                  pl.BlockSpec((None, d, tn), lambda l, j: (l, 0, j)),
                  pl.BlockSpec((None, 1, tn), lambda l, j: (l, 0, j))],
        out_specs=pl.BlockSpec((None, rows, tn), lambda l, j: (l, 0, j)),
        out_shape=jax.ShapeDtypeStruct((depth, rows, n), F32),
        compiler_params=_cparams("parallel", "parallel"),
        name="ada_modulation",
    )(cond, ada_w, ada_b.reshape(depth, 1, n))


def _norm_modulate(x, nw, sc, sh):
    ms = jnp.mean(x * x, axis=-1, keepdims=True)
    y = x * lax.rsqrt(ms + EPS) * nw
    return y * (1.0 + sc) + sh


NORM_ROWS = 32


def _fill_norm_modulate(h_ref, x_ref, nw_ref, sc_ref, sh_ref):
    def strip(r, carry):
        rows = pl.ds(pl.multiple_of(r * NORM_ROWS, NORM_ROWS), NORM_ROWS)
        h_ref[rows, :] = _norm_modulate(x_ref[rows, :], nw_ref[...], sc_ref[...], sh_ref[...]).astype(h_ref.dtype)
        return carry

    lax.fori_loop(0, x_ref.shape[0] // NORM_ROWS, strip, 0, unroll=4)


def _mod_index(layer, j, tiles_per_row, row0):
    def index(i, *_):
        return (layer, row0 + i // tiles_per_row, j, 0, 0)
    return index


PROJ_TN = 512
F32_TILES = (OFF_AK // PROJ_TN,) + tuple(range(OFF_FF // PROJ_TN, OFF_HI // PROJ_TN))
OFF32_KV = 0
OFF32_FF = PROJ_TN
OFF32_FB = OFF32_FF + HG_WIDTH
assert OFF_AK % PROJ_TN == 0 and 2 * KV_WIDTH == PROJ_TN and OFF_FF % PROJ_TN == 0 and OFF_HI % PROJ_TN == 0


def _f32_tile_slot(j):
    return jnp.maximum(sum((j >= tile).astype(jnp.int32) for tile in F32_TILES) - 1, 0)


def _ln_mm_kernel(x_ref, nw_ref, sc_ref, sh_ref, w_ref, o_ref, o32_ref, h_ref):
    j = pl.program_id(1)

    @pl.when(j == 0)
    def _():
        _fill_norm_modulate(h_ref, x_ref, nw_ref, sc_ref, sh_ref)

    acc = _dot(h_ref[...], w_ref[...])
    o_ref[...] = acc.astype(o_ref.dtype)
    keep = functools.reduce(jnp.logical_or, [j == tile for tile in F32_TILES])

    @pl.when(keep)
    def _():
        o32_ref[...] = acc


def ln_mod_matmul(x, nw, mod, mod_idx, w, *, layer, tiles_per_row, row0, tm):
    t, d = x.shape
    n = w.shape[1]
    tn = PROJ_TN
    sc_j, sh_j = mod_idx
    mspec = lambda j: pl.BlockSpec((None, None, None, 1, d), _mod_index(layer, j, tiles_per_row, row0))
    return pl.pallas_call(
        _ln_mm_kernel,
        grid=(t // tm, n // tn),
        in_specs=[pl.BlockSpec((tm, d), lambda i, j: (i, 0)),
                  pl.BlockSpec((1, d), lambda i, j: (0, 0)),
                  mspec(sc_j), mspec(sh_j),
                  pl.BlockSpec((d, tn), lambda i, j: (0, j))],
        out_specs=[pl.BlockSpec((tm, tn), lambda i, j: (i, j)),
                   pl.BlockSpec((tm, tn), lambda i, j: (i, _f32_tile_slot(j)))],
        out_shape=[jax.ShapeDtypeStruct((t, n), BF16),
                   jax.ShapeDtypeStruct((t, len(F32_TILES) * tn), F32)],
        scratch_shapes=[pltpu.VMEM((tm, d), BF16)],
        compiler_params=_cparams("parallel", "arbitrary"),
        name="ln_mod_matmul",
    )(x, nw.reshape(1, d), mod, mod, w)


def _ln_glu_kernel(x_ref, nw_ref, sc_ref, sh_ref, w1_ref, w3_ref, o_ref, h_ref):
    @pl.when(pl.program_id(1) == 0)
    def _():
        _fill_norm_modulate(h_ref, x_ref, nw_ref, sc_ref, sh_ref)

    h = h_ref[...]
    o_ref[...] = (_silu(_dot(h, w1_ref[...])) * _dot(h, w3_ref[...])).astype(o_ref.dtype)


def ln_mod_glu(x, nw, mod, mod_idx, w1, w3, *, layer, tiles_per_row, row0, tm, tn):
    t, d = x.shape
    n = w1.shape[1]
    sc_j, sh_j = mod_idx
    mspec = lambda j: pl.BlockSpec((None, None, None, 1, d), _mod_index(layer, j, tiles_per_row, row0))
    return pl.pallas_call(
        _ln_glu_kernel,
        grid=(t // tm, n // tn),
        in_specs=[pl.BlockSpec((tm, d), lambda i, j: (i, 0)),
                  pl.BlockSpec((1, d), lambda i, j: (0, 0)),
                  mspec(sc_j), mspec(sh_j),
                  pl.BlockSpec((d, tn), lambda i, j: (0, j)),
                  pl.BlockSpec((d, tn), lambda i, j: (0, j))],
        out_specs=pl.BlockSpec((tm, tn), lambda i, j: (i, j)),
        out_shape=jax.ShapeDtypeStruct((t, n), BF16),
        scratch_shapes=[pltpu.VMEM((tm, d), BF16)],
        compiler_params=_cparams("parallel", "arbitrary"),
        name="ln_mod_glu",
    )(x, nw.reshape(1, d), mod, mod, w1, w3)


def _merge_kernel(ya_ref, yb_ref, yc_ref, ma_ref, mb_ref, mc_ref, wa_ref, wb_ref, wc_ref, o_ref):
    gate = lambda m_ref: jax.nn.sigmoid(m_ref[...].astype(F32))
    acc = gate(ma_ref) * _dot(ya_ref[...], wa_ref[...])
    acc = acc + gate(mb_ref) * _dot(yb_ref[...], wb_ref[...])
    acc = acc + gate(mc_ref) * _dot(yc_ref[...], wc_ref[...])
    o_ref[...] = acc.astype(o_ref.dtype)


def branch_merge(ya, yb, yc, proj, wa, wb, wc, *, tm, tn):
    t, k = ya.shape
    n = wa.shape[1]
    gate = lambda off: pl.BlockSpec((tm, tn), lambda i, j: (i, off // tn + j))
    yspec = pl.BlockSpec((tm, k), lambda i, j: (i, 0))
    wspec = pl.BlockSpec((k, tn), lambda i, j: (0, j))
    return pl.pallas_call(
        _merge_kernel,
        grid=(t // tm, n // tn),
        in_specs=[yspec, yspec, yspec, gate(OFF_MA), gate(OFF_MA + n), gate(OFF_MA + 2 * n),
                  wspec, wspec, wspec],
        out_specs=pl.BlockSpec((tm, tn), lambda i, j: (i, j)),
        out_shape=jax.ShapeDtypeStruct((t, n), BF16),
        compiler_params=_cparams("parallel", "parallel"),
        name="branch_merge",
    )(ya, yb, yc, proj, proj, proj, wa, wb, wc)


def _mm_resid_kernel(a_ref, w_ref, x_ref, g_ref, o_ref):
    o_ref[...] = x_ref[...] + g_ref[...] * _dot(a_ref[...], w_ref[...])


def matmul_gated_residual(a, w, x, mod, g_j, *, layer, tiles_per_row, row0, tm, tn):
    t, k = a.shape
    n = w.shape[1]
    tn = min(tn, n)
    return pl.pallas_call(
        _mm_resid_kernel,
        grid=(t // tm, n // tn),
        in_specs=[pl.BlockSpec((tm, k), lambda i, j: (i, 0)),
                  pl.BlockSpec((k, tn), lambda i, j: (0, j)),
                  pl.BlockSpec((tm, tn), lambda i, j: (i, j)),
                  pl.BlockSpec((None, None, None, 1, tn),
                               lambda i, j: (layer, row0 + i // tiles_per_row, g_j, 0, j))],
        out_specs=pl.BlockSpec((tm, tn), lambda i, j: (i, j)),
        out_shape=jax.ShapeDtypeStruct((t, n), F32),
        compiler_params=_cparams("parallel", "parallel"),
        name="matmul_gated_residual",
    )(a, w, x, mod)


def _final_norm_kernel(x_ref, fw_ref, o_ref):
    x = x_ref[...]
    o_ref[...] = x * lax.rsqrt(jnp.mean(x * x, axis=-1, keepdims=True) + EPS) * fw_ref[...]


def final_norm(x, fw, *, tm):
    t, d = x.shape
    return pl.pallas_call(
        _final_norm_kernel,
        grid=(t // tm,),
        in_specs=[pl.BlockSpec((tm, d), lambda i: (i, 0)), pl.BlockSpec((1, d), lambda i: (0, 0))],
        out_specs=pl.BlockSpec((tm, d), lambda i: (i, 0)),
        out_shape=jax.ShapeDtypeStruct((t, d), F32),
        compiler_params=_cparams("parallel"),
        name="final_norm",
    )(x, fw.reshape(1, d))


MOE_ROW_TILE = 512
MOE_TOKEN_TILE = 256


_HIGH_HALF = 0xFFFF0000


def _pack_bf16_pairs(x):
    bits = lax.bitcast_convert_type(x.astype(BF16).astype(F32), jnp.uint32)
    half = x.shape[1] // 2
    return (bits[:, :half] & jnp.uint32(_HIGH_HALF)) | (bits[:, half:] >> 16)


def _unpack_bf16_pairs(w):
    hi = lax.bitcast_convert_type(w & jnp.uint32(_HIGH_HALF), F32).astype(BF16)
    lo = lax.bitcast_convert_type(w << 16, F32).astype(BF16)
    return jnp.concatenate([hi, lo], axis=1)


def _router_kernel(x_ref, nw_ref, sc_ref, sh_ref, rw_ref, h_ref, tw_ref, route_ref, cnt_ref, run_ref):
    @pl.when(pl.program_id(0) == 0)
    def _():
        run_ref[...] = jnp.zeros_like(run_ref)

    h = _norm_modulate(x_ref[...], nw_ref[...], sc_ref[...], sh_ref[...])
    h_ref[...] = _pack_bf16_pairs(h)
    tm = h.shape[0]
    lane = lax.broadcasted_iota(jnp.int32, (tm, LANES), 1)
    logits = jnp.where(lane < N_EXPERTS, _dot_f32_3pass(h, rw_ref[...]), -jnp.inf)
    m1 = jnp.max(logits, axis=-1, keepdims=True)
    i1 = jnp.min(jnp.where(logits == m1, lane, LANES), axis=-1, keepdims=True)
    rest = jnp.where(lane == i1, -jnp.inf, logits)
    m2 = jnp.max(rest, axis=-1, keepdims=True)
    i2 = jnp.min(jnp.where(rest == m2, lane, LANES), axis=-1, keepdims=True)
    e2 = jnp.exp(m2 - m1)
    inv = 1.0 / (1.0 + e2)
    tw_ref[...] = jnp.where(lane == 0, inv, jnp.where(lane == 1, e2 * inv, 0.0))

    sel = jnp.where((lane == i1) | (lane == i2), 1.0, 0.0)
    before = lax.broadcasted_iota(jnp.int32, (tm, tm), 1) < lax.broadcasted_iota(jnp.int32, (tm, tm), 0)
    rank = _dot(jnp.where(before, 1.0, 0.0).astype(BF16), sel.astype(BF16)) + run_ref[0:1, :]
    r1 = jnp.sum(jnp.where(lane == i1, rank, 0.0), axis=-1, keepdims=True).astype(jnp.int32)
    r2 = jnp.sum(jnp.where(lane == i2, rank, 0.0), axis=-1, keepdims=True).astype(jnp.int32)
    route_ref[...] = jnp.where(lane == 0, i1, jnp.where(lane == 1, i2, jnp.where(lane == 2, r1, r2)))
    run_ref[...] = run_ref[...] + jnp.sum(sel, axis=0, keepdims=True)
    cnt_ref[...] = run_ref[...]


def router(x, nw, mod, mod_idx, rw, *, layer, tiles_per_row, row0, tm):
    t, d = x.shape
    sc_j, sh_j = mod_idx
    rw_pad = jnp.zeros((d, LANES), F32).at[:, :N_EXPERTS].set(rw)
    mspec = lambda j: pl.BlockSpec((None, None, None, 1, d), _mod_index(layer, j, tiles_per_row, row0))
    return pl.pallas_call(
        _router_kernel,
        grid=(t // tm,),
        in_specs=[pl.BlockSpec((tm, d), lambda i: (i, 0)),
                  pl.BlockSpec((1, d), lambda i: (0, 0)),
                  mspec(sc_j), mspec(sh_j),
                  pl.BlockSpec((d, LANES), lambda i: (0, 0))],
        out_specs=[pl.BlockSpec((tm, d // 2), lambda i: (i, 0)),
                   pl.BlockSpec((tm, LANES), lambda i: (i, 0)),
                   pl.BlockSpec((tm, LANES), lambda i: (i, 0)),
                   pl.BlockSpec((8, LANES), lambda i: (0, 0))],
        out_shape=[jax.ShapeDtypeStruct((t, d // 2), jnp.uint32), jax.ShapeDtypeStruct((t, LANES), F32),
                   jax.ShapeDtypeStruct((t, LANES), jnp.int32), jax.ShapeDtypeStruct((8, LANES), F32)],
        scratch_shapes=[pltpu.VMEM((8, LANES), F32)],
        compiler_params=_cparams("arbitrary"),
        name="router",
    )(x, nw.reshape(1, d), mod, mod, rw_pad)


def _row_copy(src, src_row, dst, dst_row, sem):
    return pltpu.make_async_copy(src.at[pl.ds(src_row, 1), :], dst.at[pl.ds(dst_row, 1), :], sem)


def _dispatch_kernel(pos_ref, ends_ref, h_ref, xs_ref, zero_ref, sem, zsem):
    nt = h_ref.shape[0]
    tm = zero_ref.shape[0]

    @pl.when(pl.program_id(0) == 0)
    def _():
        zero_ref[...] = jnp.zeros_like(zero_ref)

        def last_tile_copy(e):
            row0 = pl.multiple_of((ends_ref[e] - 1) * tm, tm)
            return pltpu.make_async_copy(zero_ref, xs_ref.at[pl.ds(row0, tm), :], zsem)

        def has_rows(e):
            return ends_ref[e] > (ends_ref[e - 1] if e else 0)

        n_tiles = xs_ref.shape[0] // tm

        def spare_tile_copy(k):
            row0 = pl.multiple_of((ends_ref[N_EXPERTS - 1] + k) * tm, tm)
            return pltpu.make_async_copy(zero_ref, xs_ref.at[pl.ds(row0, tm), :], zsem)

        def is_spare(k):
            return ends_ref[N_EXPERTS - 1] + k < n_tiles

        for e in range(N_EXPERTS):
            @pl.when(has_rows(e))
            def _():
                last_tile_copy(e).start()

            @pl.when(is_spare(e))
            def _():
                spare_tile_copy(e).start()
        for e in range(N_EXPERTS):
            @pl.when(has_rows(e))
            def _():
                last_tile_copy(e).wait()

            @pl.when(is_spare(e))
            def _():
                spare_tile_copy(e).wait()

    def start(r, carry):
        for k in range(TOP_K):
            _row_copy(h_ref, r, xs_ref, pos_ref[0, k * nt + r], sem).start()
        return carry

    def wait(r, carry):
        for k in range(TOP_K):
            _row_copy(h_ref, r, xs_ref, pos_ref[0, k * nt + r], sem).wait()
        return carry

    lax.fori_loop(0, nt, start, 0, unroll=8)
    lax.fori_loop(0, nt, wait, 0, unroll=8)


def moe_dispatch(h, pos, tile_ends, rows):
    t, d = h.shape
    nt = MOE_TOKEN_TILE
    return pl.pallas_call(
        _dispatch_kernel,
        grid=(t // nt,),
        in_specs=[pl.BlockSpec((None, 1, TOP_K * nt), lambda i: (i, 0, 0), memory_space=pltpu.SMEM),
                  pl.BlockSpec(memory_space=pltpu.SMEM),
                  pl.BlockSpec((nt, d), lambda i: (i, 0))],
        out_specs=pl.BlockSpec(memory_space=pl.ANY),
        out_shape=jax.ShapeDtypeStruct((rows, d), h.dtype),
        scratch_shapes=[pltpu.VMEM((MOE_ROW_TILE, d), h.dtype), pltpu.SemaphoreType.DMA(()),
                        pltpu.SemaphoreType.DMA(())],
        compiler_params=_cparams("arbitrary"),
        name="moe_dispatch",
    )(pos, tile_ends, h)


def _new_weight_tile(te_ref, i):
    return jnp.logical_or(i == 0, te_ref[i] != te_ref[jnp.maximum(i - 1, 0)])


def _gmm_up_kernel(te_ref, nu_ref, xs_ref, w1_ref, w3_ref, o_ref, w1b_ref, w3b_ref):
    i = pl.program_id(1)
    used = i < nu_ref[0]

    @pl.when(_new_weight_tile(te_ref, i))
    def _():
        w1b_ref[...] = w1_ref[...].astype(BF16)
        w3b_ref[...] = w3_ref[...].astype(BF16)

    @pl.when(used)
    def _():
        a = _unpack_bf16_pairs(xs_ref[...])
        o_ref[...] = (_silu(_dot(a, w1b_ref[...])) * _dot(a, w3b_ref[...])).astype(o_ref.dtype)

    @pl.when(jnp.logical_not(used))
    def _():
        o_ref[...] = jnp.zeros_like(o_ref)


def gmm_up(xs, w1, w3, tile_expert, n_used, *, tn):
    rows = xs.shape[0]
    _, d, n = w1.shape
    tm = MOE_ROW_TILE
    wspec = pl.BlockSpec((None, d, tn), lambda j, i, te, nu: (te[i], 0, j))
    return pl.pallas_call(
        _gmm_up_kernel,
        grid_spec=pltpu.PrefetchScalarGridSpec(
            num_scalar_prefetch=2,
            grid=(n // tn, rows // tm),
            in_specs=[pl.BlockSpec((tm, d // 2), lambda j, i, te, nu: (jnp.minimum(i, nu[0] - 1), 0)),
                      wspec, wspec],
            out_specs=pl.BlockSpec((tm, tn), lambda j, i, te, nu: (i, j)),
            scratch_shapes=[pltpu.VMEM((d, tn), BF16), pltpu.VMEM((d, tn), BF16)]),
        out_shape=jax.ShapeDtypeStruct((rows, n), BF16),
        compiler_params=_cparams("arbitrary", "arbitrary"),
        name="gmm_up",
    )(tile_expert, n_used, xs, w1, w3)


def _gmm_down_kernel(te_ref, nu_ref, f_ref, w_ref, o_ref, wb_ref):
    i = pl.program_id(1)
    used = i < nu_ref[0]

    @pl.when(_new_weight_tile(te_ref, i))
    def _():
        wb_ref[...] = w_ref[...].astype(BF16)

    @pl.when(used)
    def _():
        o_ref[...] = _dot(f_ref[...], wb_ref[...])

    @pl.when(jnp.logical_not(used))
    def _():
        o_ref[...] = jnp.zeros_like(o_ref)


def gmm_down(f, w2, tile_expert, n_used, *, tn):
    rows, k = f.shape
    n = w2.shape[2]
    tm = MOE_ROW_TILE
    return pl.pallas_call(
        _gmm_down_kernel,
        grid_spec=pltpu.PrefetchScalarGridSpec(
            num_scalar_prefetch=2,
            grid=(n // tn, rows // tm),
            in_specs=[pl.BlockSpec((tm, k), lambda j, i, te, nu: (jnp.minimum(i, nu[0] - 1), 0)),
                      pl.BlockSpec((None, k, tn), lambda j, i, te, nu: (te[i], 0, j))],
            out_specs=pl.BlockSpec((tm, tn), lambda j, i, te, nu: (i, j)),
            scratch_shapes=[pltpu.VMEM((k, tn), BF16)]),
        out_shape=jax.ShapeDtypeStruct((rows, n), F32),
        compiler_params=_cparams("arbitrary", "arbitrary"),
        name="gmm_down",
    )(tile_expert, n_used, f, w2)


def _combine_kernel(final, pos_ref, nxt_ref, x_ref, tw_ref, g_ref, fw_ref, ys_ref, o_ref, y_ref, sems):
    nt = x_ref.shape[0]
    i = pl.program_id(0)
    slot = lax.rem(i, 2)

    def gather(p_ref, s):
        def copies(r):
            return [_row_copy(ys_ref, p_ref[0, k * nt + r], y_ref.at[s, k], r, sems.at[s])
                    for k in range(TOP_K)]
        return copies

    def start_all(copies):
        def body(r, carry):
            for c in copies(r):
                c.start()
            return carry
        lax.fori_loop(0, nt, body, 0, unroll=8)

    def wait_all(copies):
        def body(r, carry):
            for c in copies(r):
                c.wait()
            return carry
        lax.fori_loop(0, nt, body, 0, unroll=8)

    @pl.when(i == 0)
    def _():
        start_all(gather(pos_ref, 0))

    @pl.when(i + 1 < pl.num_programs(0))
    def _():
        start_all(gather(nxt_ref, 1 - slot))

    wait_all(gather(pos_ref, slot))
    tw = tw_ref[...]
    x = x_ref[...] + g_ref[...] * (tw[:, 0:1] * y_ref[slot, 0] + tw[:, 1:2] * y_ref[slot, 1])
    if final:
        x = x * lax.rsqrt(jnp.mean(x * x, axis=-1, keepdims=True) + EPS) * fw_ref[...]
    o_ref[...] = x


def moe_combine(x, ys, pos, tw, mod, g_j, fw, *, layer, tiles_per_row, row0, final):
    t, d = x.shape
    nt = MOE_TOKEN_TILE
    n = t // nt
    pos_spec = lambda index: pl.BlockSpec((None, 1, TOP_K * nt), index, memory_space=pltpu.SMEM)
    return pl.pallas_call(
        functools.partial(_combine_kernel, final),
        grid=(n,),
        in_specs=[pos_spec(lambda i: (i, 0, 0)),
                  pos_spec(lambda i: (jnp.minimum(i + 1, n - 1), 0, 0)),
                  pl.BlockSpec((nt, d), lambda i: (i, 0)),
                  pl.BlockSpec((nt, LANES), lambda i: (i, 0)),
                  pl.BlockSpec((None, None, None, 1, d), _mod_index(layer, g_j, tiles_per_row, row0)),
                  pl.BlockSpec((1, d), lambda i: (0, 0)),
                  pl.BlockSpec(memory_space=pl.ANY)],
        out_specs=pl.BlockSpec((nt, d), lambda i: (i, 0)),
        out_shape=jax.ShapeDtypeStruct((t, d), F32),
        scratch_shapes=[pltpu.VMEM((2, TOP_K, nt, d), F32), pltpu.SemaphoreType.DMA((2,))],
        compiler_params=_cparams("arbitrary"),
        name="moe_combine",
    )(pos, pos, x, tw, mod, fw.reshape(1, d), ys)


def _moe_plan(route, counts, t):
    tm = MOE_ROW_TILE
    nt = MOE_TOKEN_TILE
    n_tiles = TOP_K * t // tm + N_EXPERTS
    cnt = counts[0, :N_EXPERTS].astype(jnp.int32)
    tiles = (cnt + tm - 1) // tm
    ends = jnp.cumsum(tiles)
    offs = (ends - tiles) * tm
    pos = [jnp.take(offs, route[:, k]) + route[:, TOP_K + k] for k in range(TOP_K)]
    pos = jnp.concatenate([p.reshape(t // nt, nt) for p in pos], axis=1).reshape(t // nt, 1, TOP_K * nt)
    tile_expert = jnp.sum(jnp.arange(n_tiles, dtype=jnp.int32)[:, None] >= ends[None, :], axis=1)
    tile_expert = jnp.minimum(tile_expert, N_EXPERTS - 1).astype(jnp.int32)
    ends = ends.astype(jnp.int32)
    return pos, tile_expert, ends, ends[-1:], n_tiles * tm


def _hyena_filter_kernel(z_ref, w1_ref, b1_ref, fr_ref, w2_ref, b2_ref, w3_ref, t_ref, dec_ref, o_ref):
    fr = fr_ref[...]
    h = jnp.sin(fr * (_dot_f32(z_ref[...], w1_ref[...]) + b1_ref[...]))
    h = jnp.sin(fr * (_dot_f32(h, w2_ref[...]) + b2_ref[...]))
    o_ref[...] = _dot_f32(h, w3_ref[...]) * jnp.exp(-t_ref[...] * jnp.abs(dec_ref[...]))


def hyena_filter(seq, w1, b1, freq, w2, b2, w3, decay, tl=256):
    t = jnp.linspace(0.0, 1.0, seq, dtype=F32)[:, None]
    pos = jnp.arange(seq, dtype=F32)[:, None]
    bands = jnp.linspace(1e-4, HY_BANDS - 1.0, HY_BANDS, dtype=F32)[None, :]
    ang = (2.0 * math.pi / seq) * pos * bands
    z = jnp.concatenate([t, jnp.cos(ang), -jnp.sin(ang)], axis=-1)
    emb = z.shape[1]
    emb_pad = LANES
    z = jnp.pad(z, ((0, 0), (0, emb_pad - emb)))
    w1p = jnp.pad(w1, ((0, emb_pad - emb), (0, 0)))
    hid = w1.shape[1]
    n = w3.shape[1]
    tl = min(tl, seq)
    full = lambda shape: pl.BlockSpec(shape, lambda i: (0, 0))
    return pl.pallas_call(
        _hyena_filter_kernel,
        grid=(seq // tl,),
        in_specs=[pl.BlockSpec((tl, emb_pad), lambda i: (i, 0)),
                  full((emb_pad, hid)), full((1, hid)), full((1, hid)),
                  full((hid, hid)), full((1, hid)), full((hid, n)),
                  pl.BlockSpec((tl, 1), lambda i: (i, 0)), full((1, n))],
        out_specs=pl.BlockSpec((tl, n), lambda i: (i, 0)),
        out_shape=jax.ShapeDtypeStruct((seq, n), F32),
        compiler_params=_cparams("parallel"),
        name="hyena_filter",
    )(z, w1p, b1.reshape(1, hid), freq.reshape(1, hid), w2, b2.reshape(1, hid), w3, t,
      decay.reshape(1, n))


def _dft_tables(seq, kb):
    n = 2 * seq
    k = jnp.arange(seq, dtype=jnp.int32)[:, None]
    s = jnp.arange(seq, dtype=jnp.int32)[None, :]
    ang = ((k * s) % n).astype(F32) * (2.0 * math.pi / n)
    cos = jnp.cos(ang)
    sin = jnp.sin(ang)
    nyq = jnp.where(s % 2 == 0, 1.0, -1.0).astype(F32)
    is0 = k == 0
    f_re = cos
    f_im = jnp.where(is0, nyq, -sin)
    i_re = jnp.where(is0, 1.0 / n, (2.0 / n) * cos)
    i_im = jnp.where(is0, nyq / n, -(2.0 / n) * sin)
    nkb = seq // kb
    fwd = jnp.concatenate([f_re.reshape(nkb, kb, seq), f_im.reshape(nkb, kb, seq)], axis=1)
    inv = jnp.concatenate([i_re.reshape(nkb, kb, seq), i_im.reshape(nkb, kb, seq)], axis=1)
    return fwd, jnp.swapaxes(inv, 1, 2)


def _spectrum_kernel(kb, f_ref, hf_ref, hb_ref, bias_ref, o_ref):
    f = f_ref[...]
    row = lax.broadcasted_iota(jnp.int32, hb_ref.shape, 0)
    hb0 = jnp.where(row == 0, 0.0, hb_ref[...])
    a = _dot_f32_3pass(f, hf_ref[...])
    b = _dot_f32_3pass(f, hb0)
    orow = lax.broadcasted_iota(jnp.int32, a.shape, 0)
    nyq_slot = (orow == kb) & (pl.program_id(0) == 0)
    o_ref[...] = jnp.where((orow < kb) | nyq_slot, a + b + bias_ref[...], a - b)


def hyena_spectrum(fwd_f32, filt, bias, kb, tc=256):
    nkb, kb2, seq = fwd_f32.shape
    c = filt.shape[1] // 2
    return pl.pallas_call(
        functools.partial(_spectrum_kernel, kb),
        grid=(nkb, c // tc),
        in_specs=[pl.BlockSpec((None, kb2, seq), lambda j, i: (j, 0, 0)),
                  pl.BlockSpec((seq, tc), lambda j, i: (0, i)),
                  pl.BlockSpec((seq, tc), lambda j, i: (0, c // tc + i)),
                  pl.BlockSpec((1, tc), lambda j, i: (0, i))],
        out_specs=pl.BlockSpec((None, kb2, tc), lambda j, i: (j, 0, i)),
        out_shape=jax.ShapeDtypeStruct((nkb, kb2, c), F32),
        compiler_params=_cparams("parallel", "parallel"),
        name="hyena_spectrum",
    )(fwd_f32, filt, filt, bias.reshape(1, c))


def _hyena_conv_kernel(kb, x0_ref, x1_ref, v_ref, cw0_ref, cw1_ref, cwv_ref, cb0_ref, cb1_ref, cbv_ref,
                       f_ref, i_ref, kf_ref, o_ref, u16_ref, acc_ref):
    j = pl.program_id(2)
    seq = x0_ref.shape[0]

    def conv3(x_ref, w_ref, b_ref):
        x = x_ref[...].astype(F32)
        row = lax.broadcasted_iota(jnp.int32, x.shape, 0)
        prev = jnp.where(row == 0, 0.0, pltpu.roll(x, 1, 0))
        nxt = jnp.where(row == seq - 1, 0.0, pltpu.roll(x, seq - 1, 0))
        w = w_ref[...]
        return prev * w[0:1, :] + x * w[1:2, :] + nxt * w[2:3, :] + b_ref[...]

    @pl.when(j == 0)
    def _():
        u16_ref[...] = (conv3(v_ref, cwv_ref, cbv_ref) * conv3(x1_ref, cw1_ref, cb1_ref)).astype(BF16)
        acc_ref[...] = jnp.zeros_like(acc_ref)

    spec = _dot(f_ref[...], u16_ref[...])
    xr, xi = spec[:kb], spec[kb:]
    kf = kf_ref[...]
    kr, ki = kf[:kb], kf[kb:]
    packed = (lax.broadcasted_iota(jnp.int32, xr.shape, 0) == 0) & (j == 0)
    yr = xr * kr - jnp.where(packed, 0.0, xi * ki)
    yi = jnp.where(packed, xi * ki, xr * ki + xi * kr)
    y = jnp.concatenate([yr, yi], axis=0).astype(BF16)
    acc_ref[...] += _dot(i_ref[...], y)

    @pl.when(j == pl.num_programs(2) - 1)
    def _():
        o_ref[...] = (acc_ref[...] * conv3(x0_ref, cw0_ref, cb0_ref)).astype(o_ref.dtype)


def hyena_conv(proj, conv_w, conv_b, fwd, inv, spectrum, *, batch, seq, ct, kb):
    c = HY_WIDTH
    nkb = seq // kb
    ncb = c // ct
    col = lambda part: pl.BlockSpec((seq, ct), lambda b, i, j: (b, part * ncb + i))
    cw = lambda part: pl.BlockSpec((3, ct), lambda b, i, j: (0, part * ncb + i))
    cb = lambda part: pl.BlockSpec((1, ct), lambda b, i, j: (0, part * ncb + i))
    conv_b = conv_b.reshape(1, 3 * c)
    return pl.pallas_call(
        functools.partial(_hyena_conv_kernel, kb),
        grid=(batch, ncb, nkb),
        in_specs=[col(0), col(1), col(2), cw(0), cw(1), cw(2), cb(0), cb(1), cb(2),
                  pl.BlockSpec((None, 2 * kb, seq), lambda b, i, j: (j, 0, 0)),
                  pl.BlockSpec((None, seq, 2 * kb), lambda b, i, j: (j, 0, 0)),
                  pl.BlockSpec((None, 2 * kb, ct), lambda b, i, j: (j, 0, i))],
        out_specs=pl.BlockSpec((seq, ct), lambda b, i, j: (b, i)),
        out_shape=jax.ShapeDtypeStruct((batch * seq, c), BF16),
        scratch_shapes=[pltpu.VMEM((seq, ct), BF16), pltpu.VMEM((seq, ct), F32)],
        compiler_params=_cparams("parallel", "parallel", "arbitrary"),
        name="hyena_conv",
    )(proj, proj, proj, conv_w, conv_w, conv_w, conv_b, conv_b, conv_b, fwd, inv, spectrum)


def _rope(x, cos, sin_a, sin_b):
    return x * cos + pltpu.roll(x, HEAD_DIM - HEAD_DIM // 4, 1) * sin_a + pltpu.roll(x, HEAD_DIM // 4, 1) * sin_b


def _dot_nt(a, b):
    return lax.dot_general(a, b, (((1,), (1,)), ((), ())), preferred_element_type=F32)


LOG2E = 1.0 / math.log(2.0)
LOGIT_SCALE = ATTN_SCALE * LOG2E


def _grouped_softmax_pv(qs, keys, values, sinks, valid, o_ref, score_scale):
    scores = [_dot_nt(q, keys) for q in qs]
    probs, dens = [], []
    for s, sink in zip(scores, sinks):
        if score_scale is not None:
            s = s * score_scale
        if valid is not None:
            s = jnp.where(valid, s, NEG_BIG)
        sink2 = sink * LOG2E
        m = jnp.maximum(jnp.max(s, axis=-1, keepdims=True), sink2)
        p = jnp.exp2(s - m)
        dens.append(jnp.sum(p, axis=-1, keepdims=True) + jnp.exp2(sink2 - m))
        probs.append(p.astype(BF16))
    outs = [_dot(p, values) for p in probs]
    for g, (o, den) in enumerate(zip(outs, dens)):
        o_ref[:, g * HEAD_DIM:(g + 1) * HEAD_DIM] = (o / den).astype(o_ref.dtype)


def _ctx_attn_kernel(sink_ref, q_ref, k_ref, v_ref, o_ref):
    kvh = pl.program_id(1)
    qs = [q_ref[:, g * HEAD_DIM:(g + 1) * HEAD_DIM] for g in range(GROUP)]
    sinks = [sink_ref[kvh * GROUP + g] for g in range(GROUP)]
    _grouped_softmax_pv(qs, k_ref[...].astype(BF16), v_ref[...].astype(BF16), sinks, None, o_ref,
                        LOGIT_SCALE)


def context_attention(proj, proj32, sink, *, batch, seq):
    qw = GROUP * HEAD_DIM
    kv = lambda part: pl.BlockSpec((seq, HEAD_DIM),
                                   lambda b, h: (b, OFF32_KV // HEAD_DIM + part * N_KV_HEADS + h))
    return pl.pallas_call(
        _ctx_attn_kernel,
        grid=(batch, N_KV_HEADS),
        in_specs=[pl.BlockSpec(memory_space=pltpu.SMEM),
                  pl.BlockSpec((seq, qw), lambda b, h: (b, OFF_AQ // qw + h)),
                  kv(0), kv(1)],
        out_specs=pl.BlockSpec((seq, qw), lambda b, h: (b, h)),
        out_shape=jax.ShapeDtypeStruct((batch * seq, ATTN_WIDTH), BF16),
        compiler_params=_cparams("parallel", "parallel"),
        name="context_attention",
    )(sink, proj, proj32, proj32)


def _lat_attn_kernel(seq, sink_ref, q_ref, k_ref, v_ref, ck_ref, cv_ref, cos_ref, sa_ref, sb_ref,
                     o_ref, kr_ref, vb_ref):
    kvh = pl.program_id(1)
    qb = pl.program_id(2)
    blk = q_ref.shape[0]
    nwin = 3 * blk
    past = ck_ref.shape[0]

    @pl.when(qb == 0)
    def _():
        kr_ref[...] = _rope(k_ref[...], cos_ref[...], sa_ref[...], sb_ref[...]).astype(BF16)
        vb_ref[...] = v_ref[...].astype(BF16)

    start = pl.multiple_of(jnp.clip((qb - 1) * blk, 0, seq - nwin), blk)
    keys = jnp.concatenate([kr_ref[pl.ds(start, nwin), :], ck_ref[...].astype(BF16)], axis=0)
    values = jnp.concatenate([vb_ref[pl.ds(start, nwin), :], cv_ref[...].astype(BF16)], axis=0)
    rows = pl.ds(pl.multiple_of(qb * blk, blk), blk)
    cos, sa, sb = cos_ref[rows, :], sa_ref[rows, :], sb_ref[rows, :]
    qpos = qb * blk + lax.broadcasted_iota(jnp.int32, (blk, nwin + past), 0)
    col = lax.broadcasted_iota(jnp.int32, (blk, nwin + past), 1)
    valid = (col >= nwin) | (jnp.abs(qpos - (start + col)) <= WINDOW)
    qs = [(_rope(q_ref[:, g * HEAD_DIM:(g + 1) * HEAD_DIM].astype(F32), cos, sa, sb) * LOGIT_SCALE).astype(BF16)
          for g in range(GROUP)]
    sinks = [sink_ref[kvh * GROUP + g] for g in range(GROUP)]
    _grouped_softmax_pv(qs, keys, values, sinks, valid, o_ref, None)


def _rope_tables(seq):
    rows = seq // GRID_W
    row = jnp.repeat(jnp.arange(rows, dtype=F32), GRID_W)
    col = jnp.tile(jnp.arange(GRID_W, dtype=F32), rows)
    quarter = HEAD_DIM // 4
    inv = ROPE_BASE ** (-jnp.arange(quarter, dtype=F32) / quarter)
    ar = row[:, None] * inv
    ac = col[:, None] * inv
    ang = jnp.concatenate([ar, ar, ac, ac], axis=-1)
    cos, sin = jnp.cos(ang), jnp.sin(ang)
    first = (jnp.arange(HEAD_DIM) % (2 * quarter)) < quarter
    return cos, jnp.where(first, -sin, 0.0), jnp.where(first, 0.0, sin)


def latent_attention(proj, proj32, cache_k, cache_v, sink, *, layer, batch, seq, blk=128):
    qw = GROUP * HEAD_DIM
    past = cache_k.shape[2]
    nqb = seq // blk
    cos, sa, sb = _rope_tables(seq)
    table = pl.BlockSpec((seq, HEAD_DIM), lambda b, h, i: (0, 0))
    cache = pl.BlockSpec((None, None, past, HEAD_DIM), lambda b, h, i: (b, layer, 0, h))
    kv = lambda part: pl.BlockSpec((seq, HEAD_DIM),
                                   lambda b, h, i: (b, OFF32_KV // HEAD_DIM + part * N_KV_HEADS + h))
    return pl.pallas_call(
        functools.partial(_lat_attn_kernel, seq),
        grid=(batch, N_KV_HEADS, nqb),
        in_specs=[pl.BlockSpec(memory_space=pltpu.SMEM),
                  pl.BlockSpec((blk, qw), lambda b, h, i: (b * nqb + i, OFF_AQ // qw + h)),
                  kv(0), kv(1),
                  cache, cache, table, table, table],
        out_specs=pl.BlockSpec((blk, qw), lambda b, h, i: (b * nqb + i, h)),
        out_shape=jax.ShapeDtypeStruct((batch * seq, ATTN_WIDTH), BF16),
        scratch_shapes=[pltpu.VMEM((seq, HEAD_DIM), BF16), pltpu.VMEM((seq, HEAD_DIM), BF16)],
        compiler_params=_cparams("parallel", "parallel", "arbitrary"),
        name="latent_attention",
    )(sink, proj, proj32, proj32, cache_k, cache_v, cos, sa, sb)


_HG_LEVELS = tuple(HG_CHUNK >> (i + 1) for i in range(int(math.log2(HG_CHUNK))))


def _hgrn_tables():
    c = HG_CHUNK
    t = np.arange(c)[:, None]
    u = np.arange(c)[None, :]
    blocks = [(u <= t), (u > t)]
    for m in _HG_LEVELS:
        ref = (t // (2 * m)) * (2 * m) + m - 1
        second = (t % (2 * m)) >= m
        blocks.append(np.where(second, (u > ref) & (u <= t), (u > t) & (u <= ref)))
    fwd = np.concatenate(blocks, axis=0).astype(np.float32)
    bwd = np.concatenate([b[::-1, ::-1] for b in blocks], axis=0).astype(np.float32)
    s = u
    level = np.full((c, c), len(_HG_LEVELS) + 1, np.int32)
    level[t == s] = len(_HG_LEVELS)
    for i, m in enumerate(_HG_LEVELS):
        hit = (t // (2 * m) == s // (2 * m)) & ((t % (2 * m)) >= m) & ((s % (2 * m)) < m)
        level[hit] = i
    a = np.stack([fwd, bwd])
    a = np.concatenate([a, a], axis=2)
    lv = np.stack([level, level.T])
    return jnp.asarray(a, BF16), jnp.asarray(lv, jnp.int32)


def _hgrn_kernel(layer, has_s0, seq, heads, *refs):
    if has_s0:
        (q_ref, ff_ref, fb_ref, i_ref, g_ref, lb_ref, nw_ref, a_ref, lv_ref, s0_ref,
         y_ref, sfin_ref, of_ref, ob_ref, st_ref) = refs
    else:
        (q_ref, ff_ref, fb_ref, i_ref, g_ref, lb_ref, nw_ref, a_ref, lv_ref,
         y_ref, sfin_ref, of_ref, ob_ref, st_ref) = refs
    c = HG_CHUNK
    nlev = len(_HG_LEVELS)
    nc = seq // c

    lbs = lb_ref[...]
    mx = jnp.max(lbs, axis=0, keepdims=True)
    ex = jnp.exp(lbs - mx)
    sm = ex / jnp.sum(ex, axis=0, keepdims=True)
    lb = jnp.zeros(sm.shape[1:], F32)
    for j in range(1, layer + 1):
        lb = lb + sm[j]

    for d in range(2):
        for h in range(heads):
            if has_s0:
                st_ref[d, h] = s0_ref[d, h].T
            else:
                st_ref[d, h] = jnp.zeros((HG_DV, HG_DK), F32)

    def decay_exponents(d, ci):
        rows = pl.ds(pl.multiple_of(ci * c, c), c)
        fpre = (ff_ref if d == 0 else fb_ref)[rows, :]
        lbd = lb[d:d + 1, :]
        f = jnp.maximum(lbd, LB_FLOOR) + (1.0 - lbd) * jax.nn.sigmoid(fpre)
        log2f = jnp.log(f) * (1.0 / math.log(2.0))
        args = _dot(a_ref[d], jnp.concatenate(_split3(log2f)[:2], axis=0))
        chains = []
        for h in range(heads):
            cols = slice(h * HG_DK, (h + 1) * HG_DK)
            q = _silu(q_ref[rows, cols].astype(F32)) * (HG_DK ** -0.5)
            chains.append(dict(d=d, h=h, rows=rows, cols=cols, q=q, k=1.0 - f[:, cols],
                               v=i_ref[rows, cols].astype(BF16), args=args[:, cols]))
        return chains

    def level_products(s):
        q, k = s['q'], s['k']
        e = jnp.exp2(s['args'])
        s['q_in'] = (q * e[0:c]).astype(BF16)
        s['k_out'] = (k * e[c:2 * c]).astype(BF16)
        last = (c - 1) if s['d'] == 0 else 0
        s['total'] = e[last:last + 1]
        prods = [_dot_nt((q * e[(2 + i) * c:(3 + i) * c]).astype(BF16), (k * e[(2 + i) * c:(3 + i) * c]).astype(BF16))
                 for i in range(nlev)]
        s['prods'] = prods + [_dot_nt(q.astype(BF16), k.astype(BF16))]
        return s

    def outputs_and_state(s, o_ref):
        d, h = s['d'], s['h']
        lv = lv_ref[d]
        att = jnp.zeros((c, c), F32)
        for i in range(nlev + 1):
            att = jnp.where(lv == i, s['prods'][i], att)
        st = st_ref[d, h]
        o_ref[s['rows'], s['cols']] = _dot_nt(s['q_in'], st.astype(BF16)) + _dot(att.astype(BF16), s['v'])
        upd = lax.dot_general(s['v'], s['k_out'], (((0,), (0,)), ((), ())), preferred_element_type=F32)
        st_ref[d, h] = st * s['total'] + upd

    def body(ci, carry):
        stage = decay_exponents(0, ci) + decay_exponents(1, nc - 1 - ci)
        stage = [level_products(s) for s in stage]
        for s in stage:
            outputs_and_state(s, of_ref if s['d'] == 0 else ob_ref)
        return carry

    lax.fori_loop(0, nc, body, 0, unroll=4)

    for h in range(heads):
        cols = slice(h * HG_DK, (h + 1) * HG_DK)
        o = of_ref[:, cols] + ob_ref[:, cols]
        o = o * lax.rsqrt(jnp.mean(o * o, axis=-1, keepdims=True) + EPS) * nw_ref[...] * _silu(g_ref[:, cols].astype(F32))
        y_ref[:, cols] = o.astype(y_ref.dtype)
        for d in range(2):
            sfin_ref[d, h] = st_ref[d, h].T


HG_HEADS_PER_STEP = 2


def hgrn2_mix(proj, proj32, hg_lb, norm_w, s0, *, layer, batch, seq):
    a_tab, lv_tab = _hgrn_tables()
    depth = hg_lb.shape[0]
    hps = HG_HEADS_PER_STEP
    w = hps * HG_DK
    col = lambda off: pl.BlockSpec((seq, w), lambda b, h: (b, off // w + h))
    state = pl.BlockSpec((None, 2, hps, HG_DK, HG_DV), lambda b, h: (b, 0, h, 0, 0))
    in_specs = [col(OFF_HQ), col(OFF32_FF), col(OFF32_FB), col(OFF_HI), col(OFF_HG),
                pl.BlockSpec((depth, 2, w), lambda b, h: (0, 0, h)),
                pl.BlockSpec((1, HG_DV), lambda b, h: (0, 0)),
                pl.BlockSpec(a_tab.shape, lambda b, h: (0, 0, 0)),
                pl.BlockSpec(lv_tab.shape, lambda b, h: (0, 0, 0))]
    args = [proj, proj32, proj32, proj, proj, hg_lb, norm_w.reshape(1, HG_DV), a_tab, lv_tab]
    if s0 is not None:
        in_specs.append(state)
        args.append(s0)
    return pl.pallas_call(
        functools.partial(_hgrn_kernel, layer, s0 is not None, seq, hps),
        grid=(batch, HG_HEADS // hps),
        in_specs=in_specs,
        out_specs=[pl.BlockSpec((seq, w), lambda b, h: (b, h)), state],
        out_shape=[jax.ShapeDtypeStruct((batch * seq, HG_WIDTH), BF16),
                   jax.ShapeDtypeStruct((batch, 2, HG_HEADS, HG_DK, HG_DV), F32)],
        scratch_shapes=[pltpu.VMEM((seq, w), F32), pltpu.VMEM((seq, w), F32),
                        pltpu.VMEM((2, hps, HG_DV, HG_DK), F32)],
        compiler_params=_cparams("parallel", "parallel"),
        name="hgrn2_mix",
    )(*args)


ROW_TILE = 1024
COL_TILE = 512
OUT_PROJ_COL_TILE = 1024
ROUTER_TILE = 512
HYENA_CHANNEL_TILE = 512
HYENA_FREQ_BLOCK = 512


def _trunk_layer(x, l, p, *, batch, seq, mod, row0, ctx, final):
    t, d = x.shape
    tm = ROW_TILE
    rows_per_cond = t if ctx is None else seq
    geo = lambda tile: dict(layer=l, tiles_per_row=rows_per_cond // tile, row0=row0)

    proj, proj32 = ln_mod_matmul(x, p['norm1_w'][l], mod, (1, 0), p['w_in'][l], tm=tm, **geo(tm))

    filt = hyena_filter(seq, p['hy_w1'][l], p['hy_b1'][l], p['hy_freq'][l], p['hy_w2'][l], p['hy_b2'][l],
                        p['hy_w3'][l], p['hy_decay'][l])
    kb = min(seq, HYENA_FREQ_BLOCK)
    fwd, inv = _dft_tables(seq, kb)
    spectrum = hyena_spectrum(fwd, filt, p['hy_bias'][l], kb)
    ya = hyena_conv(proj, p['hy_conv_w'][l], p['hy_conv_b'][l], fwd.astype(BF16), inv.astype(BF16),
                    spectrum, batch=batch, seq=seq, ct=HYENA_CHANNEL_TILE, kb=kb)

    if ctx is None:
        yb = context_attention(proj, proj32, p['attn_sink'][l], batch=batch, seq=seq)
        s0 = None
    else:
        cache_k, cache_v, s0 = ctx
        yb = latent_attention(proj, proj32, cache_k, cache_v, p['attn_sink'][l], layer=l, batch=batch,
                              seq=seq)
        s0 = s0[:, l]

    yc, s_fin = hgrn2_mix(proj, proj32, p['hg_lb'], p['hg_norm_w'][l], s0, layer=l, batch=batch, seq=seq)

    mixed = branch_merge(ya, yb, yc, proj, p['w_branch_a'][l], p['w_branch_b'][l], p['w_branch_c'][l],
                         tm=tm, tn=COL_TILE)
    x = matmul_gated_residual(mixed, p['w_out'][l], x, mod, 2, tm=tm, tn=OUT_PROJ_COL_TILE, **geo(tm))

    j = l // 2
    if l % 2 == 0:
        f = ln_mod_glu(x, p['norm2_w'][l], mod, (4, 3), p['ffn_w1'][j], p['ffn_w3'][j], tm=tm, tn=COL_TILE,
                       **geo(tm))
        x = matmul_gated_residual(f, p['ffn_w2'][j], x, mod, 5, tm=tm, tn=COL_TILE, **geo(tm))
        if final:
            x = final_norm(x, p['final_norm_w'], tm=ROUTER_TILE)
    else:
        h2, tw, route, counts = router(x, p['norm2_w'][l], mod, (4, 3), p['router_w'][j], tm=ROUTER_TILE,
                                       **geo(ROUTER_TILE))
        pos, tile_expert, tile_ends, n_used, rows = _moe_plan(route, counts, t)
        xs = moe_dispatch(h2, pos, tile_ends, rows)
        f = gmm_up(xs, p['moe_w1'][j], p['moe_w3'][j], tile_expert, n_used, tn=COL_TILE)
        ys = gmm_down(f, p['moe_w2'][j], tile_expert, n_used, tn=COL_TILE)
        x = moe_combine(x, ys, pos, tw, mod, 5, p['final_norm_w'], final=final, **geo(MOE_TOKEN_TILE))
    k = proj32[:, OFF32_KV:OFF32_KV + KV_WIDTH]
    v = proj32[:, OFF32_KV + KV_WIDTH:OFF32_KV + 2 * KV_WIDTH]
    return x, k, v, s_fin


def kernel(x_prompt, x_sample, c, cache_k, cache_v, state_hgrn, c_ctx, ada_w, ada_b, norm1_w, norm2_w, w_in,
           hy_conv_w, hy_conv_b, hy_w1, hy_b1, hy_freq, hy_w2, hy_b2, hy_w3, hy_decay, hy_bias, attn_sink,
           hg_lb, hg_norm_w, w_branch_a, w_branch_b, w_branch_c, w_out, ffn_w1, ffn_w3, ffn_w2, router_w,
           moe_w1, moe_w3, moe_w2, final_norm_w):
    batch, seq, d = x_prompt.shape
    dbatch, dseq, _ = x_sample.shape
    depth = ada_w.shape[0]
    bf = lambda a: a.astype(BF16)
    p = dict(norm1_w=norm1_w, norm2_w=norm2_w, w_in=bf(w_in), hy_conv_w=hy_conv_w, hy_conv_b=hy_conv_b,
             hy_w1=hy_w1, hy_b1=hy_b1, hy_freq=hy_freq, hy_w2=hy_w2, hy_b2=hy_b2, hy_w3=hy_w3,
             hy_decay=hy_decay, hy_bias=hy_bias, attn_sink=attn_sink, hg_lb=hg_lb, hg_norm_w=hg_norm_w,
             w_branch_a=bf(w_branch_a), w_branch_b=bf(w_branch_b), w_branch_c=bf(w_branch_c),
             w_out=bf(w_out), ffn_w1=bf(ffn_w1), ffn_w3=bf(ffn_w3), ffn_w2=bf(ffn_w2), router_w=router_w,
             moe_w1=moe_w1, moe_w3=moe_w3, moe_w2=moe_w2, final_norm_w=final_norm_w)

    nrows = 16
    cond = jnp.zeros((nrows, d), F32).at[:dbatch].set(c).at[dbatch].set(c_ctx)
    mod = ada_modulation(cond, ada_w, ada_b).reshape(depth, nrows, 6, 1, d)

    xp = x_prompt.reshape(batch * seq, d)
    ks, vs, ss = [], [], []
    for l in range(depth):
        xp, k_l, v_l, s_l = _trunk_layer(xp, l, p, batch=batch, seq=seq, mod=mod, row0=dbatch, ctx=None,
                                         final=(l == depth - 1))
        ks.append(k_l.reshape(batch, seq, N_KV_HEADS, HEAD_DIM))
        vs.append(v_l.reshape(batch, seq, N_KV_HEADS, HEAD_DIM))
        ss.append(s_l)
    y_prompt = xp.reshape(batch, seq, d)
    new_cache_k = jnp.stack(ks, axis=1)
    new_cache_v = jnp.stack(vs, axis=1)
    new_state = jnp.stack(ss, axis=1)

    past = cache_k.shape[2]
    ck = cache_k.reshape(dbatch, depth, past, KV_WIDTH)
    cv = cache_v.reshape(dbatch, depth, past, KV_WIDTH)
    xs = x_sample.reshape(dbatch * dseq, d)
    for l in range(depth):
        xs, _, _, _ = _trunk_layer(xs, l, p, batch=dbatch, seq=dseq, mod=mod, row0=0,
                                   ctx=(ck, cv, state_hgrn), final=(l == depth - 1))
    y_sample = xs.reshape(dbatch, dseq, d)
    return (y_prompt, y_sample, new_cache_k, new_cache_v, new_state)
```

```python
import functools
import math

import numpy as np
import jax
import jax.numpy as jnp
from jax import lax
from jax.experimental import pallas as pl
from jax.experimental.pallas import tpu as pltpu

F32 = jnp.float32
BF16 = jnp.bfloat16

VMEM_LIMIT_BYTES = 56 * 1024 * 1024
LANES = 128

EPS = 1e-6
NEG_BIG = -1e30
LB_FLOOR = 1e-30
GRID_W = 64
HY_WIDTH = 1024
HY_BANDS = 16
N_HEADS = 8
N_KV_HEADS = 2
GROUP = N_HEADS // N_KV_HEADS
HEAD_DIM = 128
ATTN_WIDTH = N_HEADS * HEAD_DIM
KV_WIDTH = N_KV_HEADS * HEAD_DIM
WINDOW = 128
ROPE_BASE = 10000.0
ATTN_SCALE = HEAD_DIM ** -0.5
HG_HEADS = 8
HG_DK = 128
HG_DV = 128
HG_WIDTH = HG_HEADS * HG_DK
HG_CHUNK = 128
N_EXPERTS = 8
TOP_K = 2

OFF_HY = 0
OFF_AQ = 3 * HY_WIDTH
OFF_AK = OFF_AQ + ATTN_WIDTH
OFF_AV = OFF_AK + KV_WIDTH
OFF_HQ = OFF_AV + KV_WIDTH
OFF_FF = OFF_HQ + HG_WIDTH
OFF_FB = OFF_FF + HG_WIDTH
OFF_HI = OFF_FB + HG_WIDTH
OFF_HG = OFF_HI + HG_WIDTH
OFF_MA = OFF_HG + HG_WIDTH


def _cparams(*sem):
    return pltpu.CompilerParams(dimension_semantics=sem, vmem_limit_bytes=VMEM_LIMIT_BYTES)


def _split3(x):
    hi = x.astype(BF16)
    r1 = x - hi.astype(F32)
    mid = r1.astype(BF16)
    lo = (r1 - mid.astype(F32)).astype(BF16)
    return hi, mid, lo


def _dot(a, b):
    return jnp.dot(a, b, preferred_element_type=F32)


def _dot_f32(a, b):
    a0, a1, a2 = _split3(a)
    b0, b1, b2 = _split3(b)
    return (_dot(a0, b0) + (_dot(a0, b1) + _dot(a1, b0))
            + (_dot(a0, b2) + _dot(a1, b1) + _dot(a2, b0)))


def _dot_f32_3pass(a, b):
    a0, a1, _ = _split3(a)
    b0, b1, _ = _split3(b)
    return _dot(a0, b0) + (_dot(a0, b1) + _dot(a1, b0))


def _silu(x):
    return x * jax.nn.sigmoid(x)


def _ada_kernel(c_ref, w_ref, b_ref, o_ref):
    o_ref[...] = _dot_f32_3pass(_silu(c_ref[...]), w_ref[...]) + b_ref[...]


def ada_modulation(cond, ada_w, ada_b, tn=512):
    depth, d, n = ada_w.shape
    rows = cond.shape[0]
    return pl.pallas_call(
        _ada_kernel,
        grid=(depth, n // tn),
        in_specs=[pl.BlockSpec((rows, d), lambda l, j: (0, 0)),
                  pl.BlockSpec((None, d, tn), lambda l, j: (l, 0, j)),
                  pl.BlockSpec((None, 1, tn), lambda l, j: (l, 0, j))],
        out_specs=pl.BlockSpec((None, rows, tn), lambda l, j: (l, 0, j)),
        out_shape=jax.ShapeDtypeStruct((depth, rows, n), F32),
        compiler_params=_cparams("parallel", "parallel"),
        name="ada_modulation",
    )(cond, ada_w, ada_b.reshape(depth, 1, n))


def _norm_modulate(x, nw, sc, sh):
    ms = jnp.mean(x * x, axis=-1, keepdims=True)
    y = x * lax.rsqrt(ms + EPS) * nw
    return y * (1.0 + sc) + sh


NORM_ROWS = 32


def _fill_norm_modulate(h_ref, x_ref, nw_ref, sc_ref, sh_ref):
    def strip(r, carry):
        rows = pl.ds(pl.multiple_of(r * NORM_ROWS, NORM_ROWS), NORM_ROWS)
        h_ref[rows, :] = _norm_modulate(x_ref[rows, :], nw_ref[...], sc_ref[...], sh_ref[...]).astype(h_ref.dtype)
        return carry

    lax.fori_loop(0, x_ref.shape[0] // NORM_ROWS, strip, 0, unroll=4)


def _mod_index(layer, j, tiles_per_row, row0):
    def index(i, *_):
        return (layer, row0 + i // tiles_per_row, j, 0, 0)
    return index


PROJ_TN = 512
F32_TILES = (OFF_AK // PROJ_TN,) + tuple(range(OFF_FF // PROJ_TN, OFF_HI // PROJ_TN))
OFF32_KV = 0
OFF32_FF = PROJ_TN
OFF32_FB = OFF32_FF + HG_WIDTH
assert OFF_AK % PROJ_TN == 0 and 2 * KV_WIDTH == PROJ_TN and OFF_FF % PROJ_TN == 0 and OFF_HI % PROJ_TN == 0


def _f32_tile_slot(j):
    return jnp.maximum(sum((j >= tile).astype(jnp.int32) for tile in F32_TILES) - 1, 0)


def _ln_mm_kernel(x_ref, nw_ref, sc_ref, sh_ref, w_ref, o_ref, o32_ref, h_ref):
    j = pl.program_id(1)

    @pl.when(j == 0)
    def _():
        _fill_norm_modulate(h_ref, x_ref, nw_ref, sc_ref, sh_ref)

    acc = _dot(h_ref[...], w_ref[...])
    o_ref[...] = acc.astype(o_ref.dtype)
    keep = functools.reduce(jnp.logical_or, [j == tile for tile in F32_TILES])

    @pl.when(keep)
    def _():
        o32_ref[...] = acc


def ln_mod_matmul(x, nw, mod, mod_idx, w, *, layer, tiles_per_row, row0, tm):
    t, d = x.shape
    n = w.shape[1]
    tn = PROJ_TN
    sc_j, sh_j = mod_idx
    mspec = lambda j: pl.BlockSpec((None, None, None, 1, d), _mod_index(layer, j, tiles_per_row, row0))
    return pl.pallas_call(
        _ln_mm_kernel,
        grid=(t // tm, n // tn),
        in_specs=[pl.BlockSpec((tm, d), lambda i, j: (i, 0)),
                  pl.BlockSpec((1, d), lambda i, j: (0, 0)),
                  mspec(sc_j), mspec(sh_j),
                  pl.BlockSpec((d, tn), lambda i, j: (0, j))],
        out_specs=[pl.BlockSpec((tm, tn), lambda i, j: (i, j)),
                   pl.BlockSpec((tm, tn), lambda i, j: (i, _f32_tile_slot(j)))],
        out_shape=[jax.ShapeDtypeStruct((t, n), BF16),
                   jax.ShapeDtypeStruct((t, len(F32_TILES) * tn), F32)],
        scratch_shapes=[pltpu.VMEM((tm, d), BF16)],
        compiler_params=_cparams("parallel", "arbitrary"),
        name="ln_mod_matmul",
    )(x, nw.reshape(1, d), mod, mod, w)


def _ln_glu_kernel(x_ref, nw_ref, sc_ref, sh_ref, w1_ref, w3_ref, o_ref, h_ref):
    @pl.when(pl.program_id(1) == 0)
    def _():
        _fill_norm_modulate(h_ref, x_ref, nw_ref, sc_ref, sh_ref)

    h = h_ref[...]
    o_ref[...] = (_silu(_dot(h, w1_ref[...])) * _dot(h, w3_ref[...])).astype(o_ref.dtype)


def ln_mod_glu(x, nw, mod, mod_idx, w1, w3, *, layer, tiles_per_row, row0, tm, tn):
    t, d = x.shape
    n = w1.shape[1]
    sc_j, sh_j = mod_idx
    mspec = lambda j: pl.BlockSpec((None, None, None, 1, d), _mod_index(layer, j, tiles_per_row, row0))
    return pl.pallas_call(
        _ln_glu_kernel,
        grid=(t // tm, n // tn),
        in_specs=[pl.BlockSpec((tm, d), lambda i, j: (i, 0)),
                  pl.BlockSpec((1, d), lambda i, j: (0, 0)),
                  mspec(sc_j), mspec(sh_j),
                  pl.BlockSpec((d, tn), lambda i, j: (0, j)),
                  pl.BlockSpec((d, tn), lambda i, j: (0, j))],
        out_specs=pl.BlockSpec((tm, tn), lambda i, j: (i, j)),
        out_shape=jax.ShapeDtypeStruct((t, n), BF16),
        scratch_shapes=[pltpu.VMEM((tm, d), BF16)],
        compiler_params=_cparams("parallel", "arbitrary"),
        name="ln_mod_glu",
    )(x, nw.reshape(1, d), mod, mod, w1, w3)


def _merge_kernel(ya_ref, yb_ref, yc_ref, ma_ref, mb_ref, mc_ref, wa_ref, wb_ref, wc_ref, o_ref):
    gate = lambda m_ref: jax.nn.sigmoid(m_ref[...].astype(F32))
    acc = gate(ma_ref) * _dot(ya_ref[...], wa_ref[...])
    acc = acc + gate(mb_ref) * _dot(yb_ref[...], wb_ref[...])
    acc = acc + gate(mc_ref) * _dot(yc_ref[...], wc_ref[...])
    o_ref[...] = acc.astype(o_ref.dtype)


def branch_merge(ya, yb, yc, proj, wa, wb, wc, *, tm, tn):
    t, k = ya.shape
    n = wa.shape[1]
    gate = lambda off: pl.BlockSpec((tm, tn), lambda i, j: (i, off // tn + j))
    yspec = pl.BlockSpec((tm, k), lambda i, j: (i, 0))
    wspec = pl.BlockSpec((k, tn), lambda i, j: (0, j))
    return pl.pallas_call(
        _merge_kernel,
        grid=(t // tm, n // tn),
        in_specs=[yspec, yspec, yspec, gate(OFF_MA), gate(OFF_MA + n), gate(OFF_MA + 2 * n),
                  wspec, wspec, wspec],
        out_specs=pl.BlockSpec((tm, tn), lambda i, j: (i, j)),
        out_shape=jax.ShapeDtypeStruct((t, n), BF16),
        compiler_params=_cparams("parallel", "parallel"),
        name="branch_merge",
    )(ya, yb, yc, proj, proj, proj, wa, wb, wc)


def _mm_resid_kernel(a_ref, w_ref, x_ref, g_ref, o_ref):
    o_ref[...] = x_ref[...] + g_ref[...] * _dot(a_ref[...], w_ref[...])


def matmul_gated_residual(a, w, x, mod, g_j, *, layer, tiles_per_row, row0, tm, tn):
    t, k = a.shape
    n = w.shape[1]
    tn = min(tn, n)
    return pl.pallas_call(
        _mm_resid_kernel,
        grid=(t // tm, n // tn),
        in_specs=[pl.BlockSpec((tm, k), lambda i, j: (i, 0)),
                  pl.BlockSpec((k, tn), lambda i, j: (0, j)),
                  pl.BlockSpec((tm, tn), lambda i, j: (i, j)),
                  pl.BlockSpec((None, None, None, 1, tn),
                               lambda i, j: (layer, row0 + i // tiles_per_row, g_j, 0, j))],
        out_specs=pl.BlockSpec((tm, tn), lambda i, j: (i, j)),
        out_shape=jax.ShapeDtypeStruct((t, n), F32),
        compiler_params=_cparams("parallel", "parallel"),
        name="matmul_gated_residual",
    )(a, w, x, mod)


def _final_norm_kernel(x_ref, fw_ref, o_ref):
    x = x_ref[...]
    o_ref[...] = x * lax.rsqrt(jnp.mean(x * x, axis=-1, keepdims=True) + EPS) * fw_ref[...]


def final_norm(x, fw, *, tm):
    t, d = x.shape
    return pl.pallas_call(
        _final_norm_kernel,
        grid=(t // tm,),
        in_specs=[pl.BlockSpec((tm, d), lambda i: (i, 0)), pl.BlockSpec((1, d), lambda i: (0, 0))],
        out_specs=pl.BlockSpec((tm, d), lambda i: (i, 0)),
        out_shape=jax.ShapeDtypeStruct((t, d), F32),
        compiler_params=_cparams("parallel"),
        name="final_norm",
    )(x, fw.reshape(1, d))


MOE_ROW_TILE = 512
MOE_TOKEN_TILE = 256


_HIGH_HALF = 0xFFFF0000


def _pack_bf16_pairs(x):
    bits = lax.bitcast_convert_type(x.astype(BF16).astype(F32), jnp.uint32)
    half = x.shape[1] // 2
    return (bits[:, :half] & jnp.uint32(_HIGH_HALF)) | (bits[:, half:] >> 16)


def _unpack_bf16_pairs(w):
    hi = lax.bitcast_convert_type(w & jnp.uint32(_HIGH_HALF), F32).astype(BF16)
    lo = lax.bitcast_convert_type(w << 16, F32).astype(BF16)
    return jnp.concatenate([hi, lo], axis=1)


def _router_kernel(x_ref, nw_ref, sc_ref, sh_ref, rw_ref, h_ref, tw_ref, route_ref, cnt_ref, run_ref):
    @pl.when(pl.program_id(0) == 0)
    def _():
        run_ref[...] = jnp.zeros_like(run_ref)

    h = _norm_modulate(x_ref[...], nw_ref[...], sc_ref[...], sh_ref[...])
    h_ref[...] = _pack_bf16_pairs(h)
    tm = h.shape[0]
    lane = lax.broadcasted_iota(jnp.int32, (tm, LANES), 1)
    logits = jnp.where(lane < N_EXPERTS, _dot_f32_3pass(h, rw_ref[...]), -jnp.inf)
    m1 = jnp.max(logits, axis=-1, keepdims=True)
    i1 = jnp.min(jnp.where(logits == m1, lane, LANES), axis=-1, keepdims=True)
    rest = jnp.where(lane == i1, -jnp.inf, logits)
    m2 = jnp.max(rest, axis=-1, keepdims=True)
    i2 = jnp.min(jnp.where(rest == m2, lane, LANES), axis=-1, keepdims=True)
    e2 = jnp.exp(m2 - m1)
    inv = 1.0 / (1.0 + e2)
    tw_ref[...] = jnp.where(lane == 0, inv, jnp.where(lane == 1, e2 * inv, 0.0))

    sel = jnp.where((lane == i1) | (lane == i2), 1.0, 0.0)
    before = lax.broadcasted_iota(jnp.int32, (tm, tm), 1) < lax.broadcasted_iota(jnp.int32, (tm, tm), 0)
    rank = _dot(jnp.where(before, 1.0, 0.0).astype(BF16), sel.astype(BF16)) + run_ref[0:1, :]
    r1 = jnp.sum(jnp.where(lane == i1, rank, 0.0), axis=-1, keepdims=True).astype(jnp.int32)
    r2 = jnp.sum(jnp.where(lane == i2, rank, 0.0), axis=-1, keepdims=True).astype(jnp.int32)
    route_ref[...] = jnp.where(lane == 0, i1, jnp.where(lane == 1, i2, jnp.where(lane == 2, r1, r2)))
    run_ref[...] = run_ref[...] + jnp.sum(sel, axis=0, keepdims=True)
    cnt_ref[...] = run_ref[...]


def router(x, nw, mod, mod_idx, rw, *, layer, tiles_per_row, row0, tm):
    t, d = x.shape
    sc_j, sh_j = mod_idx
    rw_pad = jnp.zeros((d, LANES), F32).at[:, :N_EXPERTS].set(rw)
    mspec = lambda j: pl.BlockSpec((None, None, None, 1, d), _mod_index(layer, j, tiles_per_row, row0))
    return pl.pallas_call(
        _router_kernel,
        grid=(t // tm,),
        in_specs=[pl.BlockSpec((tm, d), lambda i: (i, 0)),
                  pl.BlockSpec((1, d), lambda i: (0, 0)),
                  mspec(sc_j), mspec(sh_j),
                  pl.BlockSpec((d, LANES), lambda i: (0, 0))],
        out_specs=[pl.BlockSpec((tm, d // 2), lambda i: (i, 0)),
                   pl.BlockSpec((tm, LANES), lambda i: (i, 0)),
                   pl.BlockSpec((tm, LANES), lambda i: (i, 0)),
                   pl.BlockSpec((8, LANES), lambda i: (0, 0))],
        out_shape=[jax.ShapeDtypeStruct((t, d // 2), jnp.uint32), jax.ShapeDtypeStruct((t, LANES), F32),
                   jax.ShapeDtypeStruct((t, LANES), jnp.int32), jax.ShapeDtypeStruct((8, LANES), F32)],
        scratch_shapes=[pltpu.VMEM((8, LANES), F32)],
        compiler_params=_cparams("arbitrary"),
        name="router",
    )(x, nw.reshape(1, d), mod, mod, rw_pad)


def _row_copy(src, src_row, dst, dst_row, sem):
    return pltpu.make_async_copy(src.at[pl.ds(src_row, 1), :], dst.at[pl.ds(dst_row, 1), :], sem)


def _dispatch_kernel(pos_ref, ends_ref, h_ref, xs_ref, zero_ref, sem, zsem):
    nt = h_ref.shape[0]
    tm = zero_ref.shape[0]

    @pl.when(pl.program_id(0) == 0)
    def _():
        zero_ref[...] = jnp.zeros_like(zero_ref)

        def last_tile_copy(e):
            row0 = pl.multiple_of((ends_ref[e] - 1) * tm, tm)
            return pltpu.make_async_copy(zero_ref, xs_ref.at[pl.ds(row0, tm), :], zsem)

        def has_rows(e):
            return ends_ref[e] > (ends_ref[e - 1] if e else 0)

        n_tiles = xs_ref.shape[0] // tm

        def spare_tile_copy(k):
            row0 = pl.multiple_of((ends_ref[N_EXPERTS - 1] + k) * tm, tm)
            return pltpu.make_async_copy(zero_ref, xs_ref.at[pl.ds(row0, tm), :], zsem)

        def is_spare(k):
            return ends_ref[N_EXPERTS - 1] + k < n_tiles

        for e in range(N_EXPERTS):
            @pl.when(has_rows(e))
            def _():
                last_tile_copy(e).start()

            @pl.when(is_spare(e))
            def _():
                spare_tile_copy(e).start()
        for e in range(N_EXPERTS):
            @pl.when(has_rows(e))
            def _():
                last_tile_copy(e).wait()

            @pl.when(is_spare(e))
            def _():
                spare_tile_copy(e).wait()

    def start(r, carry):
        for k in range(TOP_K):
            _row_copy(h_ref, r, xs_ref, pos_ref[0, k * nt + r], sem).start()
        return carry

    def wait(r, carry):
        for k in range(TOP_K):
            _row_copy(h_ref, r, xs_ref, pos_ref[0, k * nt + r], sem).wait()
        return carry

    lax.fori_loop(0, nt, start, 0, unroll=8)
    lax.fori_loop(0, nt, wait, 0, unroll=8)


def moe_dispatch(h, pos, tile_ends, rows):
    t, d = h.shape
    nt = MOE_TOKEN_TILE
    return pl.pallas_call(
        _dispatch_kernel,
        grid=(t // nt,),
        in_specs=[pl.BlockSpec((None, 1, TOP_K * nt), lambda i: (i, 0, 0), memory_space=pltpu.SMEM),
                  pl.BlockSpec(memory_space=pltpu.SMEM),
                  pl.BlockSpec((nt, d), lambda i: (i, 0))],
        out_specs=pl.BlockSpec(memory_space=pl.ANY),
        out_shape=jax.ShapeDtypeStruct((rows, d), h.dtype),
        scratch_shapes=[pltpu.VMEM((MOE_ROW_TILE, d), h.dtype), pltpu.SemaphoreType.DMA(()),
                        pltpu.SemaphoreType.DMA(())],
        compiler_params=_cparams("arbitrary"),
        name="moe_dispatch",
    )(pos, tile_ends, h)


def _new_weight_tile(te_ref, i):
    return jnp.logical_or(i == 0, te_ref[i] != te_ref[jnp.maximum(i - 1, 0)])


def _gmm_up_kernel(te_ref, nu_ref, xs_ref, w1_ref, w3_ref, o_ref, w1b_ref, w3b_ref):
    i = pl.program_id(1)
    used = i < nu_ref[0]

    @pl.when(_new_weight_tile(te_ref, i))
    def _():
        w1b_ref[...] = w1_ref[...].astype(BF16)
        w3b_ref[...] = w3_ref[...].astype(BF16)

    @pl.when(used)
    def _():
        a = _unpack_bf16_pairs(xs_ref[...])
        o_ref[...] = (_silu(_dot(a, w1b_ref[...])) * _dot(a, w3b_ref[...])).astype(o_ref.dtype)

    @pl.when(jnp.logical_not(used))
    def _():
        o_ref[...] = jnp.zeros_like(o_ref)


def gmm_up(xs, w1, w3, tile_expert, n_used, *, tn):
    rows = xs.shape[0]
    _, d, n = w1.shape
    tm = MOE_ROW_TILE
    wspec = pl.BlockSpec((None, d, tn), lambda j, i, te, nu: (te[i], 0, j))
    return pl.pallas_call(
        _gmm_up_kernel,
        grid_spec=pltpu.PrefetchScalarGridSpec(
            num_scalar_prefetch=2,
            grid=(n // tn, rows // tm),
            in_specs=[pl.BlockSpec((tm, d // 2), lambda j, i, te, nu: (jnp.minimum(i, nu[0] - 1), 0)),
                      wspec, wspec],
            out_specs=pl.BlockSpec((tm, tn), lambda j, i, te, nu: (i, j)),
            scratch_shapes=[pltpu.VMEM((d, tn), BF16), pltpu.VMEM((d, tn), BF16)]),
        out_shape=jax.ShapeDtypeStruct((rows, n), BF16),
        compiler_params=_cparams("arbitrary", "arbitrary"),
        name="gmm_up",
    )(tile_expert, n_used, xs, w1, w3)


def _gmm_down_kernel(te_ref, nu_ref, f_ref, w_ref, o_ref, wb_ref):
    i = pl.program_id(1)
    used = i < nu_ref[0]

    @pl.when(_new_weight_tile(te_ref, i))
    def _():
        wb_ref[...] = w_ref[...].astype(BF16)

    @pl.when(used)
    def _():
        o_ref[...] = _dot(f_ref[...], wb_ref[...])

    @pl.when(jnp.logical_not(used))
    def _():
        o_ref[...] = jnp.zeros_like(o_ref)


def gmm_down(f, w2, tile_expert, n_used, *, tn):
    rows, k = f.shape
    n = w2.shape[2]
    tm = MOE_ROW_TILE
    return pl.pallas_call(
        _gmm_down_kernel,
        grid_spec=pltpu.PrefetchScalarGridSpec(
            num_scalar_prefetch=2,
            grid=(n // tn, rows // tm),
            in_specs=[pl.BlockSpec((tm, k), lambda j, i, te, nu: (jnp.minimum(i, nu[0] - 1), 0)),
                      pl.BlockSpec((None, k, tn), lambda j, i, te, nu: (te[i], 0, j))],
            out_specs=pl.BlockSpec((tm, tn), lambda j, i, te, nu: (i, j)),
            scratch_shapes=[pltpu.VMEM((k, tn), BF16)]),
        out_shape=jax.ShapeDtypeStruct((rows, n), F32),
        compiler_params=_cparams("arbitrary", "arbitrary"),
        name="gmm_down",
    )(tile_expert, n_used, f, w2)


def _combine_kernel(final, pos_ref, nxt_ref, x_ref, tw_ref, g_ref, fw_ref, ys_ref, o_ref, y_ref, sems):
    nt = x_ref.shape[0]
    i = pl.program_id(0)
    slot = lax.rem(i, 2)

    def gather(p_ref, s):
        def copies(r):
            return [_row_copy(ys_ref, p_ref[0, k * nt + r], y_ref.at[s, k], r, sems.at[s])
                    for k in range(TOP_K)]
        return copies

    def start_all(copies):
        def body(r, carry):
            for c in copies(r):
                c.start()
            return carry
        lax.fori_loop(0, nt, body, 0, unroll=8)

    def wait_all(copies):
        def body(r, carry):
            for c in copies(r):
                c.wait()
            return carry
        lax.fori_loop(0, nt, body, 0, unroll=8)

    @pl.when(i == 0)
    def _():
        start_all(gather(pos_ref, 0))

    @pl.when(i + 1 < pl.num_programs(0))
    def _():
        start_all(gather(nxt_ref, 1 - slot))

    wait_all(gather(pos_ref, slot))
    tw = tw_ref[...]
    x = x_ref[...] + g_ref[...] * (tw[:, 0:1] * y_ref[slot, 0] + tw[:, 1:2] * y_ref[slot, 1])
    if final:
        x = x * lax.rsqrt(jnp.mean(x * x, axis=-1, keepdims=True) + EPS) * fw_ref[...]
    o_ref[...] = x


def moe_combine(x, ys, pos, tw, mod, g_j, fw, *, layer, tiles_per_row, row0, final):
    t, d = x.shape
    nt = MOE_TOKEN_TILE
    n = t // nt
    pos_spec = lambda index: pl.BlockSpec((None, 1, TOP_K * nt), index, memory_space=pltpu.SMEM)
    return pl.pallas_call(
        functools.partial(_combine_kernel, final),
        grid=(n,),
        in_specs=[pos_spec(lambda i: (i, 0, 0)),
                  pos_spec(lambda i: (jnp.minimum(i + 1, n - 1), 0, 0)),
                  pl.BlockSpec((nt, d), lambda i: (i, 0)),
                  pl.BlockSpec((nt, LANES), lambda i: (i, 0)),
                  pl.BlockSpec((None, None, None, 1, d), _mod_index(layer, g_j, tiles_per_row, row0)),
                  pl.BlockSpec((1, d), lambda i: (0, 0)),
                  pl.BlockSpec(memory_space=pl.ANY)],
        out_specs=pl.BlockSpec((nt, d), lambda i: (i, 0)),
        out_shape=jax.ShapeDtypeStruct((t, d), F32),
        scratch_shapes=[pltpu.VMEM((2, TOP_K, nt, d), F32), pltpu.SemaphoreType.DMA((2,))],
        compiler_params=_cparams("arbitrary"),
        name="moe_combine",
    )(pos, pos, x, tw, mod, fw.reshape(1, d), ys)


def _moe_plan(route, counts, t):
    tm = MOE_ROW_TILE
    nt = MOE_TOKEN_TILE
    n_tiles = TOP_K * t // tm + N_EXPERTS
    cnt = counts[0, :N_EXPERTS].astype(jnp.int32)
    tiles = (cnt + tm - 1) // tm
    ends = jnp.cumsum(tiles)
    offs = (ends - tiles) * tm
    pos = [jnp.take(offs, route[:, k]) + route[:, TOP_K + k] for k in range(TOP_K)]
    pos = jnp.concatenate([p.reshape(t // nt, nt) for p in pos], axis=1).reshape(t // nt, 1, TOP_K * nt)
    tile_expert = jnp.sum(jnp.arange(n_tiles, dtype=jnp.int32)[:, None] >= ends[None, :], axis=1)
    tile_expert = jnp.minimum(tile_expert, N_EXPERTS - 1).astype(jnp.int32)
    ends = ends.astype(jnp.int32)
    return pos, tile_expert, ends, ends[-1:], n_tiles * tm


def _hyena_filter_kernel(z_ref, w1_ref, b1_ref, fr_ref, w2_ref, b2_ref, w3_ref, t_ref, dec_ref, o_ref):
    fr = fr_ref[...]
    h = jnp.sin(fr * (_dot_f32(z_ref[...], w1_ref[...]) + b1_ref[...]))
    h = jnp.sin(fr * (_dot_f32(h, w2_ref[...]) + b2_ref[...]))
    o_ref[...] = _dot_f32(h, w3_ref[...]) * jnp.exp(-t_ref[...] * jnp.abs(dec_ref[...]))


def hyena_filter(seq, w1, b1, freq, w2, b2, w3, decay, tl=256):
    t = jnp.linspace(0.0, 1.0, seq, dtype=F32)[:, None]
    pos = jnp.arange(seq, dtype=F32)[:, None]
    bands = jnp.linspace(1e-4, HY_BANDS - 1.0, HY_BANDS, dtype=F32)[None, :]
    ang = (2.0 * math.pi / seq) * pos * bands
    z = jnp.concatenate([t, jnp.cos(ang), -jnp.sin(ang)], axis=-1)
    emb = z.shape[1]
    emb_pad = LANES
    z = jnp.pad(z, ((0, 0), (0, emb_pad - emb)))
    w1p = jnp.pad(w1, ((0, emb_pad - emb), (0, 0)))
    hid = w1.shape[1]
    n = w3.shape[1]
    tl = min(tl, seq)
    full = lambda shape: pl.BlockSpec(shape, lambda i: (0, 0))
    return pl.pallas_call(
        _hyena_filter_kernel,
        grid=(seq // tl,),
        in_specs=[pl.BlockSpec((tl, emb_pad), lambda i: (i, 0)),
                  full((emb_pad, hid)), full((1, hid)), full((1, hid)),
                  full((hid, hid)), full((1, hid)), full((hid, n)),
                  pl.BlockSpec((tl, 1), lambda i: (i, 0)), full((1, n))],
        out_specs=pl.BlockSpec((tl, n), lambda i: (i, 0)),
        out_shape=jax.ShapeDtypeStruct((seq, n), F32),
        compiler_params=_cparams("parallel"),
        name="hyena_filter",
    )(z, w1p, b1.reshape(1, hid), freq.reshape(1, hid), w2, b2.reshape(1, hid), w3, t,
      decay.reshape(1, n))


def _dft_tables(seq, kb):
    n = 2 * seq
    k = jnp.arange(seq, dtype=jnp.int32)[:, None]
    s = jnp.arange(seq, dtype=jnp.int32)[None, :]
    ang = ((k * s) % n).astype(F32) * (2.0 * math.pi / n)
    cos = jnp.cos(ang)
    sin = jnp.sin(ang)
    nyq = jnp.where(s % 2 == 0, 1.0, -1.0).astype(F32)
    is0 = k == 0
    f_re = cos
    f_im = jnp.where(is0, nyq, -sin)
    i_re = jnp.where(is0, 1.0 / n, (2.0 / n) * cos)
    i_im = jnp.where(is0, nyq / n, -(2.0 / n) * sin)
    nkb = seq // kb
    fwd = jnp.concatenate([f_re.reshape(nkb, kb, seq), f_im.reshape(nkb, kb, seq)], axis=1)
    inv = jnp.concatenate([i_re.reshape(nkb, kb, seq), i_im.reshape(nkb, kb, seq)], axis=1)
    return fwd, jnp.swapaxes(inv, 1, 2)


def _spectrum_kernel(kb, f_ref, hf_ref, hb_ref, bias_ref, o_ref):
    f = f_ref[...]
    row = lax.broadcasted_iota(jnp.int32, hb_ref.shape, 0)
    hb0 = jnp.where(row == 0, 0.0, hb_ref[...])
    a = _dot_f32_3pass(f, hf_ref[...])
    b = _dot_f32_3pass(f, hb0)
    orow = lax.broadcasted_iota(jnp.int32, a.shape, 0)
    nyq_slot = (orow == kb) & (pl.program_id(0) == 0)
    o_ref[...] = jnp.where((orow < kb) | nyq_slot, a + b + bias_ref[...], a - b)


def hyena_spectrum(fwd_f32, filt, bias, kb, tc=256):
    nkb, kb2, seq = fwd_f32.shape
    c = filt.shape[1] // 2
    return pl.pallas_call(
        functools.partial(_spectrum_kernel, kb),
        grid=(nkb, c // tc),
        in_specs=[pl.BlockSpec((None, kb2, seq), lambda j, i: (j, 0, 0)),
                  pl.BlockSpec((seq, tc), lambda j, i: (0, i)),
                  pl.BlockSpec((seq, tc), lambda j, i: (0, c // tc + i)),
                  pl.BlockSpec((1, tc), lambda j, i: (0, i))],
        out_specs=pl.BlockSpec((None, kb2, tc), lambda j, i: (j, 0, i)),
        out_shape=jax.ShapeDtypeStruct((nkb, kb2, c), F32),
        compiler_params=_cparams("parallel", "parallel"),
        name="hyena_spectrum",
    )(fwd_f32, filt, filt, bias.reshape(1, c))


def _hyena_conv_kernel(kb, x0_ref, x1_ref, v_ref, cw0_ref, cw1_ref, cwv_ref, cb0_ref, cb1_ref, cbv_ref,
                       f_ref, i_ref, kf_ref, o_ref, u16_ref, acc_ref):
    j = pl.program_id(2)
    seq = x0_ref.shape[0]

    def conv3(x_ref, w_ref, b_ref):
        x = x_ref[...].astype(F32)
        row = lax.broadcasted_iota(jnp.int32, x.shape, 0)
        prev = jnp.where(row == 0, 0.0, pltpu.roll(x, 1, 0))
        nxt = jnp.where(row == seq - 1, 0.0, pltpu.roll(x, seq - 1, 0))
        w = w_ref[...]
        return prev * w[0:1, :] + x * w[1:2, :] + nxt * w[2:3, :] + b_ref[...]

    @pl.when(j == 0)
    def _():
        u16_ref[...] = (conv3(v_ref, cwv_ref, cbv_ref) * conv3(x1_ref, cw1_ref, cb1_ref)).astype(BF16)
        acc_ref[...] = jnp.zeros_like(acc_ref)

    spec = _dot(f_ref[...], u16_ref[...])
    xr, xi = spec[:kb], spec[kb:]
    kf = kf_ref[...]
    kr, ki = kf[:kb], kf[kb:]
    packed = (lax.broadcasted_iota(jnp.int32, xr.shape, 0) == 0) & (j == 0)
    yr = xr * kr - jnp.where(packed, 0.0, xi * ki)
    yi = jnp.where(packed, xi * ki, xr * ki + xi * kr)
    y = jnp.concatenate([yr, yi], axis=0).astype(BF16)
    acc_ref[...] += _dot(i_ref[...], y)

    @pl.when(j == pl.num_programs(2) - 1)
    def _():
        o_ref[...] = (acc_ref[...] * conv3(x0_ref, cw0_ref, cb0_ref)).astype(o_ref.dtype)


def hyena_conv(proj, conv_w, conv_b, fwd, inv, spectrum, *, batch, seq, ct, kb):
    c = HY_WIDTH
    nkb = seq // kb
    ncb = c // ct
    col = lambda part: pl.BlockSpec((seq, ct), lambda b, i, j: (b, part * ncb + i))
    cw = lambda part: pl.BlockSpec((3, ct), lambda b, i, j: (0, part * ncb + i))
    cb = lambda part: pl.BlockSpec((1, ct), lambda b, i, j: (0, part * ncb + i))
    conv_b = conv_b.reshape(1, 3 * c)
    return pl.pallas_call(
        functools.partial(_hyena_conv_kernel, kb),
        grid=(batch, ncb, nkb),
        in_specs=[col(0), col(1), col(2), cw(0), cw(1), cw(2), cb(0), cb(1), cb(2),
                  pl.BlockSpec((None, 2 * kb, seq), lambda b, i, j: (j, 0, 0)),
                  pl.BlockSpec((None, seq, 2 * kb), lambda b, i, j: (j, 0, 0)),
                  pl.BlockSpec((None, 2 * kb, ct), lambda b, i, j: (j, 0, i))],
        out_specs=pl.BlockSpec((seq, ct), lambda b, i, j: (b, i)),
        out_shape=jax.ShapeDtypeStruct((batch * seq, c), BF16),
        scratch_shapes=[pltpu.VMEM((seq, ct), BF16), pltpu.VMEM((seq, ct), F32)],
        compiler_params=_cparams("parallel", "parallel", "arbitrary"),
        name="hyena_conv",
    )(proj, proj, proj, conv_w, conv_w, conv_w, conv_b, conv_b, conv_b, fwd, inv, spectrum)


def _rope(x, cos, sin_a, sin_b):
    return x * cos + pltpu.roll(x, HEAD_DIM - HEAD_DIM // 4, 1) * sin_a + pltpu.roll(x, HEAD_DIM // 4, 1) * sin_b


def _dot_nt(a, b):
    return lax.dot_general(a, b, (((1,), (1,)), ((), ())), preferred_element_type=F32)


LOG2E = 1.0 / math.log(2.0)
LOGIT_SCALE = ATTN_SCALE * LOG2E


def _grouped_softmax_pv(qs, keys, values, sinks, valid, o_ref, score_scale):
    scores = [_dot_nt(q, keys) for q in qs]
    probs, dens = [], []
    for s, sink in zip(scores, sinks):
        if score_scale is not None:
            s = s * score_scale
        if valid is not None:
            s = jnp.where(valid, s, NEG_BIG)
        sink2 = sink * LOG2E
        m = jnp.maximum(jnp.max(s, axis=-1, keepdims=True), sink2)
        p = jnp.exp2(s - m)
        dens.append(jnp.sum(p, axis=-1, keepdims=True) + jnp.exp2(sink2 - m))
        probs.append(p.astype(BF16))
    outs = [_dot(p, values) for p in probs]
    for g, (o, den) in enumerate(zip(outs, dens)):
        o_ref[:, g * HEAD_DIM:(g + 1) * HEAD_DIM] = (o / den).astype(o_ref.dtype)


def _ctx_attn_kernel(sink_ref, q_ref, k_ref, v_ref, o_ref):
    kvh = pl.program_id(1)
    qs = [q_ref[:, g * HEAD_DIM:(g + 1) * HEAD_DIM] for g in range(GROUP)]
    sinks = [sink_ref[kvh * GROUP + g] for g in range(GROUP)]
    _grouped_softmax_pv(qs, k_ref[...].astype(BF16), v_ref[...].astype(BF16), sinks, None, o_ref,
                        LOGIT_SCALE)


def context_attention(proj, proj32, sink, *, batch, seq):
    qw = GROUP * HEAD_DIM
    kv = lambda part: pl.BlockSpec((seq, HEAD_DIM),
                                   lambda b, h: (b, OFF32_KV // HEAD_DIM + part * N_KV_HEADS + h))
    return pl.pallas_call(
        _ctx_attn_kernel,
        grid=(batch, N_KV_HEADS),
        in_specs=[pl.BlockSpec(memory_space=pltpu.SMEM),
                  pl.BlockSpec((seq, qw), lambda b, h: (b, OFF_AQ // qw + h)),
                  kv(0), kv(1)],
        out_specs=pl.BlockSpec((seq, qw), lambda b, h: (b, h)),
        out_shape=jax.ShapeDtypeStruct((batch * seq, ATTN_WIDTH), BF16),
        compiler_params=_cparams("parallel", "parallel"),
        name="context_attention",
    )(sink, proj, proj32, proj32)


def _lat_attn_kernel(seq, sink_ref, q_ref, k_ref, v_ref, ck_ref, cv_ref, cos_ref, sa_ref, sb_ref,
                     o_ref, kr_ref, vb_ref):
    kvh = pl.program_id(1)
    qb = pl.program_id(2)
    blk = q_ref.shape[0]
    nwin = 3 * blk
    past = ck_ref.shape[0]

    @pl.when(qb == 0)
    def _():
        kr_ref[...] = _rope(k_ref[...], cos_ref[...], sa_ref[...], sb_ref[...]).astype(BF16)
        vb_ref[...] = v_ref[...].astype(BF16)

    start = pl.multiple_of(jnp.clip((qb - 1) * blk, 0, seq - nwin), blk)
    keys = jnp.concatenate([kr_ref[pl.ds(start, nwin), :], ck_ref[...].astype(BF16)], axis=0)
    values = jnp.concatenate([vb_ref[pl.ds(start, nwin), :], cv_ref[...].astype(BF16)], axis=0)
    rows = pl.ds(pl.multiple_of(qb * blk, blk), blk)
    cos, sa, sb = cos_ref[rows, :], sa_ref[rows, :], sb_ref[rows, :]
    qpos = qb * blk + lax.broadcasted_iota(jnp.int32, (blk, nwin + past), 0)
    col = lax.broadcasted_iota(jnp.int32, (blk, nwin + past), 1)
    valid = (col >= nwin) | (jnp.abs(qpos - (start + col)) <= WINDOW)
    qs = [(_rope(q_ref[:, g * HEAD_DIM:(g + 1) * HEAD_DIM].astype(F32), cos, sa, sb) * LOGIT_SCALE).astype(BF16)
          for g in range(GROUP)]
    sinks = [sink_ref[kvh * GROUP + g] for g in range(GROUP)]
    _grouped_softmax_pv(qs, keys, values, sinks, valid, o_ref, None)


def _rope_tables(seq):
    rows = seq // GRID_W
    row = jnp.repeat(jnp.arange(rows, dtype=F32), GRID_W)
    col = jnp.tile(jnp.arange(GRID_W, dtype=F32), rows)
    quarter = HEAD_DIM // 4
    inv = ROPE_BASE ** (-jnp.arange(quarter, dtype=F32) / quarter)
    ar = row[:, None] * inv
    ac = col[:, None] * inv
    ang = jnp.concatenate([ar, ar, ac, ac], axis=-1)
    cos, sin = jnp.cos(ang), jnp.sin(ang)
    first = (jnp.arange(HEAD_DIM) % (2 * quarter)) < quarter
    return cos, jnp.where(first, -sin, 0.0), jnp.where(first, 0.0, sin)


def latent_attention(proj, proj32, cache_k, cache_v, sink, *, layer, batch, seq, blk=128):
    qw = GROUP * HEAD_DIM
    past = cache_k.shape[2]
    nqb = seq // blk
    cos, sa, sb = _rope_tables(seq)
    table = pl.BlockSpec((seq, HEAD_DIM), lambda b, h, i: (0, 0))
    cache = pl.BlockSpec((None, None, past, HEAD_DIM), lambda b, h, i: (b, layer, 0, h))
    kv = lambda part: pl.BlockSpec((seq, HEAD_DIM),
                                   lambda b, h, i: (b, OFF32_KV // HEAD_DIM + part * N_KV_HEADS + h))
    return pl.pallas_call(
        functools.partial(_lat_attn_kernel, seq),
        grid=(batch, N_KV_HEADS, nqb),
        in_specs=[pl.BlockSpec(memory_space=pltpu.SMEM),
                  pl.BlockSpec((blk, qw), lambda b, h, i: (b * nqb + i, OFF_AQ // qw + h)),
                  kv(0), kv(1),
                  cache, cache, table, table, table],
        out_specs=pl.BlockSpec((blk, qw), lambda b, h, i: (b * nqb + i, h)),
        out_shape=jax.ShapeDtypeStruct((batch * seq, ATTN_WIDTH), BF16),
        scratch_shapes=[pltpu.VMEM((seq, HEAD_DIM), BF16), pltpu.VMEM((seq, HEAD_DIM), BF16)],
        compiler_params=_cparams("parallel", "parallel", "arbitrary"),
        name="latent_attention",
    )(sink, proj, proj32, proj32, cache_k, cache_v, cos, sa, sb)


_HG_LEVELS = tuple(HG_CHUNK >> (i + 1) for i in range(int(math.log2(HG_CHUNK))))


def _hgrn_tables():
    c = HG_CHUNK
    t = np.arange(c)[:, None]
    u = np.arange(c)[None, :]
    blocks = [(u <= t), (u > t)]
    for m in _HG_LEVELS:
        ref = (t // (2 * m)) * (2 * m) + m - 1
        second = (t % (2 * m)) >= m
        blocks.append(np.where(second, (u > ref) & (u <= t), (u > t) & (u <= ref)))
    fwd = np.concatenate(blocks, axis=0).astype(np.float32)
    bwd = np.concatenate([b[::-1, ::-1] for b in blocks], axis=0).astype(np.float32)
    s = u
    level = np.full((c, c), len(_HG_LEVELS) + 1, np.int32)
    level[t == s] = len(_HG_LEVELS)
    for i, m in enumerate(_HG_LEVELS):
        hit = (t // (2 * m) == s // (2 * m)) & ((t % (2 * m)) >= m) & ((s % (2 * m)) < m)
        level[hit] = i
    a = np.stack([fwd, bwd])
    a = np.concatenate([a, a], axis=2)
    lv = np.stack([level, level.T])
    return jnp.asarray(a, BF16), jnp.asarray(lv, jnp.int32)


def _hgrn_kernel(layer, has_s0, seq, heads, *refs):
    if has_s0:
        (q_ref, ff_ref, fb_ref, i_ref, g_ref, lb_ref, nw_ref, a_ref, lv_ref, s0_ref,
         y_ref, sfin_ref, of_ref, ob_ref, st_ref) = refs
    else:
        (q_ref, ff_ref, fb_ref, i_ref, g_ref, lb_ref, nw_ref, a_ref, lv_ref,
         y_ref, sfin_ref, of_ref, ob_ref, st_ref) = refs
    c = HG_CHUNK
    nlev = len(_HG_LEVELS)
    nc = seq // c

    lbs = lb_ref[...]
    mx = jnp.max(lbs, axis=0, keepdims=True)
    ex = jnp.exp(lbs - mx)
    sm = ex / jnp.sum(ex, axis=0, keepdims=True)
    lb = jnp.zeros(sm.shape[1:], F32)
    for j in range(1, layer + 1):
        lb = lb + sm[j]

    for d in range(2):
        for h in range(heads):
            if has_s0:
                st_ref[d, h] = s0_ref[d, h].T
            else:
                st_ref[d, h] = jnp.zeros((HG_DV, HG_DK), F32)

    def decay_exponents(d, ci):
        rows = pl.ds(pl.multiple_of(ci * c, c), c)
        fpre = (ff_ref if d == 0 else fb_ref)[rows, :]
        lbd = lb[d:d + 1, :]
        f = jnp.maximum(lbd, LB_FLOOR) + (1.0 - lbd) * jax.nn.sigmoid(fpre)
        log2f = jnp.log(f) * (1.0 / math.log(2.0))
        args = _dot(a_ref[d], jnp.concatenate(_split3(log2f)[:2], axis=0))
        chains = []
        for h in range(heads):
            cols = slice(h * HG_DK, (h + 1) * HG_DK)
            q = _silu(q_ref[rows, cols].astype(F32)) * (HG_DK ** -0.5)
            chains.append(dict(d=d, h=h, rows=rows, cols=cols, q=q, k=1.0 - f[:, cols],
                               v=i_ref[rows, cols].astype(BF16), args=args[:, cols]))
        return chains

    def level_products(s):
        q, k = s['q'], s['k']
        e = jnp.exp2(s['args'])
        s['q_in'] = (q * e[0:c]).astype(BF16)
        s['k_out'] = (k * e[c:2 * c]).astype(BF16)
        last = (c - 1) if s['d'] == 0 else 0
        s['total'] = e[last:last + 1]
        prods = [_dot_nt((q * e[(2 + i) * c:(3 + i) * c]).astype(BF16), (k * e[(2 + i) * c:(3 + i) * c]).astype(BF16))
                 for i in range(nlev)]
        s['prods'] = prods + [_dot_nt(q.astype(BF16), k.astype(BF16))]
        return s

    def outputs_and_state(s, o_ref):
        d, h = s['d'], s['h']
        lv = lv_ref[d]
        att = jnp.zeros((c, c), F32)
        for i in range(nlev + 1):
            att = jnp.where(lv == i, s['prods'][i], att)
        st = st_ref[d, h]
        o_ref[s['rows'], s['cols']] = _dot_nt(s['q_in'], st.astype(BF16)) + _dot(att.astype(BF16), s['v'])
        upd = lax.dot_general(s['v'], s['k_out'], (((0,), (0,)), ((), ())), preferred_element_type=F32)
        st_ref[d, h] = st * s['total'] + upd

    def body(ci, carry):
        stage = decay_exponents(0, ci) + decay_exponents(1, nc - 1 - ci)
        stage = [level_products(s) for s in stage]
        for s in stage:
            outputs_and_state(s, of_ref if s['d'] == 0 else ob_ref)
        return carry

    lax.fori_loop(0, nc, body, 0, unroll=4)

    for h in range(heads):
        cols = slice(h * HG_DK, (h + 1) * HG_DK)
        o = of_ref[:, cols] + ob_ref[:, cols]
        o = o * lax.rsqrt(jnp.mean(o * o, axis=-1, keepdims=True) + EPS) * nw_ref[...] * _silu(g_ref[:, cols].astype(F32))
        y_ref[:, cols] = o.astype(y_ref.dtype)
        for d in range(2):
            sfin_ref[d, h] = st_ref[d, h].T


HG_HEADS_PER_STEP = 2


def hgrn2_mix(proj, proj32, hg_lb, norm_w, s0, *, layer, batch, seq):
    a_tab, lv_tab = _hgrn_tables()
    depth = hg_lb.shape[0]
    hps = HG_HEADS_PER_STEP
    w = hps * HG_DK
    col = lambda off: pl.BlockSpec((seq, w), lambda b, h: (b, off // w + h))
    state = pl.BlockSpec((None, 2, hps, HG_DK, HG_DV), lambda b, h: (b, 0, h, 0, 0))
    in_specs = [col(OFF_HQ), col(OFF32_FF), col(OFF32_FB), col(OFF_HI), col(OFF_HG),
                pl.BlockSpec((depth, 2, w), lambda b, h: (0, 0, h)),
                pl.BlockSpec((1, HG_DV), lambda b, h: (0, 0)),
                pl.BlockSpec(a_tab.shape, lambda b, h: (0, 0, 0)),
                pl.BlockSpec(lv_tab.shape, lambda b, h: (0, 0, 0))]
    args = [proj, proj32, proj32, proj, proj, hg_lb, norm_w.reshape(1, HG_DV), a_tab, lv_tab]
    if s0 is not None:
        in_specs.append(state)
        args.append(s0)
    return pl.pallas_call(
        functools.partial(_hgrn_kernel, layer, s0 is not None, seq, hps),
        grid=(batch, HG_HEADS // hps),
        in_specs=in_specs,
        out_specs=[pl.BlockSpec((seq, w), lambda b, h: (b, h)), state],
        out_shape=[jax.ShapeDtypeStruct((batch * seq, HG_WIDTH), BF16),
                   jax.ShapeDtypeStruct((batch, 2, HG_HEADS, HG_DK, HG_DV), F32)],
        scratch_shapes=[pltpu.VMEM((seq, w), F32), pltpu.VMEM((seq, w), F32),
                        pltpu.VMEM((2, hps, HG_DV, HG_DK), F32)],
        compiler_params=_cparams("parallel", "parallel"),
        name="hgrn2_mix",
    )(*args)


ROW_TILE = 1024
COL_TILE = 512
OUT_PROJ_COL_TILE = 1024
ROUTER_TILE = 512
HYENA_CHANNEL_TILE = 512
HYENA_FREQ_BLOCK = 512


def _trunk_layer(x, l, p, *, batch, seq, mod, row0, ctx, final):
    t, d = x.shape
    tm = ROW_TILE
    rows_per_cond = t if ctx is None else seq
    geo = lambda tile: dict(layer=l, tiles_per_row=rows_per_cond // tile, row0=row0)

    proj, proj32 = ln_mod_matmul(x, p['norm1_w'][l], mod, (1, 0), p['w_in'][l], tm=tm, **geo(tm))

    filt = hyena_filter(seq, p['hy_w1'][l], p['hy_b1'][l], p['hy_freq'][l], p['hy_w2'][l], p['hy_b2'][l],
                        p['hy_w3'][l], p['hy_decay'][l])
    kb = min(seq, HYENA_FREQ_BLOCK)
    fwd, inv = _dft_tables(seq, kb)
    spectrum = hyena_spectrum(fwd, filt, p['hy_bias'][l], kb)
    ya = hyena_conv(proj, p['hy_conv_w'][l], p['hy_conv_b'][l], fwd.astype(BF16), inv.astype(BF16),
                    spectrum, batch=batch, seq=seq, ct=HYENA_CHANNEL_TILE, kb=kb)

    if ctx is None:
        yb = context_attention(proj, proj32, p['attn_sink'][l], batch=batch, seq=seq)
        s0 = None
    else:
        cache_k, cache_v, s0 = ctx
        yb = latent_attention(proj, proj32, cache_k, cache_v, p['attn_sink'][l], layer=l, batch=batch,
                              seq=seq)
        s0 = s0[:, l]

    yc, s_fin = hgrn2_mix(proj, proj32, p['hg_lb'], p['hg_norm_w'][l], s0, layer=l, batch=batch, seq=seq)

    mixed = branch_merge(ya, yb, yc, proj, p['w_branch_a'][l], p['w_branch_b'][l], p['w_branch_c'][l],
                         tm=tm, tn=COL_TILE)
    x = matmul_gated_residual(mixed, p['w_out'][l], x, mod, 2, tm=tm, tn=OUT_PROJ_COL_TILE, **geo(tm))

    j = l // 2
    if l % 2 == 0:
        f = ln_mod_glu(x, p['norm2_w'][l], mod, (4, 3), p['ffn_w1'][j], p['ffn_w3'][j], tm=tm, tn=COL_TILE,
                       **geo(tm))
        x = matmul_gated_residual(f, p['ffn_w2'][j], x, mod, 5, tm=tm, tn=COL_TILE, **geo(tm))
        if final:
            x = final_norm(x, p['final_norm_w'], tm=ROUTER_TILE)
    else:
        h2, tw, route, counts = router(x, p['norm2_w'][l], mod, (4, 3), p['router_w'][j], tm=ROUTER_TILE,
                                       **geo(ROUTER_TILE))
        pos, tile_expert, tile_ends, n_used, rows = _moe_plan(route, counts, t)
        xs = moe_dispatch(h2, pos, tile_ends, rows)
        f = gmm_up(xs, p['moe_w1'][j], p['moe_w3'][j], tile_expert, n_used, tn=COL_TILE)
        ys = gmm_down(f, p['moe_w2'][j], tile_expert, n_used, tn=COL_TILE)
        x = moe_combine(x, ys, pos, tw, mod, 5, p['final_norm_w'], final=final, **geo(MOE_TOKEN_TILE))
    k = proj32[:, OFF32_KV:OFF32_KV + KV_WIDTH]
    v = proj32[:, OFF32_KV + KV_WIDTH:OFF32_KV + 2 * KV_WIDTH]
    return x, k, v, s_fin


def kernel(x_prompt, x_sample, c, cache_k, cache_v, state_hgrn, c_ctx, ada_w, ada_b, norm1_w, norm2_w, w_in,
           hy_conv_w, hy_conv_b, hy_w1, hy_b1, hy_freq, hy_w2, hy_b2, hy_w3, hy_decay, hy_bias, attn_sink,
           hg_lb, hg_norm_w, w_branch_a, w_branch_b, w_branch_c, w_out, ffn_w1, ffn_w3, ffn_w2, router_w,
           moe_w1, moe_w3, moe_w2, final_norm_w):
    batch, seq, d = x_prompt.shape
    dbatch, dseq, _ = x_sample.shape
    depth = ada_w.shape[0]
    bf = lambda a: a.astype(BF16)
    p = dict(norm1_w=norm1_w, norm2_w=norm2_w, w_in=bf(w_in), hy_conv_w=hy_conv_w, hy_conv_b=hy_conv_b,
             hy_w1=hy_w1, hy_b1=hy_b1, hy_freq=hy_freq, hy_w2=hy_w2, hy_b2=hy_b2, hy_w3=hy_w3,
             hy_decay=hy_decay, hy_bias=hy_bias, attn_sink=attn_sink, hg_lb=hg_lb, hg_norm_w=hg_norm_w,
             w_branch_a=bf(w_branch_a), w_branch_b=bf(w_branch_b), w_branch_c=bf(w_branch_c),
             w_out=bf(w_out), ffn_w1=bf(ffn_w1), ffn_w3=bf(ffn_w3), ffn_w2=bf(ffn_w2), router_w=router_w,
             moe_w1=moe_w1, moe_w3=moe_w3, moe_w2=moe_w2, final_norm_w=final_norm_w)

    nrows = 16
    cond = jnp.zeros((nrows, d), F32).at[:dbatch].set(c).at[dbatch].set(c_ctx)
    mod = ada_modulation(cond, ada_w, ada_b).reshape(depth, nrows, 6, 1, d)

    xp = x_prompt.reshape(batch * seq, d)
    ks, vs, ss = [], [], []
    for l in range(depth):
        xp, k_l, v_l, s_l = _trunk_layer(xp, l, p, batch=batch, seq=seq, mod=mod, row0=dbatch, ctx=None,
                                         final=(l == depth - 1))
        ks.append(k_l.reshape(batch, seq, N_KV_HEADS, HEAD_DIM))
        vs.append(v_l.reshape(batch, seq, N_KV_HEADS, HEAD_DIM))
        ss.append(s_l)
    y_prompt = xp.reshape(batch, seq, d)
    new_cache_k = jnp.stack(ks, axis=1)
    new_cache_v = jnp.stack(vs, axis=1)
    new_state = jnp.stack(ss, axis=1)

    past = cache_k.shape[2]
    ck = cache_k.reshape(dbatch, depth, past, KV_WIDTH)
    cv = cache_v.reshape(dbatch, depth, past, KV_WIDTH)
    xs = x_sample.reshape(dbatch * dseq, d)
    for l in range(depth):
        xs, _, _, _ = _trunk_layer(xs, l, p, batch=dbatch, seq=dseq, mod=mod, row0=0,
                                   ctx=(ck, cv, state_hgrn), final=(l == depth - 1))
    y_sample = xs.reshape(dbatch, dseq, d)
    return (y_prompt, y_sample, new_cache_k, new_cache_v, new_state)
```
